```python
import math
import jax, jax.numpy as jnp
from jax import lax
import numpy as np

D_MODEL = 1024
BATCH = 4
SEQ = 8192
DEPTH = 1

CHUNK = 64
Q_BLOCK = 128
EPS = 1e-6

FOX_HEAD_DIM = 64
FOX_WIDTH = D_MODEL // 2
FOX_HEADS = FOX_WIDTH // FOX_HEAD_DIM

SSM_WIDTH = D_MODEL // 2
SSM_GROUP = 16
SSM_GROUPS = SSM_WIDTH // SSM_GROUP
SSM_STATE = 64
DT_MIN = 1e-3
DT_MAX = 1e-1

N_BRANCHES = 2

COL_Q = FOX_WIDTH
COL_K = COL_Q + FOX_WIDTH
COL_V = COL_K + FOX_WIDTH
COL_F = COL_V + FOX_HEADS
COL_U = COL_F + SSM_WIDTH
IN_COLS = COL_U + N_BRANCHES * D_MODEL

N_GROUPS = 4
EXPERTS_PER_GROUP = 4
N_EXPERTS = N_GROUPS * EXPERTS_PER_GROUP
TOP_K_IN_GROUP = 2
EXPERT_FF = D_MODEL // 4

kernel_name = "hybrid_fox_s5_hiermoe_block"


def rms_norm(x, g):
    xf = x.astype(jnp.float32)
    y = xf * lax.rsqrt(jnp.mean(xf * xf, axis=-1, keepdims=True) + EPS)
    return (y * g.astype(jnp.float32)).astype(x.dtype)


def forgetting_attention(q, k, v, cum):
    seq = q.shape[1]
    scale = 1.0 / math.sqrt(FOX_HEAD_DIM)
    cum_h = jnp.transpose(cum, (0, 2, 1))
    outs = []
    for blk in range(seq // Q_BLOCK):
        lo, hi = blk * Q_BLOCK, (blk + 1) * Q_BLOCK
        qb, kb, vb = q[:, lo:hi], k[:, :hi], v[:, :hi]
        s = jnp.einsum('bqhd,bkhd->bhqk', qb, kb).astype(jnp.float32) * scale
        s = s + cum_h[:, :, lo:hi, None] - cum_h[:, :, None, :hi]
        q_pos = lo + jnp.arange(Q_BLOCK)
        k_pos = jnp.arange(hi)
        mask = k_pos[None, :] <= q_pos[:, None]
        s = jnp.where(mask[None, None], s, -jnp.inf)
        p = jax.nn.softmax(s, axis=-1).astype(vb.dtype)
        outs.append(jnp.einsum('bhqk,bkhd->bqhd', p, vb))
    return jnp.concatenate(outs, axis=1)


def s5_ssm(u, lambda_re, lambda_im, log_step, b_re, b_im, c_re, c_im, d_skip):
    bsz, seq, _ = u.shape
    f32 = jnp.float32
    ug = u.reshape(bsz, seq, SSM_GROUPS, SSM_GROUP).astype(f32)
    dt = jnp.exp(log_step.astype(f32))[:, None]
    lr = lambda_re.astype(f32)
    li = lambda_im.astype(f32)
    mag = jnp.exp(lr * dt)
    abar_re = mag * jnp.cos(li * dt)
    abar_im = mag * jnp.sin(li * dt)
    den = lr * lr + li * li
    num_re = abar_re - 1.0
    z_re = (num_re * lr + abar_im * li) / den
    z_im = (abar_im * lr - num_re * li) / den
    br = b_re.astype(f32)
    bi = b_im.astype(f32)
    bb_re = z_re[..., None] * br - z_im[..., None] * bi
    bb_im = z_re[..., None] * bi + z_im[..., None] * br
    bu_re = jnp.einsum('bsgc,gpc->bsgp', ug, bb_re)
    bu_im = jnp.einsum('bsgc,gpc->bsgp', ug, bb_im)
    a_re = jnp.broadcast_to(abar_re, (1, seq, SSM_GROUPS, SSM_STATE))
    a_im = jnp.broadcast_to(abar_im, (1, seq, SSM_GROUPS, SSM_STATE))

    def combine(left, right):
        ar1, ai1, br1, bi1 = left
        ar2, ai2, br2, bi2 = right
        return (ar2 * ar1 - ai2 * ai1,
                ar2 * ai1 + ai2 * ar1,
                ar2 * br1 - ai2 * bi1 + br2,
                ar2 * bi1 + ai2 * br1 + bi2)

    _, _, s_re, s_im = lax.associative_scan(combine, (a_re, a_im, bu_re, bu_im), axis=1)
    y = (jnp.einsum('bsgp,gcp->bsgc', s_re, c_re.astype(f32))
         - jnp.einsum('bsgp,gcp->bsgc', s_im, c_im.astype(f32)))
    y = y + d_skip.astype(f32).reshape(SSM_GROUPS, SSM_GROUP) * ug
    return y.reshape(bsz, seq, SSM_WIDTH)


def hierarchical_moe(h, w_rg, b_rg, w_re, b_re, w_gate, w_up, w_down):
    f32 = jnp.float32
    logits_g = (h @ w_rg).astype(f32) + b_rg.astype(f32)
    p_g = jax.nn.softmax(logits_g, axis=-1)
    g_idx = jnp.argmax(logits_g, axis=-1)
    p_sel = jnp.max(p_g, axis=-1)
    logits_e = ((h @ w_re).astype(f32) + b_re.astype(f32)).reshape(
        h.shape[0], h.shape[1], N_GROUPS, EXPERTS_PER_GROUP)
    le_sel = jnp.einsum('bsg,bsge->bse', jax.nn.one_hot(g_idx, N_GROUPS, dtype=f32), logits_e)
    top_v, top_i = lax.top_k(le_sel, TOP_K_IN_GROUP)
    w = jax.nn.softmax(top_v, axis=-1) * p_sel[..., None]
    expert_id = g_idx[..., None] * EXPERTS_PER_GROUP + top_i
    combine_w = jnp.einsum('bsk,bske->bse', w,
                           jax.nn.one_hot(expert_id, N_EXPERTS, dtype=f32)).astype(h.dtype)
    hg = jnp.einsum('bsd,edf->bsef', h, w_gate)
    hu = jnp.einsum('bsd,edf->bsef', h, w_up)
    act = jax.nn.silu(hg) * hu * combine_w[..., None]
    return jnp.einsum('bsef,efd->bsd', act, w_down)


def setup_inputs(seed: int = 0) -> dict:
    key = jax.random.key(seed)
    ks = jax.random.split(key, 32)
    f32 = jnp.float32
    L = DEPTH

    def nrm(k, shape, scale):
        return jax.random.normal(k, shape, f32) * scale

    x = nrm(ks[0], (BATCH, SEQ, D_MODEL), 1.0)
    g_mix = 1.0 + nrm(ks[1], (L, D_MODEL), 0.02)
    w_in = nrm(ks[2], (L, D_MODEL, IN_COLS), D_MODEL ** -0.5)
    b_forget = 2.0 + nrm(ks[3], (L, FOX_HEADS), 0.1)
    b_gate = nrm(ks[4], (L, N_BRANCHES * D_MODEL), 0.02)
    w_out_a = nrm(ks[5], (L, FOX_WIDTH, D_MODEL), FOX_WIDTH ** -0.5)
    lambda_re = -0.5 + nrm(ks[6], (L, SSM_GROUPS, SSM_STATE), 0.01)
    lambda_im = (jnp.pi * jnp.arange(SSM_STATE, dtype=f32))[None, None, :] \
        + nrm(ks[7], (L, SSM_GROUPS, SSM_STATE), 0.01)
    log_step = jax.random.uniform(ks[8], (L, SSM_GROUPS), f32,
                                  math.log(DT_MIN), math.log(DT_MAX))
    ssm_b_re = nrm(ks[9], (L, SSM_GROUPS, SSM_STATE, SSM_GROUP), (2 * SSM_GROUP) ** -0.5)
    ssm_b_im = nrm(ks[10], (L, SSM_GROUPS, SSM_STATE, SSM_GROUP), (2 * SSM_GROUP) ** -0.5)
    ssm_c_re = nrm(ks[11], (L, SSM_GROUPS, SSM_GROUP, SSM_STATE), SSM_STATE ** -0.5)
    ssm_c_im = nrm(ks[12], (L, SSM_GROUPS, SSM_GROUP, SSM_STATE), SSM_STATE ** -0.5)
    ssm_d = nrm(ks[13], (L, SSM_WIDTH), 1.0)
    w_glu = nrm(ks[14], (L, SSM_WIDTH, SSM_WIDTH), SSM_WIDTH ** -0.5)
    b_glu = nrm(ks[15], (L, SSM_WIDTH), 0.02)
    w_out_b = nrm(ks[16], (L, SSM_WIDTH, D_MODEL), SSM_WIDTH ** -0.5)
    w_out = nrm(ks[17], (L, D_MODEL, D_MODEL), D_MODEL ** -0.5)
    g_ffn = 1.0 + nrm(ks[18], (L, D_MODEL), 0.02)
    w_router_group = nrm(ks[19], (L, D_MODEL, N_GROUPS), D_MODEL ** -0.5)
    b_router_group = nrm(ks[20], (L, N_GROUPS), 0.01)
    w_router_expert = nrm(ks[21], (L, D_MODEL, N_EXPERTS), D_MODEL ** -0.5)
    b_router_expert = nrm(ks[22], (L, N_EXPERTS), 0.01)
    w_exp_gate = nrm(ks[23], (L, N_EXPERTS, D_MODEL, EXPERT_FF), D_MODEL ** -0.5)
    w_exp_up = nrm(ks[24], (L, N_EXPERTS, D_MODEL, EXPERT_FF), D_MODEL ** -0.5)
    w_exp_down = nrm(ks[25], (L, N_EXPERTS, EXPERT_FF, D_MODEL), EXPERT_FF ** -0.5)
    g_final = 1.0 + nrm(ks[26], (D_MODEL,), 0.02)
    return {
        "x": x, "g_mix": g_mix, "w_in": w_in, "b_forget": b_forget, "b_gate": b_gate,
        "w_out_a": w_out_a, "lambda_re": lambda_re, "lambda_im": lambda_im,
        "log_step": log_step, "ssm_b_re": ssm_b_re, "ssm_b_im": ssm_b_im,
        "ssm_c_re": ssm_c_re, "ssm_c_im": ssm_c_im, "ssm_d": ssm_d,
        "w_glu": w_glu, "b_glu": b_glu, "w_out_b": w_out_b, "w_out": w_out,
        "g_ffn": g_ffn, "w_router_group": w_router_group, "b_router_group": b_router_group,
        "w_router_expert": w_router_expert, "b_router_expert": b_router_expert,
        "w_exp_gate": w_exp_gate, "w_exp_up": w_exp_up, "w_exp_down": w_exp_down,
        "g_final": g_final,
    }


def reference(x, g_mix, w_in, b_forget, b_gate, w_out_a, lambda_re, lambda_im, log_step,
              ssm_b_re, ssm_b_im, ssm_c_re, ssm_c_im, ssm_d, w_glu, b_glu, w_out_b, w_out,
              g_ffn, w_router_group, b_router_group, w_router_expert, b_router_expert,
              w_exp_gate, w_exp_up, w_exp_down, g_final):
    bsz, seq, _ = x.shape
    for layer in range(DEPTH):
        h = rms_norm(x, g_mix[layer])
        proj = h @ w_in[layer]
        q, k, v, f_logit, u, gate_logit = jnp.split(
            proj, [COL_Q, COL_K, COL_V, COL_F, COL_U], axis=-1)

        q = q.reshape(bsz, seq, FOX_HEADS, FOX_HEAD_DIM)
        k = k.reshape(bsz, seq, FOX_HEADS, FOX_HEAD_DIM)
        v = v.reshape(bsz, seq, FOX_HEADS, FOX_HEAD_DIM)
        log_f = jax.nn.log_sigmoid(f_logit.astype(jnp.float32)
                                   + b_forget[layer].astype(jnp.float32))
        cum = jnp.cumsum(log_f, axis=1)
        o_a = forgetting_attention(q, k, v, cum).reshape(bsz, seq, FOX_WIDTH)
        y_a = o_a @ w_out_a[layer]

        y_s = s5_ssm(u, lambda_re[layer], lambda_im[layer], log_step[layer],
                     ssm_b_re[layer], ssm_b_im[layer], ssm_c_re[layer], ssm_c_im[layer],
                     ssm_d[layer]).astype(x.dtype)
        z = jax.nn.gelu(y_s)
        z = z * jax.nn.sigmoid(z @ w_glu[layer] + b_glu[layer])
        y_b = z @ w_out_b[layer]

        gates = jax.nn.sigmoid(gate_logit + b_gate[layer]).reshape(
            bsz, seq, N_BRANCHES, D_MODEL)
        merged = gates[:, :, 0, :] * y_a + gates[:, :, 1, :] * y_b
        x = x + merged @ w_out[layer]

        h2 = rms_norm(x, g_ffn[layer])
        x = x + hierarchical_moe(h2, w_router_group[layer], b_router_group[layer],
                                 w_router_expert[layer], b_router_expert[layer],
                                 w_exp_gate[layer], w_exp_up[layer], w_exp_down[layer])
    return rms_norm(x, g_final)
```

```python
import functools
import math

import jax
import jax.numpy as jnp
from jax import lax
from jax.experimental import pallas as pl
from jax.experimental.pallas import tpu as pltpu

D_MODEL = 1024
HEADS = 8
HEAD_DIM = 64
FOX_WIDTH = HEADS * HEAD_DIM
SSM_WIDTH = 512
SSM_GROUP = 16
SSM_GROUPS = 32
SSM_STATE = 64
N_GROUPS = 4
EXPERTS_PER_GROUP = 4
N_EXPERTS = 16
EXPERT_FF = 256
EPS = 1e-6

LANES = 128
QK_DEPTH = LANES
V_ROWS = HEAD_DIM + 16
SUPER = 4
SG_CH = SSM_WIDTH // SUPER
SG_ST = SSM_GROUPS // SUPER * SSM_STATE
ROUTER_COLS = LANES

F32 = jnp.float32
BF16 = jnp.bfloat16
VMEM_LIMIT = 56 * 1024 * 1024


def _cparams(sem):
    return pltpu.CompilerParams(dimension_semantics=sem, vmem_limit_bytes=VMEM_LIMIT)


def _rms(x, g):
    return x * lax.rsqrt(jnp.mean(x * x, axis=-1, keepdims=True) + EPS) * g


def _proj_kernel(x_ref, g_ref, w_ref, bf_ref, bg_ref,
                 q_ref, k_ref, v_ref, u_ref, gate_ref, f_ref):
    h = _rms(x_ref[...], g_ref[...]).astype(BF16)

    def mm(lo, hi):
        return jnp.dot(h, w_ref[:, lo:hi], preferred_element_type=F32)

    w = FOX_WIDTH
    q_ref[...] = mm(0, w).astype(BF16)
    k_ref[...] = mm(w, 2 * w).astype(BF16)
    v_ref[...] = mm(2 * w, 3 * w).astype(BF16)
    u_ref[...] = mm(3 * w, 4 * w).astype(BF16)
    for c in range(4):
        lo = 4 * w + c * w
        gate_ref[:, c * w:(c + 1) * w] = jax.nn.sigmoid(
            mm(lo, lo + w) + bg_ref[:, c * w:(c + 1) * w]).astype(BF16)
    f_ref[...] = mm(8 * w, 8 * w + LANES) + bf_ref[...]


def _proj(x2, g_mix, w1, bf_pad, b_gate, tm):
    t = x2.shape[0]
    n1 = w1.shape[1]
    row = lambda i: (i, 0)
    const = lambda i: (0, 0)
    return pl.pallas_call(
        _proj_kernel,
        grid=(t // tm,),
        in_specs=[
            pl.BlockSpec((tm, D_MODEL), row),
            pl.BlockSpec((1, D_MODEL), const),
            pl.BlockSpec((D_MODEL, n1), const),
            pl.BlockSpec((1, LANES), const),
            pl.BlockSpec((1, 2 * D_MODEL), const),
        ],
        out_specs=[
            pl.BlockSpec((tm, FOX_WIDTH), row),
            pl.BlockSpec((tm, FOX_WIDTH), row),
            pl.BlockSpec((tm, FOX_WIDTH), row),
            pl.BlockSpec((tm, SSM_WIDTH), row),
            pl.BlockSpec((tm, 2 * D_MODEL), row),
            pl.BlockSpec((tm, LANES), row),
        ],
        out_shape=[
            jax.ShapeDtypeStruct((t, FOX_WIDTH), BF16),
            jax.ShapeDtypeStruct((t, FOX_WIDTH), BF16),
            jax.ShapeDtypeStruct((t, FOX_WIDTH), BF16),
            jax.ShapeDtypeStruct((t, SSM_WIDTH), BF16),
            jax.ShapeDtypeStruct((t, 2 * D_MODEL), BF16),
            jax.ShapeDtypeStruct((t, LANES), F32),
        ],
        compiler_params=_cparams(("arbitrary",)),
        name="proj",
    )(x2, g_mix, w1, bf_pad, b_gate)


def _cum_kernel(f_ref, c_ref):
    x = f_ref[...]
    c = jnp.minimum(x, 0.0) - jnp.log(1.0 + jnp.exp(-jnp.abs(x)))
    seq = x.shape[1]
    lane = lax.broadcasted_iota(jnp.int32, x.shape, 1)
    sh = 1
    while sh < seq:
        c = c + jnp.where(lane >= sh, pltpu.roll(c, sh, axis=1), 0.0)
        sh *= 2
    c1 = c.astype(BF16).astype(F32)
    r = c - c1
    c2 = r.astype(BF16).astype(F32)
    c3 = (r - c2).astype(BF16).astype(F32)
    c_ref[0] = c1
    c_ref[1] = c2
    c_ref[2] = c3


def _cum(ft):
    b, h, s = ft.shape
    return pl.pallas_call(
        _cum_kernel,
        grid=(b,),
        in_specs=[pl.BlockSpec((None, h, s), lambda i: (i, 0, 0))],
        out_specs=pl.BlockSpec((None, 3, h, s), lambda i: (i, 0, 0, 0)),
        out_shape=jax.ShapeDtypeStruct((b, 3, h, s), F32),
        compiler_params=_cparams(("arbitrary",)),
        name="cum",
    )(ft)


def _attn_kernel(qt_ref, kx_ref, vt_ref, o_ref, *, tile):
    i = pl.program_id(2)
    qt = qt_ref[...]

    def step(j, m, acc, masked):
        s = jnp.dot(kx_ref[j], qt, preferred_element_type=F32)
        if masked:
            kpos = lax.broadcasted_iota(jnp.int32, s.shape, 0)
            qpos = lax.broadcasted_iota(jnp.int32, s.shape, 1)
            s = jnp.where(kpos <= qpos, s, -jnp.inf)
        m_new = jnp.maximum(m, jnp.max(s, axis=0, keepdims=True))
        p = jnp.exp(s - m_new).astype(BF16)
        acc = jnp.exp(m - m_new) * acc + jnp.dot(vt_ref[j], p, preferred_element_type=F32)
        return m_new, acc

    m0 = jnp.full((1, tile), -jnp.inf, F32)
    acc0 = jnp.zeros((V_ROWS, tile), F32)
    m, acc = step(i, m0, acc0, True)
    m, acc = lax.fori_loop(0, i, lambda j, c: step(j, c[0], c[1], False), (m, acc))
    o_ref[...] = (acc[:HEAD_DIM] / acc[HEAD_DIM:HEAD_DIM + 1]).astype(BF16)


def _attn(qt, kx, vt, tile):
    b, h, _, s = qt.shape
    n = s // tile
    return pl.pallas_call(
        functools.partial(_attn_kernel, tile=tile),
        grid=(b, h, n),
        in_specs=[
            pl.BlockSpec((None, None, QK_DEPTH, tile), lambda bi, hi, i: (bi, hi, 0, i)),
            pl.BlockSpec((None, None, n, tile, QK_DEPTH), lambda bi, hi, i: (bi, hi, 0, 0, 0)),
            pl.BlockSpec((None, None, n, V_ROWS, tile), lambda bi, hi, i: (bi, hi, 0, 0, 0)),
        ],
        out_specs=pl.BlockSpec((None, None, HEAD_DIM, tile), lambda bi, hi, i: (bi, hi, 0, i)),
        out_shape=jax.ShapeDtypeStruct((b, h, HEAD_DIM, s), BF16),
        compiler_params=_cparams(("arbitrary", "arbitrary", "arbitrary")),
        name="attn",
    )(qt, kx, vt)


def _ssm_kernel(u_ref, bm_ref, cm_ref, ar_ref, ai_ref, d_ref, y_ref, x_scr, s_scr, *, tb):
    rows = tb * 8

    @pl.when(pl.program_id(0) == 0)
    def _():
        s_scr[...] = jnp.zeros_like(s_scr)

    u = u_ref[...]
    is_re = (lax.broadcasted_iota(jnp.int32, (rows, SG_CH), 0) % 8) < 4
    zero = jnp.zeros((rows, SG_CH), BF16)
    for g in range(SUPER):
        ug = u[:, g * SG_CH:(g + 1) * SG_CH]
        lhs = jnp.concatenate([jnp.where(is_re, ug, zero), jnp.where(is_re, zero, ug)], axis=1)
        x_scr[:, g * SG_ST:(g + 1) * SG_ST] = jnp.dot(lhs, bm_ref[g], preferred_element_type=F32)

    half = 2 * SG_ST
    for c in range(SUPER * SG_ST // half):
        lo = c * half
        ar = ar_ref[:, lo:lo + half]
        ai = ai_ref[:, lo:lo + half]

        def body(t, s, lo=lo, ar=ar, ai=ai):
            r0 = pl.multiple_of(t * 8, 8)
            s = ar * s + ai * pltpu.roll(s, 4, axis=0) + x_scr[pl.ds(r0, 8), lo:lo + half]
            x_scr[pl.ds(r0, 8), lo:lo + half] = s
            return s

        s_scr[:, lo:lo + half] = lax.fori_loop(0, tb, body, s_scr[:, lo:lo + half])

    for g in range(SUPER):
        st = x_scr[:, g * SG_ST:(g + 1) * SG_ST].astype(BF16)
        o = jnp.dot(st, cm_ref[g], preferred_element_type=F32)
        y = o[:, :SG_CH] + pltpu.roll(o[:, SG_CH:], rows - 4, axis=0)
        y = y + d_ref[:, g * SG_CH:(g + 1) * SG_CH] * u[:, g * SG_CH:(g + 1) * SG_CH].astype(F32)
        y_ref[:, g * SG_CH:(g + 1) * SG_CH] = y.astype(BF16)


def _ssm(u8, bm, cm, ar, ai, d, tb):
    rows_total = u8.shape[0]
    rows = tb * 8
    return pl.pallas_call(
        functools.partial(_ssm_kernel, tb=tb),
        grid=(rows_total // rows,),
        in_specs=[
            pl.BlockSpec((rows, SSM_WIDTH), lambda i: (i, 0)),
            pl.BlockSpec((SUPER, 2 * SG_CH, SG_ST), lambda i: (0, 0, 0)),
            pl.BlockSpec((SUPER, SG_ST, 2 * SG_CH), lambda i: (0, 0, 0)),
            pl.BlockSpec((8, SUPER * SG_ST), lambda i: (0, 0)),
            pl.BlockSpec((8, SUPER * SG_ST), lambda i: (0, 0)),
            pl.BlockSpec((1, SSM_WIDTH), lambda i: (0, 0)),
        ],
        out_specs=pl.BlockSpec((rows, SSM_WIDTH), lambda i: (i, 0)),
        out_shape=jax.ShapeDtypeStruct((rows_total, SSM_WIDTH), BF16),
        scratch_shapes=[
            pltpu.VMEM((rows, SUPER * SG_ST), F32),
            pltpu.VMEM((8, SUPER * SG_ST), F32),
        ],
        compiler_params=_cparams(("arbitrary",)),
        name="ssm",
    )(u8, bm, cm, ar, ai, d)


def _merge_kernel(x_ref, o_ref, ys_ref, gate_ref, woa_ref, wglu_ref, bglu_ref, wob_ref, wout_ref,
                  x1_ref):
    y_a = jnp.dot(o_ref[...], woa_ref[...], preferred_element_type=F32)
    z = jax.nn.gelu(ys_ref[...].astype(F32))
    zg = jnp.dot(z.astype(BF16), wglu_ref[...], preferred_element_type=F32) + bglu_ref[...]
    z = z * jax.nn.sigmoid(zg)
    y_b = jnp.dot(z.astype(BF16), wob_ref[...], preferred_element_type=F32)
    g0 = gate_ref[:, :D_MODEL].astype(F32)
    g1 = gate_ref[:, D_MODEL:].astype(F32)
    merged = (g0 * y_a + g1 * y_b).astype(BF16)
    x1_ref[...] = x_ref[...] + jnp.dot(merged, wout_ref[...], preferred_element_type=F32)


def _merge(x2, o, ys, gates, woa, wglu, bglu, wob, wout, tm):
    t = x2.shape[0]
    row = lambda i: (i, 0)
    const = lambda i: (0, 0)
    return pl.pallas_call(
        _merge_kernel,
        grid=(t // tm,),
        in_specs=[
            pl.BlockSpec((tm, D_MODEL), row),
            pl.BlockSpec((tm, FOX_WIDTH), row),
            pl.BlockSpec((tm, SSM_WIDTH), row),
            pl.BlockSpec((tm, 2 * D_MODEL), row),
            pl.BlockSpec((FOX_WIDTH, D_MODEL), const),
            pl.BlockSpec((SSM_WIDTH, SSM_WIDTH), const),
            pl.BlockSpec((1, SSM_WIDTH), const),
            pl.BlockSpec((SSM_WIDTH, D_MODEL), const),
            pl.BlockSpec((D_MODEL, D_MODEL), const),
        ],
        out_specs=pl.BlockSpec((tm, D_MODEL), row),
        out_shape=jax.ShapeDtypeStruct((t, D_MODEL), F32),
        compiler_params=_cparams(("arbitrary",)),
        name="merge",
    )(x2, o, ys, gates, woa, wglu, bglu, wob, wout)


def _combine_weights(logits):
    lane = lax.broadcasted_iota(jnp.int32, logits.shape, 1)
    neg = -jnp.inf
    big = jnp.int32(ROUTER_COLS)
    is_g = lane < N_GROUPS
    lg = jnp.where(is_g, logits, neg)
    gmax = jnp.max(lg, axis=1, keepdims=True)
    p_sel = 1.0 / jnp.sum(jnp.exp(lg - gmax), axis=1, keepdims=True)
    g_idx = jnp.min(jnp.where(lg == gmax, lane, big), axis=1, keepdims=True)
    lo = N_GROUPS + g_idx * EXPERTS_PER_GROUP
    in_grp = (lane >= lo) & (lane < lo + EXPERTS_PER_GROUP)
    le = jnp.where(in_grp, logits, neg)
    v1 = jnp.max(le, axis=1, keepdims=True)
    i1 = jnp.min(jnp.where(le == v1, lane, big), axis=1, keepdims=True)
    le2 = jnp.where(lane == i1, neg, le)
    v2 = jnp.max(le2, axis=1, keepdims=True)
    i2 = jnp.min(jnp.where(le2 == v2, lane, big), axis=1, keepdims=True)
    e2 = jnp.exp(v2 - v1)
    w1 = p_sel / (1.0 + e2)
    w2 = p_sel * e2 / (1.0 + e2)
    return jnp.where(lane == i1, w1, 0.0) + jnp.where(lane == i2, w2, 0.0)


def _moe_kernel(x1_ref, gf_ref, wr_ref, br_ref, wg_ref, wu_ref, wd_ref, gfin_ref, out_ref,
                h_scr, cw_scr, acc_scr):
    e = pl.program_id(1)

    @pl.when(e == 0)
    def _():
        h = _rms(x1_ref[...], gf_ref[...])
        h_hi = h.astype(BF16)
        h_lo = (h - h_hi.astype(F32)).astype(BF16)
        logits = (jnp.dot(h_hi, wr_ref[0], preferred_element_type=F32)
                  + jnp.dot(h_lo, wr_ref[0], preferred_element_type=F32)
                  + jnp.dot(h_hi, wr_ref[1], preferred_element_type=F32)) + br_ref[...]
        h_scr[...] = h_hi
        cw_scr[...] = _combine_weights(logits)
        acc_scr[...] = jnp.zeros_like(acc_scr)

    h = h_scr[...]
    hg = jnp.dot(h, wg_ref[...], preferred_element_type=F32)
    hu = jnp.dot(h, wu_ref[...], preferred_element_type=F32)
    lane = lax.broadcasted_iota(jnp.int32, cw_scr.shape, 1)
    cw = jnp.sum(jnp.where(lane == N_GROUPS + e, cw_scr[...], 0.0), axis=1, keepdims=True)
    act = (hg * jax.nn.sigmoid(hg) * hu * cw).astype(BF16)
    acc_scr[...] += jnp.dot(act, wd_ref[...], preferred_element_type=F32)

    @pl.when(e == N_EXPERTS - 1)
    def _():
        out_ref[...] = _rms(x1_ref[...] + acc_scr[...], gfin_ref[...])


def _moe(x1, g_ffn, wr, br, wg, wu, wd, g_final, tm):
    t = x1.shape[0]
    return pl.pallas_call(
        _moe_kernel,
        grid=(t // tm, N_EXPERTS),
        in_specs=[
            pl.BlockSpec((tm, D_MODEL), lambda i, e: (i, 0)),
            pl.BlockSpec((1, D_MODEL), lambda i, e: (0, 0)),
            pl.BlockSpec((2, D_MODEL, ROUTER_COLS), lambda i, e: (0, 0, 0)),
            pl.BlockSpec((1, ROUTER_COLS), lambda i, e: (0, 0)),
            pl.BlockSpec((None, D_MODEL, EXPERT_FF), lambda i, e: (e, 0, 0)),
            pl.BlockSpec((None, D_MODEL, EXPERT_FF), lambda i, e: (e, 0, 0)),
            pl.BlockSpec((None, EXPERT_FF, D_MODEL), lambda i, e: (e, 0, 0)),
            pl.BlockSpec((1, D_MODEL), lambda i, e: (0, 0)),
        ],
        out_specs=pl.BlockSpec((tm, D_MODEL), lambda i, e: (i, 0)),
        out_shape=jax.ShapeDtypeStruct((t, D_MODEL), F32),
        scratch_shapes=[
            pltpu.VMEM((tm, D_MODEL), BF16),
            pltpu.VMEM((tm, ROUTER_COLS), F32),
            pltpu.VMEM((tm, D_MODEL), F32),
        ],
        compiler_params=_cparams(("arbitrary", "arbitrary")),
        name="moe",
    )(x1, g_ffn, wr, br, wg, wu, wd, g_final)


def _ssm_params(lambda_re, lambda_im, log_step, b_re, b_im, c_re, c_im):
    dt = jnp.exp(log_step)[:, None]
    mag = jnp.exp(lambda_re * dt)
    a_re = mag * jnp.cos(lambda_im * dt)
    a_im = mag * jnp.sin(lambda_im * dt)
    den = lambda_re * lambda_re + lambda_im * lambda_im
    num_re = a_re - 1.0
    z_re = (num_re * lambda_re + a_im * lambda_im) / den
    z_im = (a_im * lambda_re - num_re * lambda_im) / den
    bb_re = z_re[..., None] * b_re - z_im[..., None] * b_im
    bb_im = z_re[..., None] * b_im + z_im[..., None] * b_re
    gl = SSM_GROUPS // SUPER
    eye = jnp.eye(gl, dtype=F32)

    def in_blk(bb):
        t = bb.reshape(SUPER, gl, SSM_STATE, SSM_GROUP).transpose(0, 1, 3, 2)
        t = t[:, :, :, None, :] * eye[None, :, None, :, None]
        return t.reshape(SUPER, gl * SSM_GROUP, gl * SSM_STATE)

    def out_blk(cc):
        t = cc.reshape(SUPER, gl, SSM_GROUP, SSM_STATE).transpose(0, 1, 3, 2)
        t = t[:, :, :, None, :] * eye[None, :, None, :, None]
        return t.reshape(SUPER, gl * SSM_STATE, gl * SSM_GROUP)

    bm = jnp.concatenate([in_blk(bb_re), in_blk(bb_im)], axis=1).astype(BF16)
    cm = jnp.concatenate([out_blk(c_re), out_blk(-c_im)], axis=2).astype(BF16)
    ar_row = a_re.reshape(1, SSM_GROUPS * SSM_STATE)
    ai_row = a_im.reshape(1, SSM_GROUPS * SSM_STATE)
    ar = jnp.broadcast_to(ar_row, (8, SSM_GROUPS * SSM_STATE))
    ai = jnp.concatenate([jnp.broadcast_to(-ai_row, (4, ai_row.shape[1])),
                          jnp.broadcast_to(ai_row, (4, ai_row.shape[1]))], axis=0)
    return bm, cm, ar, ai


def _pick(n, pref):
    return pref if n % pref == 0 else n


def _mixers(x, g_mix, w_in, b_forget, b_gate, w_out_a, lambda_re, lambda_im, log_step,
            ssm_b_re, ssm_b_im, ssm_c_re, ssm_c_im, ssm_d, w_glu, b_glu, w_out_b, w_out):
    bsz, seq, _ = x.shape
    assert bsz == 4, "the SSM kernel packs (re/im) x 4 batches onto the 8 sublanes"
    t = bsz * seq
    layer = 0
    x2 = x.reshape(t, D_MODEL)

    w = w_in[layer]
    fw = FOX_WIDTH
    wq, wk, wv = w[:, :fw], w[:, fw:2 * fw], w[:, 2 * fw:3 * fw]
    wf = w[:, 3 * fw:3 * fw + HEADS]
    wu = w[:, 3 * fw + HEADS:3 * fw + HEADS + SSM_WIDTH]
    wgt = w[:, 3 * fw + HEADS + SSM_WIDTH:]
    scale = 1.0 / math.sqrt(HEAD_DIM)
    w1 = jnp.concatenate(
        [wq * scale, wk, wv, wu, wgt, jnp.pad(wf, ((0, 0), (0, LANES - HEADS)))], axis=1).astype(BF16)
    bf_pad = jnp.pad(b_forget[layer], (0, LANES - HEADS)).reshape(1, LANES)

    tm = _pick(t, 512)
    q, k, v, u, gates, f = _proj(x2, g_mix[layer].reshape(1, D_MODEL), w1, bf_pad,
                                 b_gate[layer].reshape(1, 2 * D_MODEL), tm)

    ft = f[:, :HEADS].reshape(bsz, seq, HEADS).transpose(0, 2, 1)
    c3 = _cum(ft).astype(BF16)

    tile = _pick(seq, 256)
    n = seq // tile
    cq = c3.transpose(0, 2, 1, 3)
    ones_q = jnp.ones((bsz, HEADS, 3, seq), BF16)
    qh = q.reshape(bsz, seq, HEADS, HEAD_DIM).transpose(0, 2, 3, 1)
    qt = jnp.concatenate(
        [qh, cq, ones_q, jnp.zeros((bsz, HEADS, QK_DEPTH - HEAD_DIM - 6, seq), BF16)], axis=2)
    kh = k.reshape(bsz, seq, HEADS, HEAD_DIM).transpose(0, 2, 1, 3)
    ck = (-c3).transpose(0, 2, 3, 1)
    kx = jnp.concatenate(
        [kh, jnp.ones((bsz, HEADS, seq, 3), BF16), ck,
         jnp.zeros((bsz, HEADS, seq, QK_DEPTH - HEAD_DIM - 6), BF16)], axis=3)
    kx = kx.reshape(bsz, HEADS, n, tile, QK_DEPTH)
    vh = v.reshape(bsz, seq, HEADS, HEAD_DIM).transpose(0, 2, 3, 1)
    vt = jnp.concatenate(
        [vh, jnp.ones((bsz, HEADS, 1, seq), BF16),
         jnp.zeros((bsz, HEADS, V_ROWS - HEAD_DIM - 1, seq), BF16)], axis=2)
    vt = vt.reshape(bsz, HEADS, V_ROWS, n, tile).transpose(0, 1, 3, 2, 4)
    o_t = _attn(qt, kx, vt, tile)
    o = o_t.transpose(0, 3, 1, 2).reshape(t, FOX_WIDTH)

    bm, cm, ar, ai = _ssm_params(lambda_re[layer], lambda_im[layer], log_step[layer],
                                 ssm_b_re[layer], ssm_b_im[layer], ssm_c_re[layer], ssm_c_im[layer])
    ub = u.reshape(bsz, seq, SSM_WIDTH).transpose(1, 0, 2)
    u8 = jnp.concatenate([ub, ub], axis=1).reshape(seq * 8, SSM_WIDTH)
    tb = _pick(seq, 128)
    y8 = _ssm(u8, bm, cm, ar, ai, ssm_d[layer].reshape(1, SSM_WIDTH), tb)
    ys = y8.reshape(seq, 8, SSM_WIDTH)[:, :bsz].transpose(1, 0, 2).reshape(t, SSM_WIDTH)

    x1 = _merge(x2, o, ys, gates, w_out_a[layer].astype(BF16), w_glu[layer].astype(BF16),
                b_glu[layer].reshape(1, SSM_WIDTH), w_out_b[layer].astype(BF16),
                w_out[layer].astype(BF16), tm)
    return x1, o, ys


def kernel(x, g_mix, w_in, b_forget, b_gate, w_out_a, lambda_re, lambda_im, log_step, ssm_b_re, ssm_b_im, ssm_c_re, ssm_c_im, ssm_d, w_glu, b_glu, w_out_b, w_out, g_ffn, w_router_group, b_router_group, w_router_expert, b_router_expert, w_exp_gate, w_exp_up, w_exp_down, g_final):
    bsz, seq, _ = x.shape
    t = bsz * seq
    layer = 0
    x1, _, _ = _mixers(x, g_mix, w_in, b_forget, b_gate, w_out_a, lambda_re, lambda_im, log_step,
                       ssm_b_re, ssm_b_im, ssm_c_re, ssm_c_im, ssm_d, w_glu, b_glu, w_out_b, w_out)

    wr_f = jnp.concatenate([w_router_group[layer], w_router_expert[layer]], axis=1)
    wr_f = jnp.pad(wr_f, ((0, 0), (0, ROUTER_COLS - N_GROUPS - N_EXPERTS)))
    wr_hi = wr_f.astype(BF16)
    wr_lo = (wr_f - wr_hi.astype(F32)).astype(BF16)
    wr = jnp.stack([wr_hi, wr_lo])
    br = jnp.pad(jnp.concatenate([b_router_group[layer], b_router_expert[layer]]),
                 (0, ROUTER_COLS - N_GROUPS - N_EXPERTS)).reshape(1, ROUTER_COLS)
    tmoe = _pick(t, 1024)
    out = _moe(x1, g_ffn[layer].reshape(1, D_MODEL), wr, br, w_exp_gate[layer].astype(BF16),
               w_exp_up[layer].astype(BF16), w_exp_down[layer].astype(BF16),
               g_final.reshape(1, D_MODEL), tmoe)
    return out.reshape(bsz, seq, D_MODEL)
```

```python
import functools
import math

import jax
import jax.numpy as jnp
from jax import lax
from jax.experimental import pallas as pl
from jax.experimental.pallas import tpu as pltpu

D_MODEL = 1024
HEADS = 8
HEAD_DIM = 64
FOX_WIDTH = HEADS * HEAD_DIM
SSM_WIDTH = 512
SSM_GROUP = 16
SSM_GROUPS = 32
SSM_STATE = 64
N_GROUPS = 4
EXPERTS_PER_GROUP = 4
N_EXPERTS = 16
EXPERT_FF = 256
EPS = 1e-6

LANES = 128
QK_DEPTH = LANES
V_ROWS = HEAD_DIM + 16
SUPER = 4
SG_CH = SSM_WIDTH // SUPER
SG_ST = SSM_GROUPS // SUPER * SSM_STATE
ROUTER_COLS = LANES

F32 = jnp.float32
BF16 = jnp.bfloat16
VMEM_LIMIT = 56 * 1024 * 1024
LOG2E = 1.4426950408889634
ATTN_HEADS = 4
ATTN_TILE = 512


def _cparams(sem):
    return pltpu.CompilerParams(dimension_semantics=sem, vmem_limit_bytes=VMEM_LIMIT)


def _rms(x, g):
    return x * lax.rsqrt(jnp.mean(x * x, axis=-1, keepdims=True) + EPS) * g


def _proj_kernel(x_ref, g_ref, w_ref, bf_ref, bg_ref,
                 q_ref, k_ref, v_ref, u_ref, gate_ref, f_ref):
    h = _rms(x_ref[...], g_ref[...]).astype(BF16)

    def mm(lo, hi):
        return jnp.dot(h, w_ref[:, lo:hi], preferred_element_type=F32)

    w = FOX_WIDTH
    q_ref[...] = mm(0, w).astype(BF16)
    k_ref[...] = mm(w, 2 * w).astype(BF16)
    v_ref[...] = mm(2 * w, 3 * w).astype(BF16)
    u_ref[...] = mm(3 * w, 4 * w).astype(BF16)
    for c in range(4):
        lo = 4 * w + c * w
        gate_ref[:, c * w:(c + 1) * w] = jax.nn.sigmoid(
            mm(lo, lo + w) + bg_ref[:, c * w:(c + 1) * w]).astype(BF16)
    f_ref[...] = mm(8 * w, 8 * w + LANES) + bf_ref[...]


def _proj(x2, g_mix, w1, bf_pad, b_gate, tm):
    t = x2.shape[0]
    n1 = w1.shape[1]
    row = lambda i: (i, 0)
    const = lambda i: (0, 0)
    return pl.pallas_call(
        _proj_kernel,
        grid=(t // tm,),
        in_specs=[
            pl.BlockSpec((tm, D_MODEL), row),
            pl.BlockSpec((1, D_MODEL), const),
            pl.BlockSpec((D_MODEL, n1), const),
            pl.BlockSpec((1, LANES), const),
            pl.BlockSpec((1, 2 * D_MODEL), const),
        ],
        out_specs=[
            pl.BlockSpec((tm, FOX_WIDTH), row),
            pl.BlockSpec((tm, FOX_WIDTH), row),
            pl.BlockSpec((tm, FOX_WIDTH), row),
            pl.BlockSpec((tm, SSM_WIDTH), row),
            pl.BlockSpec((tm, 2 * D_MODEL), row),
            pl.BlockSpec((tm, LANES), row),
        ],
        out_shape=[
            jax.ShapeDtypeStruct((t, FOX_WIDTH), BF16),
            jax.ShapeDtypeStruct((t, FOX_WIDTH), BF16),
            jax.ShapeDtypeStruct((t, FOX_WIDTH), BF16),
            jax.ShapeDtypeStruct((t, SSM_WIDTH), BF16),
            jax.ShapeDtypeStruct((t, 2 * D_MODEL), BF16),
            jax.ShapeDtypeStruct((t, LANES), F32),
        ],
        compiler_params=_cparams(("arbitrary",)),
        name="proj",
    )(x2, g_mix, w1, bf_pad, b_gate)


def _cum_kernel(f_ref, c_ref):
    x = f_ref[...]
    c = (jnp.minimum(x, 0.0) - jnp.log(1.0 + jnp.exp(-jnp.abs(x)))) * LOG2E
    seq = x.shape[1]
    lane = lax.broadcasted_iota(jnp.int32, x.shape, 1)
    sh = 1
    while sh < seq:
        c = c + jnp.where(lane >= sh, pltpu.roll(c, sh, axis=1), 0.0)
        sh *= 2
    c1 = c.astype(BF16).astype(F32)
    r = c - c1
    c2 = r.astype(BF16).astype(F32)
    c3 = (r - c2).astype(BF16).astype(F32)
    c_ref[0] = c1
    c_ref[1] = c2
    c_ref[2] = c3


def _cum(ft):
    b, h, s = ft.shape
    return pl.pallas_call(
        _cum_kernel,
        grid=(b,),
        in_specs=[pl.BlockSpec((None, h, s), lambda i: (i, 0, 0))],
        out_specs=pl.BlockSpec((None, 3, h, s), lambda i: (i, 0, 0, 0)),
        out_shape=jax.ShapeDtypeStruct((b, 3, h, s), F32),
        compiler_params=_cparams(("arbitrary",)),
        name="cum",
    )(ft)


def _attn_kernel(qt_ref, kx_ref, vt_ref, o_ref, m_scr, acc_scr, *, tile, heads):
    i = pl.program_id(2)
    m_scr[...] = jnp.full(m_scr.shape, -jnp.inf, F32)
    acc_scr[...] = jnp.zeros(acc_scr.shape, F32)

    def step(j, masked):
        ss = [jnp.dot(kx_ref[g, j], qt_ref[g], preferred_element_type=F32) for g in range(heads)]
        ps, alphas = [], []
        for g in range(heads):
            s = ss[g]
            if masked:
                kpos = lax.broadcasted_iota(jnp.int32, s.shape, 0)
                qpos = lax.broadcasted_iota(jnp.int32, s.shape, 1)
                s = jnp.where(kpos <= qpos, s, -jnp.inf)
            m = m_scr[g]
            m_new = jnp.maximum(m, jnp.max(s, axis=0, keepdims=True))
            ps.append(jnp.exp2(s - m_new).astype(BF16))
            alphas.append(jnp.exp2(m - m_new))
            m_scr[g] = m_new
        for g in range(heads):
            acc_scr[g] = (alphas[g] * acc_scr[g]
                          + jnp.dot(vt_ref[g, j], ps[g], preferred_element_type=F32))

    step(i, True)

    def body(j, carry):
        step(j, False)
        return carry

    lax.fori_loop(0, i, body, 0)
    for g in range(heads):
        acc = acc_scr[g]
        o_ref[g] = (acc[:HEAD_DIM] / acc[HEAD_DIM:HEAD_DIM + 1]).astype(BF16)


def _attn(qt, kx, vt, tile, heads):
    b, h, _, s = qt.shape
    n = s // tile
    return pl.pallas_call(
        functools.partial(_attn_kernel, tile=tile, heads=heads),
        grid=(b, h // heads, n),
        in_specs=[
            pl.BlockSpec((None, heads, QK_DEPTH, tile), lambda bi, hi, i: (bi, hi, 0, i)),
            pl.BlockSpec((None, heads, n, tile, QK_DEPTH), lambda bi, hi, i: (bi, hi, 0, 0, 0)),
            pl.BlockSpec((None, heads, n, V_ROWS, tile), lambda bi, hi, i: (bi, hi, 0, 0, 0)),
        ],
        out_specs=pl.BlockSpec((None, heads, HEAD_DIM, tile), lambda bi, hi, i: (bi, hi, 0, i)),
        out_shape=jax.ShapeDtypeStruct((b, h, HEAD_DIM, s), BF16),
        scratch_shapes=[
            pltpu.VMEM((heads, 1, tile), F32),
            pltpu.VMEM((heads, V_ROWS, tile), F32),
        ],
        compiler_params=_cparams(("arbitrary", "arbitrary", "arbitrary")),
        name="attn",
    )(qt, kx, vt)


def _ssm_kernel(u_ref, bm_ref, cm_ref, ar_ref, ai_ref, d_ref, y_ref, x_scr, s_scr, *, tb):
    rows = tb * 8

    @pl.when(pl.program_id(0) == 0)
    def _():
        s_scr[...] = jnp.zeros_like(s_scr)

    u = u_ref[...]
    is_re = (lax.broadcasted_iota(jnp.int32, (rows, SG_CH), 0) % 8) < 4
    zero = jnp.zeros((rows, SG_CH), BF16)
    for g in range(SUPER):
        ug = u[:, g * SG_CH:(g + 1) * SG_CH]
        lhs = jnp.concatenate([jnp.where(is_re, ug, zero), jnp.where(is_re, zero, ug)], axis=1)
        x_scr[:, g * SG_ST:(g + 1) * SG_ST] = jnp.dot(lhs, bm_ref[g], preferred_element_type=F32)

    half = 2 * SG_ST
    for c in range(SUPER * SG_ST // half):
        lo = c * half
        ar = ar_ref[:, lo:lo + half]
        ai = ai_ref[:, lo:lo + half]

        def body(t, s, lo=lo, ar=ar, ai=ai):
            r0 = pl.multiple_of(t * 8, 8)
            s = ar * s + ai * pltpu.roll(s, 4, axis=0) + x_scr[pl.ds(r0, 8), lo:lo + half]
            x_scr[pl.ds(r0, 8), lo:lo + half] = s
            return s

        s_scr[:, lo:lo + half] = lax.fori_loop(0, tb, body, s_scr[:, lo:lo + half])

    for g in range(SUPER):
        st = x_scr[:, g * SG_ST:(g + 1) * SG_ST].astype(BF16)
        o = jnp.dot(st, cm_ref[g], preferred_element_type=F32)
        y = o[:, :SG_CH] + pltpu.roll(o[:, SG_CH:], rows - 4, axis=0)
        y = y + d_ref[:, g * SG_CH:(g + 1) * SG_CH] * u[:, g * SG_CH:(g + 1) * SG_CH].astype(F32)
        y_ref[:, g * SG_CH:(g + 1) * SG_CH] = y.astype(BF16)


def _ssm(u8, bm, cm, ar, ai, d, tb):
    rows_total = u8.shape[0]
    rows = tb * 8
    return pl.pallas_call(
        functools.partial(_ssm_kernel, tb=tb),
        grid=(rows_total // rows,),
        in_specs=[
            pl.BlockSpec((rows, SSM_WIDTH), lambda i: (i, 0)),
            pl.BlockSpec((SUPER, 2 * SG_CH, SG_ST), lambda i: (0, 0, 0)),
            pl.BlockSpec((SUPER, SG_ST, 2 * SG_CH), lambda i: (0, 0, 0)),
            pl.BlockSpec((8, SUPER * SG_ST), lambda i: (0, 0)),
            pl.BlockSpec((8, SUPER * SG_ST), lambda i: (0, 0)),
            pl.BlockSpec((1, SSM_WIDTH), lambda i: (0, 0)),
        ],
        out_specs=pl.BlockSpec((rows, SSM_WIDTH), lambda i: (i, 0)),
        out_shape=jax.ShapeDtypeStruct((rows_total, SSM_WIDTH), BF16),
        scratch_shapes=[
            pltpu.VMEM((rows, SUPER * SG_ST), F32),
            pltpu.VMEM((8, SUPER * SG_ST), F32),
        ],
        compiler_params=_cparams(("arbitrary",)),
        name="ssm",
    )(u8, bm, cm, ar, ai, d)


def _merge_kernel(x_ref, o_ref, ys_ref, gate_ref, woa_ref, wglu_ref, bglu_ref, wob_ref, wout_ref,
                  x1_ref):
    y_a = jnp.dot(o_ref[...], woa_ref[...], preferred_element_type=F32)
    z = jax.nn.gelu(ys_ref[...].astype(F32))
    zg = jnp.dot(z.astype(BF16), wglu_ref[...], preferred_element_type=F32) + bglu_ref[...]
    z = z * jax.nn.sigmoid(zg)
    y_b = jnp.dot(z.astype(BF16), wob_ref[...], preferred_element_type=F32)
    g0 = gate_ref[:, :D_MODEL].astype(F32)
    g1 = gate_ref[:, D_MODEL:].astype(F32)
    merged = (g0 * y_a + g1 * y_b).astype(BF16)
    x1_ref[...] = x_ref[...] + jnp.dot(merged, wout_ref[...], preferred_element_type=F32)


def _merge(x2, o, ys, gates, woa, wglu, bglu, wob, wout, tm):
    t = x2.shape[0]
    row = lambda i: (i, 0)
    const = lambda i: (0, 0)
    return pl.pallas_call(
        _merge_kernel,
        grid=(t // tm,),
        in_specs=[
            pl.BlockSpec((tm, D_MODEL), row),
            pl.BlockSpec((tm, FOX_WIDTH), row),
            pl.BlockSpec((tm, SSM_WIDTH), row),
            pl.BlockSpec((tm, 2 * D_MODEL), row),
            pl.BlockSpec((FOX_WIDTH, D_MODEL), const),
            pl.BlockSpec((SSM_WIDTH, SSM_WIDTH), const),
            pl.BlockSpec((1, SSM_WIDTH), const),
            pl.BlockSpec((SSM_WIDTH, D_MODEL), const),
            pl.BlockSpec((D_MODEL, D_MODEL), const),
        ],
        out_specs=pl.BlockSpec((tm, D_MODEL), row),
        out_shape=jax.ShapeDtypeStruct((t, D_MODEL), F32),
        compiler_params=_cparams(("arbitrary",)),
        name="merge",
    )(x2, o, ys, gates, woa, wglu, bglu, wob, wout)


def _combine_weights(logits):
    lane = lax.broadcasted_iota(jnp.int32, logits.shape, 1)
    neg = -jnp.inf
    big = jnp.int32(ROUTER_COLS)
    is_g = lane < N_GROUPS
    lg = jnp.where(is_g, logits, neg)
    gmax = jnp.max(lg, axis=1, keepdims=True)
    p_sel = 1.0 / jnp.sum(jnp.exp(lg - gmax), axis=1, keepdims=True)
    g_idx = jnp.min(jnp.where(lg == gmax, lane, big), axis=1, keepdims=True)
    lo = N_GROUPS + g_idx * EXPERTS_PER_GROUP
    in_grp = (lane >= lo) & (lane < lo + EXPERTS_PER_GROUP)
    le = jnp.where(in_grp, logits, neg)
    v1 = jnp.max(le, axis=1, keepdims=True)
    i1 = jnp.min(jnp.where(le == v1, lane, big), axis=1, keepdims=True)
    le2 = jnp.where(lane == i1, neg, le)
    v2 = jnp.max(le2, axis=1, keepdims=True)
    i2 = jnp.min(jnp.where(le2 == v2, lane, big), axis=1, keepdims=True)
    e2 = jnp.exp(v2 - v1)
    w1 = p_sel / (1.0 + e2)
    w2 = p_sel * e2 / (1.0 + e2)
    return jnp.where(lane == i1, w1, 0.0) + jnp.where(lane == i2, w2, 0.0)


def _moe_kernel(x1_ref, gf_ref, wr_ref, br_ref, wg_ref, wu_ref, wd_ref, gfin_ref, out_ref,
                h_scr, cw_scr, acc_scr):
    e = pl.program_id(1)

    @pl.when(e == 0)
    def _():
        h = _rms(x1_ref[...], gf_ref[...])
        h_hi = h.astype(BF16)
        h_lo = (h - h_hi.astype(F32)).astype(BF16)
        logits = (jnp.dot(h_hi, wr_ref[0], preferred_element_type=F32)
                  + jnp.dot(h_lo, wr_ref[0], preferred_element_type=F32)
                  + jnp.dot(h_hi, wr_ref[1], preferred_element_type=F32)) + br_ref[...]
        h_scr[...] = h_hi
        cw_scr[...] = _combine_weights(logits)
        acc_scr[...] = jnp.zeros_like(acc_scr)

    h = h_scr[...]
    hg = jnp.dot(h, wg_ref[...], preferred_element_type=F32)
    hu = jnp.dot(h, wu_ref[...], preferred_element_type=F32)
    lane = lax.broadcasted_iota(jnp.int32, cw_scr.shape, 1)
    cw = jnp.sum(jnp.where(lane == N_GROUPS + e, cw_scr[...], 0.0), axis=1, keepdims=True)
    act = (hg * jax.nn.sigmoid(hg) * hu * cw).astype(BF16)
    acc_scr[...] += jnp.dot(act, wd_ref[...], preferred_element_type=F32)

    @pl.when(e == N_EXPERTS - 1)
    def _():
        out_ref[...] = _rms(x1_ref[...] + acc_scr[...], gfin_ref[...])


def _moe(x1, g_ffn, wr, br, wg, wu, wd, g_final, tm):
    t = x1.shape[0]
    return pl.pallas_call(
        _moe_kernel,
        grid=(t // tm, N_EXPERTS),
        in_specs=[
            pl.BlockSpec((tm, D_MODEL), lambda i, e: (i, 0)),
            pl.BlockSpec((1, D_MODEL), lambda i, e: (0, 0)),
            pl.BlockSpec((2, D_MODEL, ROUTER_COLS), lambda i, e: (0, 0, 0)),
            pl.BlockSpec((1, ROUTER_COLS), lambda i, e: (0, 0)),
            pl.BlockSpec((None, D_MODEL, EXPERT_FF), lambda i, e: (e, 0, 0)),
            pl.BlockSpec((None, D_MODEL, EXPERT_FF), lambda i, e: (e, 0, 0)),
            pl.BlockSpec((None, EXPERT_FF, D_MODEL), lambda i, e: (e, 0, 0)),
            pl.BlockSpec((1, D_MODEL), lambda i, e: (0, 0)),
        ],
        out_specs=pl.BlockSpec((tm, D_MODEL), lambda i, e: (i, 0)),
        out_shape=jax.ShapeDtypeStruct((t, D_MODEL), F32),
        scratch_shapes=[
            pltpu.VMEM((tm, D_MODEL), BF16),
            pltpu.VMEM((tm, ROUTER_COLS), F32),
            pltpu.VMEM((tm, D_MODEL), F32),
        ],
        compiler_params=_cparams(("arbitrary", "arbitrary")),
        name="moe",
    )(x1, g_ffn, wr, br, wg, wu, wd, g_final)


def _ssm_params(lambda_re, lambda_im, log_step, b_re, b_im, c_re, c_im):
    dt = jnp.exp(log_step)[:, None]
    mag = jnp.exp(lambda_re * dt)
    a_re = mag * jnp.cos(lambda_im * dt)
    a_im = mag * jnp.sin(lambda_im * dt)
    den = lambda_re * lambda_re + lambda_im * lambda_im
    num_re = a_re - 1.0
    z_re = (num_re * lambda_re + a_im * lambda_im) / den
    z_im = (a_im * lambda_re - num_re * lambda_im) / den
    bb_re = z_re[..., None] * b_re - z_im[..., None] * b_im
    bb_im = z_re[..., None] * b_im + z_im[..., None] * b_re
    gl = SSM_GROUPS // SUPER
    eye = jnp.eye(gl, dtype=F32)

    def in_blk(bb):
        t = bb.reshape(SUPER, gl, SSM_STATE, SSM_GROUP).transpose(0, 1, 3, 2)
        t = t[:, :, :, None, :] * eye[None, :, None, :, None]
        return t.reshape(SUPER, gl * SSM_GROUP, gl * SSM_STATE)

    def out_blk(cc):
        t = cc.reshape(SUPER, gl, SSM_GROUP, SSM_STATE).transpose(0, 1, 3, 2)
        t = t[:, :, :, None, :] * eye[None, :, None, :, None]
        return t.reshape(SUPER, gl * SSM_STATE, gl * SSM_GROUP)

    bm = jnp.concatenate([in_blk(bb_re), in_blk(bb_im)], axis=1).astype(BF16)
    cm = jnp.concatenate([out_blk(c_re), out_blk(-c_im)], axis=2).astype(BF16)
    ar_row = a_re.reshape(1, SSM_GROUPS * SSM_STATE)
    ai_row = a_im.reshape(1, SSM_GROUPS * SSM_STATE)
    ar = jnp.broadcast_to(ar_row, (8, SSM_GROUPS * SSM_STATE))
    ai = jnp.concatenate([jnp.broadcast_to(-ai_row, (4, ai_row.shape[1])),
                          jnp.broadcast_to(ai_row, (4, ai_row.shape[1]))], axis=0)
    return bm, cm, ar, ai


def _pick(n, pref):
    return pref if n % pref == 0 else n


def _mixers(x, g_mix, w_in, b_forget, b_gate, w_out_a, lambda_re, lambda_im, log_step,
            ssm_b_re, ssm_b_im, ssm_c_re, ssm_c_im, ssm_d, w_glu, b_glu, w_out_b, w_out):
    bsz, seq, _ = x.shape
    assert bsz == 4, "the SSM kernel packs (re/im) x 4 batches onto the 8 sublanes"
    t = bsz * seq
    layer = 0
    x2 = x.reshape(t, D_MODEL)

    w = w_in[layer]
    fw = FOX_WIDTH
    wq, wk, wv = w[:, :fw], w[:, fw:2 * fw], w[:, 2 * fw:3 * fw]
    wf = w[:, 3 * fw:3 * fw + HEADS]
    wu = w[:, 3 * fw + HEADS:3 * fw + HEADS + SSM_WIDTH]
    wgt = w[:, 3 * fw + HEADS + SSM_WIDTH:]
    scale = LOG2E / math.sqrt(HEAD_DIM)
    w1 = jnp.concatenate(
        [wq * scale, wk, wv, wu, wgt, jnp.pad(wf, ((0, 0), (0, LANES - HEADS)))], axis=1).astype(BF16)
    bf_pad = jnp.pad(b_forget[layer], (0, LANES - HEADS)).reshape(1, LANES)

    tm = _pick(t, 512)
    q, k, v, u, gates, f = _proj(x2, g_mix[layer].reshape(1, D_MODEL), w1, bf_pad,
                                 b_gate[layer].reshape(1, 2 * D_MODEL), tm)

    ft = f[:, :HEADS].reshape(bsz, seq, HEADS).transpose(0, 2, 1)
    c3 = _cum(ft).astype(BF16)

    tile = _pick(seq, ATTN_TILE)
    n = seq // tile
    cq = c3.transpose(0, 2, 1, 3)
    ones_q = jnp.ones((bsz, HEADS, 3, seq), BF16)
    qh = q.reshape(bsz, seq, HEADS, HEAD_DIM).transpose(0, 2, 3, 1)
    qt = jnp.concatenate(
        [qh, cq, ones_q, jnp.zeros((bsz, HEADS, QK_DEPTH - HEAD_DIM - 6, seq), BF16)], axis=2)
    kh = k.reshape(bsz, seq, HEADS, HEAD_DIM).transpose(0, 2, 1, 3)
    ck = (-c3).transpose(0, 2, 3, 1)
    kx = jnp.concatenate(
        [kh, jnp.ones((bsz, HEADS, seq, 3), BF16), ck,
         jnp.zeros((bsz, HEADS, seq, QK_DEPTH - HEAD_DIM - 6), BF16)], axis=3)
    kx = kx.reshape(bsz, HEADS, n, tile, QK_DEPTH)
    vh = v.reshape(bsz, seq, HEADS, HEAD_DIM).transpose(0, 2, 3, 1)
    vt = jnp.concatenate(
        [vh, jnp.ones((bsz, HEADS, 1, seq), BF16),
         jnp.zeros((bsz, HEADS, V_ROWS - HEAD_DIM - 1, seq), BF16)], axis=2)
    vt = vt.reshape(bsz, HEADS, V_ROWS, n, tile).transpose(0, 1, 3, 2, 4)
    o_t = _attn(qt, kx, vt, tile, ATTN_HEADS)
    o = o_t.transpose(0, 3, 1, 2).reshape(t, FOX_WIDTH)

    bm, cm, ar, ai = _ssm_params(lambda_re[layer], lambda_im[layer], log_step[layer],
                                 ssm_b_re[layer], ssm_b_im[layer], ssm_c_re[layer], ssm_c_im[layer])
    ub = u.reshape(bsz, seq, SSM_WIDTH).transpose(1, 0, 2)
    u8 = jnp.concatenate([ub, ub], axis=1).reshape(seq * 8, SSM_WIDTH)
    tb = _pick(seq, 128)
    y8 = _ssm(u8, bm, cm, ar, ai, ssm_d[layer].reshape(1, SSM_WIDTH), tb)
    ys = y8.reshape(seq, 8, SSM_WIDTH)[:, :bsz].transpose(1, 0, 2).reshape(t, SSM_WIDTH)

    x1 = _merge(x2, o, ys, gates, w_out_a[layer].astype(BF16), w_glu[layer].astype(BF16),
                b_glu[layer].reshape(1, SSM_WIDTH), w_out_b[layer].astype(BF16),
                w_out[layer].astype(BF16), tm)
    return x1, o, ys


def kernel(x, g_mix, w_in, b_forget, b_gate, w_out_a, lambda_re, lambda_im, log_step, ssm_b_re, ssm_b_im, ssm_c_re, ssm_c_im, ssm_d, w_glu, b_glu, w_out_b, w_out, g_ffn, w_router_group, b_router_group, w_router_expert, b_router_expert, w_exp_gate, w_exp_up, w_exp_down, g_final):
    bsz, seq, _ = x.shape
    t = bsz * seq
    layer = 0
    x1, _, _ = _mixers(x, g_mix, w_in, b_forget, b_gate, w_out_a, lambda_re, lambda_im, log_step,
                       ssm_b_re, ssm_b_im, ssm_c_re, ssm_c_im, ssm_d, w_glu, b_glu, w_out_b, w_out)

    wr_f = jnp.concatenate([w_router_group[layer], w_router_expert[layer]], axis=1)
    wr_f = jnp.pad(wr_f, ((0, 0), (0, ROUTER_COLS - N_GROUPS - N_EXPERTS)))
    wr_hi = wr_f.astype(BF16)
    wr_lo = (wr_f - wr_hi.astype(F32)).astype(BF16)
    wr = jnp.stack([wr_hi, wr_lo])
    br = jnp.pad(jnp.concatenate([b_router_group[layer], b_router_expert[layer]]),
                 (0, ROUTER_COLS - N_GROUPS - N_EXPERTS)).reshape(1, ROUTER_COLS)
    tmoe = _pick(t, 1024)
    out = _moe(x1, g_ffn[layer].reshape(1, D_MODEL), wr, br, w_exp_gate[layer].astype(BF16),
               w_exp_up[layer].astype(BF16), w_exp_down[layer].astype(BF16),
               g_final.reshape(1, D_MODEL), tmoe)
    return out.reshape(bsz, seq, D_MODEL)
```

```python
import functools
import math

import jax
import jax.numpy as jnp
from jax import lax
from jax.experimental import pallas as pl
from jax.experimental.pallas import tpu as pltpu

D_MODEL = 1024
HEADS = 8
HEAD_DIM = 64
FOX_WIDTH = HEADS * HEAD_DIM
SSM_WIDTH = 512
SSM_GROUP = 16
SSM_GROUPS = 32
SSM_STATE = 64
N_GROUPS = 4
EXPERTS_PER_GROUP = 4
N_EXPERTS = 16
EXPERT_FF = 256
EPS = 1e-6

LANES = 128
HEAD_BLOCK = LANES
AUX = HEAD_BLOCK - HEAD_DIM
V_ROWS = HEAD_DIM + 16
SUPER = 4
SG_CH = SSM_WIDTH // SUPER
SG_ST = SSM_GROUPS // SUPER * SSM_STATE
ROUTER_COLS = LANES

F32 = jnp.float32
BF16 = jnp.bfloat16
VMEM_LIMIT = 56 * 1024 * 1024
LOG2E = 1.4426950408889634
ATTN_HEADS = 4
ATTN_TILE = 512


def _cparams(sem):
    return pltpu.CompilerParams(dimension_semantics=sem, vmem_limit_bytes=VMEM_LIMIT)


def _rms(x, g):
    return x * lax.rsqrt(jnp.mean(x * x, axis=-1, keepdims=True) + EPS) * g


def _proj_kernel(x_ref, g_ref, w_ref, bf_ref, bg_ref, selq_ref, selk_ref, auxc_ref,
                 q_ref, k_ref, v_ref, u_ref, gate_ref, carry_scr, *, tiles_per_seq):
    tm = x_ref.shape[0]

    @pl.when(pl.program_id(0) % tiles_per_seq == 0)
    def _():
        carry_scr[...] = jnp.zeros_like(carry_scr)

    h = _rms(x_ref[...], g_ref[...]).astype(BF16)

    def mm(lo, hi):
        return jnp.dot(h, w_ref[:, lo:hi], preferred_element_type=F32)

    w = FOX_WIDTH
    f = mm(8 * w, 8 * w + LANES) + bf_ref[...]
    lane = lax.broadcasted_iota(jnp.int32, f.shape, 1)
    row = lax.broadcasted_iota(jnp.int32, f.shape, 0)
    c = jnp.where(lane < HEADS, (jnp.minimum(f, 0.0) - jnp.log(1.0 + jnp.exp(-jnp.abs(f)))) * LOG2E, 0.0)
    sh = 1
    while sh < tm:
        c = c + jnp.where(row >= sh, pltpu.roll(c, sh, axis=0), 0.0)
        sh *= 2
    c = c + carry_scr[...]
    carry_scr[...] = c[tm - 1:tm, :]
    c1 = c.astype(BF16)
    r = c - c1.astype(F32)
    c2 = r.astype(BF16)
    c3 = (r - c2.astype(F32)).astype(BF16)

    def place(sel_ref):
        return (jnp.dot(c1, sel_ref[0], preferred_element_type=F32)
                + jnp.dot(c2, sel_ref[1], preferred_element_type=F32)
                + jnp.dot(c3, sel_ref[2], preferred_element_type=F32))

    aux_q = place(selq_ref) + auxc_ref[0:1, :]
    aux_k = place(selk_ref) + auxc_ref[1:2, :]
    aux_v = jnp.broadcast_to(auxc_ref[2:3, :], aux_q.shape)

    for out_ref, lo, aux in ((q_ref, 0, aux_q), (k_ref, w, aux_k), (v_ref, 2 * w, aux_v)):
        val = mm(lo, lo + w)
        for hd in range(HEADS):
            blk = jnp.concatenate([val[:, hd * HEAD_DIM:(hd + 1) * HEAD_DIM],
                                   aux[:, hd * AUX:(hd + 1) * AUX]], axis=1)
            out_ref[:, hd * HEAD_BLOCK:(hd + 1) * HEAD_BLOCK] = blk.astype(BF16)

    u_ref[...] = mm(3 * w, 4 * w).astype(BF16)
    for cc in range(4):
        lo = 4 * w + cc * w
        gate_ref[:, cc * w:(cc + 1) * w] = jax.nn.sigmoid(
            mm(lo, lo + w) + bg_ref[:, cc * w:(cc + 1) * w]).astype(BF16)


def _proj(x2, g_mix, w1, bf_pad, b_gate, selq, selk, auxc, tm, seq):
    t = x2.shape[0]
    n1 = w1.shape[1]
    hb = HEADS * HEAD_BLOCK
    row = lambda i: (i, 0)
    const = lambda i: (0, 0)
    const3 = lambda i: (0, 0, 0)
    return pl.pallas_call(
        functools.partial(_proj_kernel, tiles_per_seq=seq // tm),
        grid=(t // tm,),
        in_specs=[
            pl.BlockSpec((tm, D_MODEL), row),
            pl.BlockSpec((1, D_MODEL), const),
            pl.BlockSpec((D_MODEL, n1), const),
            pl.BlockSpec((1, LANES), const),
            pl.BlockSpec((1, 2 * D_MODEL), const),
            pl.BlockSpec((3, LANES, HEADS * AUX), const3),
            pl.BlockSpec((3, LANES, HEADS * AUX), const3),
            pl.BlockSpec((8, HEADS * AUX), const),
        ],
        out_specs=[
            pl.BlockSpec((tm, hb), row),
            pl.BlockSpec((tm, hb), row),
            pl.BlockSpec((tm, hb), row),
            pl.BlockSpec((tm, SSM_WIDTH), row),
            pl.BlockSpec((tm, 2 * D_MODEL), row),
        ],
        out_shape=[
            jax.ShapeDtypeStruct((t, hb), BF16),
            jax.ShapeDtypeStruct((t, hb), BF16),
            jax.ShapeDtypeStruct((t, hb), BF16),
            jax.ShapeDtypeStruct((t, SSM_WIDTH), BF16),
            jax.ShapeDtypeStruct((t, 2 * D_MODEL), BF16),
        ],
        scratch_shapes=[pltpu.VMEM((1, LANES), F32)],
        compiler_params=_cparams(("arbitrary",)),
        name="proj",
    )(x2, g_mix, w1, bf_pad, b_gate, selq, selk, auxc)


def _aux_constants():
    selq = jnp.zeros((3, LANES, HEADS * AUX), F32)
    selk = jnp.zeros((3, LANES, HEADS * AUX), F32)
    auxc = jnp.zeros((8, HEADS * AUX), F32)
    hd = jnp.arange(HEADS)
    for p in range(3):
        selq = selq.at[p, hd, hd * AUX + p].set(1.0)
        selk = selk.at[p, hd, hd * AUX + 3 + p].set(-1.0)
        auxc = auxc.at[0, hd * AUX + 3 + p].set(1.0)
        auxc = auxc.at[1, hd * AUX + p].set(1.0)
    auxc = auxc.at[2, hd * AUX].set(1.0)
    return selq.astype(BF16), selk.astype(BF16), auxc


def _attn_kernel(q_ref, k_ref, v_ref, o_ref, vt_scr, m_scr, acc_scr, *, tile, heads):
    i = pl.program_id(2)
    n = k_ref.shape[0] // tile
    nt = (((1,), (1,)), ((), ()))

    @pl.when(i == 0)
    def _():
        for g in range(heads):
            def tr(c, carry, g=g):
                r0 = pl.multiple_of(c * tile, tile)
                blk = v_ref[pl.ds(r0, tile), g * HEAD_BLOCK:(g + 1) * HEAD_BLOCK].astype(F32)
                vt_scr[g, c] = blk.T[:V_ROWS].astype(BF16)
                return carry
            lax.fori_loop(0, n, tr, 0)

    m_scr[...] = jnp.full(m_scr.shape, -jnp.inf, F32)
    acc_scr[...] = jnp.zeros(acc_scr.shape, F32)

    def step(j, masked):
        r0 = pl.multiple_of(j * tile, tile)
        ss = [lax.dot_general(k_ref[pl.ds(r0, tile), g * HEAD_BLOCK:(g + 1) * HEAD_BLOCK],
                              q_ref[:, g * HEAD_BLOCK:(g + 1) * HEAD_BLOCK], nt,
                              preferred_element_type=F32) for g in range(heads)]
        ps, alphas = [], []
        for g in range(heads):
            s = ss[g]
            if masked:
                kpos = lax.broadcasted_iota(jnp.int32, s.shape, 0)
                qpos = lax.broadcasted_iota(jnp.int32, s.shape, 1)
                s = jnp.where(kpos <= qpos, s, -jnp.inf)
            m = m_scr[g]
            m_new = jnp.maximum(m, jnp.max(s, axis=0, keepdims=True))
            ps.append(jnp.exp2(s - m_new).astype(BF16))
            alphas.append(jnp.exp2(m - m_new))
            m_scr[g] = m_new
        for g in range(heads):
            acc_scr[g] = (alphas[g] * acc_scr[g]
                          + jnp.dot(vt_scr[g, j], ps[g], preferred_element_type=F32))

    step(i, True)

    def body(j, carry):
        step(j, False)
        return carry

    lax.fori_loop(0, i, body, 0)
    for pair in range(heads // 2):
        a0 = acc_scr[2 * pair]
        a1 = acc_scr[2 * pair + 1]
        o2 = jnp.concatenate([a0[:HEAD_DIM] / a0[HEAD_DIM:HEAD_DIM + 1],
                              a1[:HEAD_DIM] / a1[HEAD_DIM:HEAD_DIM + 1]], axis=0)
        o_ref[:, pair * LANES:(pair + 1) * LANES] = o2.T.astype(BF16)


def _attn(qx, kx, vx, tile, heads):
    b, s, _ = qx.shape
    n = s // tile
    hb = heads * HEAD_BLOCK
    return pl.pallas_call(
        functools.partial(_attn_kernel, tile=tile, heads=heads),
        grid=(b, HEADS // heads, n),
        in_specs=[
            pl.BlockSpec((None, tile, hb), lambda bi, hi, i: (bi, i, hi)),
            pl.BlockSpec((None, s, hb), lambda bi, hi, i: (bi, 0, hi)),
            pl.BlockSpec((None, s, hb), lambda bi, hi, i: (bi, 0, hi)),
        ],
        out_specs=pl.BlockSpec((None, tile, heads * HEAD_DIM), lambda bi, hi, i: (bi, i, hi)),
        out_shape=jax.ShapeDtypeStruct((b, s, FOX_WIDTH), BF16),
        scratch_shapes=[
            pltpu.VMEM((heads, n, V_ROWS, tile), BF16),
            pltpu.VMEM((heads, 1, tile), F32),
            pltpu.VMEM((heads, V_ROWS, tile), F32),
        ],
        compiler_params=_cparams(("arbitrary", "arbitrary", "arbitrary")),
        name="attn",
    )(qx, kx, vx)


def _ssm_kernel(u_ref, bm_ref, cm_ref, ar_ref, ai_ref, d_ref, y_ref, x_scr, s_scr, *, tb):
    rows = tb * 8

    @pl.when(pl.program_id(0) == 0)
    def _():
        s_scr[...] = jnp.zeros_like(s_scr)

    u = u_ref[...]
    is_re = (lax.broadcasted_iota(jnp.int32, (rows, SG_CH), 0) % 8) < 4
    zero = jnp.zeros((rows, SG_CH), BF16)
    for g in range(SUPER):
        ug = u[:, g * SG_CH:(g + 1) * SG_CH]
        lhs = jnp.concatenate([jnp.where(is_re, ug, zero), jnp.where(is_re, zero, ug)], axis=1)
        x_scr[:, g * SG_ST:(g + 1) * SG_ST] = jnp.dot(lhs, bm_ref[g], preferred_element_type=F32)

    half = 2 * SG_ST
    for c in range(SUPER * SG_ST // half):
        lo = c * half
        ar = ar_ref[:, lo:lo + half]
        ai = ai_ref[:, lo:lo + half]

        def body(t, s, lo=lo, ar=ar, ai=ai):
            r0 = pl.multiple_of(t * 8, 8)
            s = ar * s + ai * pltpu.roll(s, 4, axis=0) + x_scr[pl.ds(r0, 8), lo:lo + half]
            x_scr[pl.ds(r0, 8), lo:lo + half] = s
            return s

        s_scr[:, lo:lo + half] = lax.fori_loop(0, tb, body, s_scr[:, lo:lo + half])

    for g in range(SUPER):
        st = x_scr[:, g * SG_ST:(g + 1) * SG_ST].astype(BF16)
        o = jnp.dot(st, cm_ref[g], preferred_element_type=F32)
        y = o[:, :SG_CH] + pltpu.roll(o[:, SG_CH:], rows - 4, axis=0)
        y = y + d_ref[:, g * SG_CH:(g + 1) * SG_CH] * u[:, g * SG_CH:(g + 1) * SG_CH].astype(F32)
        y_ref[:, g * SG_CH:(g + 1) * SG_CH] = y.astype(BF16)


def _ssm(u8, bm, cm, ar, ai, d, tb):
    rows_total = u8.shape[0]
    rows = tb * 8
    return pl.pallas_call(
        functools.partial(_ssm_kernel, tb=tb),
        grid=(rows_total // rows,),
        in_specs=[
            pl.BlockSpec((rows, SSM_WIDTH), lambda i: (i, 0)),
            pl.BlockSpec((SUPER, 2 * SG_CH, SG_ST), lambda i: (0, 0, 0)),
            pl.BlockSpec((SUPER, SG_ST, 2 * SG_CH), lambda i: (0, 0, 0)),
            pl.BlockSpec((8, SUPER * SG_ST), lambda i: (0, 0)),
            pl.BlockSpec((8, SUPER * SG_ST), lambda i: (0, 0)),
            pl.BlockSpec((1, SSM_WIDTH), lambda i: (0, 0)),
        ],
        out_specs=pl.BlockSpec((rows, SSM_WIDTH), lambda i: (i, 0)),
        out_shape=jax.ShapeDtypeStruct((rows_total, SSM_WIDTH), BF16),
        scratch_shapes=[
            pltpu.VMEM((rows, SUPER * SG_ST), F32),
            pltpu.VMEM((8, SUPER * SG_ST), F32),
        ],
        compiler_params=_cparams(("arbitrary",)),
        name="ssm",
    )(u8, bm, cm, ar, ai, d)


def _merge_kernel(x_ref, o_ref, ys_ref, gate_ref, woa_ref, wglu_ref, bglu_ref, wob_ref, wout_ref,
                  x1_ref):
    y_a = jnp.dot(o_ref[...], woa_ref[...], preferred_element_type=F32)
    z = jax.nn.gelu(ys_ref[...].astype(F32))
    zg = jnp.dot(z.astype(BF16), wglu_ref[...], preferred_element_type=F32) + bglu_ref[...]
    z = z * jax.nn.sigmoid(zg)
    y_b = jnp.dot(z.astype(BF16), wob_ref[...], preferred_element_type=F32)
    g0 = gate_ref[:, :D_MODEL].astype(F32)
    g1 = gate_ref[:, D_MODEL:].astype(F32)
    merged = (g0 * y_a + g1 * y_b).astype(BF16)
    x1_ref[...] = x_ref[...] + jnp.dot(merged, wout_ref[...], preferred_element_type=F32)


def _merge(x2, o, ys, gates, woa, wglu, bglu, wob, wout, tm):
    t = x2.shape[0]
    row = lambda i: (i, 0)
    const = lambda i: (0, 0)
    return pl.pallas_call(
        _merge_kernel,
        grid=(t // tm,),
        in_specs=[
            pl.BlockSpec((tm, D_MODEL), row),
            pl.BlockSpec((tm, FOX_WIDTH), row),
            pl.BlockSpec((tm, SSM_WIDTH), row),
            pl.BlockSpec((tm, 2 * D_MODEL), row),
            pl.BlockSpec((FOX_WIDTH, D_MODEL), const),
            pl.BlockSpec((SSM_WIDTH, SSM_WIDTH), const),
            pl.BlockSpec((1, SSM_WIDTH), const),
            pl.BlockSpec((SSM_WIDTH, D_MODEL), const),
            pl.BlockSpec((D_MODEL, D_MODEL), const),
        ],
        out_specs=pl.BlockSpec((tm, D_MODEL), row),
        out_shape=jax.ShapeDtypeStruct((t, D_MODEL), F32),
        compiler_params=_cparams(("arbitrary",)),
        name="merge",
    )(x2, o, ys, gates, woa, wglu, bglu, wob, wout)


def _combine_weights(logits):
    lane = lax.broadcasted_iota(jnp.int32, logits.shape, 1)
    neg = -jnp.inf
    big = jnp.int32(ROUTER_COLS)
    is_g = lane < N_GROUPS
    lg = jnp.where(is_g, logits, neg)
    gmax = jnp.max(lg, axis=1, keepdims=True)
    p_sel = 1.0 / jnp.sum(jnp.exp(lg - gmax), axis=1, keepdims=True)
    g_idx = jnp.min(jnp.where(lg == gmax, lane, big), axis=1, keepdims=True)
    lo = N_GROUPS + g_idx * EXPERTS_PER_GROUP
    in_grp = (lane >= lo) & (lane < lo + EXPERTS_PER_GROUP)
    le = jnp.where(in_grp, logits, neg)
    v1 = jnp.max(le, axis=1, keepdims=True)
    i1 = jnp.min(jnp.where(le == v1, lane, big), axis=1, keepdims=True)
    le2 = jnp.where(lane == i1, neg, le)
    v2 = jnp.max(le2, axis=1, keepdims=True)
    i2 = jnp.min(jnp.where(le2 == v2, lane, big), axis=1, keepdims=True)
    e2 = jnp.exp(v2 - v1)
    w1 = p_sel / (1.0 + e2)
    w2 = p_sel * e2 / (1.0 + e2)
    return jnp.where(lane == i1, w1, 0.0) + jnp.where(lane == i2, w2, 0.0)


def _moe_kernel(x1_ref, gf_ref, wr_ref, br_ref, wg_ref, wu_ref, wd_ref, gfin_ref, out_ref,
                h_scr, cw_scr, acc_scr):
    e = pl.program_id(1)

    @pl.when(e == 0)
    def _():
        h = _rms(x1_ref[...], gf_ref[...])
        h_hi = h.astype(BF16)
        h_lo = (h - h_hi.astype(F32)).astype(BF16)
        logits = (jnp.dot(h_hi, wr_ref[0], preferred_element_type=F32)
                  + jnp.dot(h_lo, wr_ref[0], preferred_element_type=F32)
                  + jnp.dot(h_hi, wr_ref[1], preferred_element_type=F32)) + br_ref[...]
        h_scr[...] = h_hi
        cw_scr[...] = _combine_weights(logits)
        acc_scr[...] = jnp.zeros_like(acc_scr)

    h = h_scr[...]
    hg = jnp.dot(h, wg_ref[...], preferred_element_type=F32)
    hu = jnp.dot(h, wu_ref[...], preferred_element_type=F32)
    lane = lax.broadcasted_iota(jnp.int32, cw_scr.shape, 1)
    cw = jnp.sum(jnp.where(lane == N_GROUPS + e, cw_scr[...], 0.0), axis=1, keepdims=True)
    act = (hg * jax.nn.sigmoid(hg) * hu * cw).astype(BF16)
    acc_scr[...] += jnp.dot(act, wd_ref[...], preferred_element_type=F32)

    @pl.when(e == N_EXPERTS - 1)
    def _():
        out_ref[...] = _rms(x1_ref[...] + acc_scr[...], gfin_ref[...])


def _moe(x1, g_ffn, wr, br, wg, wu, wd, g_final, tm):
    t = x1.shape[0]
    return pl.pallas_call(
        _moe_kernel,
        grid=(t // tm, N_EXPERTS),
        in_specs=[
            pl.BlockSpec((tm, D_MODEL), lambda i, e: (i, 0)),
            pl.BlockSpec((1, D_MODEL), lambda i, e: (0, 0)),
            pl.BlockSpec((2, D_MODEL, ROUTER_COLS), lambda i, e: (0, 0, 0)),
            pl.BlockSpec((1, ROUTER_COLS), lambda i, e: (0, 0)),
            pl.BlockSpec((None, D_MODEL, EXPERT_FF), lambda i, e: (e, 0, 0)),
            pl.BlockSpec((None, D_MODEL, EXPERT_FF), lambda i, e: (e, 0, 0)),
            pl.BlockSpec((None, EXPERT_FF, D_MODEL), lambda i, e: (e, 0, 0)),
            pl.BlockSpec((1, D_MODEL), lambda i, e: (0, 0)),
        ],
        out_specs=pl.BlockSpec((tm, D_MODEL), lambda i, e: (i, 0)),
        out_shape=jax.ShapeDtypeStruct((t, D_MODEL), F32),
        scratch_shapes=[
            pltpu.VMEM((tm, D_MODEL), BF16),
            pltpu.VMEM((tm, ROUTER_COLS), F32),
            pltpu.VMEM((tm, D_MODEL), F32),
        ],
        compiler_params=_cparams(("arbitrary", "arbitrary")),
        name="moe",
    )(x1, g_ffn, wr, br, wg, wu, wd, g_final)


def _ssm_params(lambda_re, lambda_im, log_step, b_re, b_im, c_re, c_im):
    dt = jnp.exp(log_step)[:, None]
    mag = jnp.exp(lambda_re * dt)
    a_re = mag * jnp.cos(lambda_im * dt)
    a_im = mag * jnp.sin(lambda_im * dt)
    den = lambda_re * lambda_re + lambda_im * lambda_im
    num_re = a_re - 1.0
    z_re = (num_re * lambda_re + a_im * lambda_im) / den
    z_im = (a_im * lambda_re - num_re * lambda_im) / den
    bb_re = z_re[..., None] * b_re - z_im[..., None] * b_im
    bb_im = z_re[..., None] * b_im + z_im[..., None] * b_re
    gl = SSM_GROUPS // SUPER
    eye = jnp.eye(gl, dtype=F32)

    def in_blk(bb):
        t = bb.reshape(SUPER, gl, SSM_STATE, SSM_GROUP).transpose(0, 1, 3, 2)
        t = t[:, :, :, None, :] * eye[None, :, None, :, None]
        return t.reshape(SUPER, gl * SSM_GROUP, gl * SSM_STATE)

    def out_blk(cc):
        t = cc.reshape(SUPER, gl, SSM_GROUP, SSM_STATE).transpose(0, 1, 3, 2)
        t = t[:, :, :, None, :] * eye[None, :, None, :, None]
        return t.reshape(SUPER, gl * SSM_STATE, gl * SSM_GROUP)

    bm = jnp.concatenate([in_blk(bb_re), in_blk(bb_im)], axis=1).astype(BF16)
    cm = jnp.concatenate([out_blk(c_re), out_blk(-c_im)], axis=2).astype(BF16)
    ar_row = a_re.reshape(1, SSM_GROUPS * SSM_STATE)
    ai_row = a_im.reshape(1, SSM_GROUPS * SSM_STATE)
    ar = jnp.broadcast_to(ar_row, (8, SSM_GROUPS * SSM_STATE))
    ai = jnp.concatenate([jnp.broadcast_to(-ai_row, (4, ai_row.shape[1])),
                          jnp.broadcast_to(ai_row, (4, ai_row.shape[1]))], axis=0)
    return bm, cm, ar, ai


def _pick(n, pref):
    return pref if n % pref == 0 else n


def _mixers(x, g_mix, w_in, b_forget, b_gate, w_out_a, lambda_re, lambda_im, log_step,
            ssm_b_re, ssm_b_im, ssm_c_re, ssm_c_im, ssm_d, w_glu, b_glu, w_out_b, w_out):
    bsz, seq, _ = x.shape
    assert bsz == 4, "the SSM kernel packs (re/im) x 4 batches onto the 8 sublanes"
    t = bsz * seq
    layer = 0
    x2 = x.reshape(t, D_MODEL)

    w = w_in[layer]
    fw = FOX_WIDTH
    wq, wk, wv = w[:, :fw], w[:, fw:2 * fw], w[:, 2 * fw:3 * fw]
    wf = w[:, 3 * fw:3 * fw + HEADS]
    wu = w[:, 3 * fw + HEADS:3 * fw + HEADS + SSM_WIDTH]
    wgt = w[:, 3 * fw + HEADS + SSM_WIDTH:]
    scale = LOG2E / math.sqrt(HEAD_DIM)
    w1 = jnp.concatenate(
        [wq * scale, wk, wv, wu, wgt, jnp.pad(wf, ((0, 0), (0, LANES - HEADS)))], axis=1).astype(BF16)
    bf_pad = jnp.pad(b_forget[layer], (0, LANES - HEADS)).reshape(1, LANES)
    selq, selk, auxc = _aux_constants()

    tm = _pick(seq, 512)
    qx, kx, vx, u, gates = _proj(x2, g_mix[layer].reshape(1, D_MODEL), w1, bf_pad,
                                 b_gate[layer].reshape(1, 2 * D_MODEL), selq, selk, auxc, tm, seq)

    hb = HEADS * HEAD_BLOCK
    tile = _pick(seq, ATTN_TILE)
    o = _attn(qx.reshape(bsz, seq, hb), kx.reshape(bsz, seq, hb), vx.reshape(bsz, seq, hb),
              tile, ATTN_HEADS).reshape(t, FOX_WIDTH)

    bm, cm, ar, ai = _ssm_params(lambda_re[layer], lambda_im[layer], log_step[layer],
                                 ssm_b_re[layer], ssm_b_im[layer], ssm_c_re[layer], ssm_c_im[layer])
    ub = u.reshape(bsz, seq, SSM_WIDTH).transpose(1, 0, 2)
    u8 = jnp.concatenate([ub, ub], axis=1).reshape(seq * 8, SSM_WIDTH)
    tb = _pick(seq, 128)
    y8 = _ssm(u8, bm, cm, ar, ai, ssm_d[layer].reshape(1, SSM_WIDTH), tb)
    ys = y8.reshape(seq, 8, SSM_WIDTH)[:, :bsz].transpose(1, 0, 2).reshape(t, SSM_WIDTH)

    x1 = _merge(x2, o, ys, gates, w_out_a[layer].astype(BF16), w_glu[layer].astype(BF16),
                b_glu[layer].reshape(1, SSM_WIDTH), w_out_b[layer].astype(BF16),
                w_out[layer].astype(BF16), tm)
    return x1, o, ys


def kernel(x, g_mix, w_in, b_forget, b_gate, w_out_a, lambda_re, lambda_im, log_step, ssm_b_re, ssm_b_im, ssm_c_re, ssm_c_im, ssm_d, w_glu, b_glu, w_out_b, w_out, g_ffn, w_router_group, b_router_group, w_router_expert, b_router_expert, w_exp_gate, w_exp_up, w_exp_down, g_final):
    bsz, seq, _ = x.shape
    t = bsz * seq
    layer = 0
    x1, _, _ = _mixers(x, g_mix, w_in, b_forget, b_gate, w_out_a, lambda_re, lambda_im, log_step,
                       ssm_b_re, ssm_b_im, ssm_c_re, ssm_c_im, ssm_d, w_glu, b_glu, w_out_b, w_out)

    wr_f = jnp.concatenate([w_router_group[layer], w_router_expert[layer]], axis=1)
    wr_f = jnp.pad(wr_f, ((0, 0), (0, ROUTER_COLS - N_GROUPS - N_EXPERTS)))
    wr_hi = wr_f.astype(BF16)
    wr_lo = (wr_f - wr_hi.astype(F32)).astype(BF16)
    wr = jnp.stack([wr_hi, wr_lo])
    br = jnp.pad(jnp.concatenate([b_router_group[layer], b_router_expert[layer]]),
                 (0, ROUTER_COLS - N_GROUPS - N_EXPERTS)).reshape(1, ROUTER_COLS)
    tmoe = _pick(t, 1024)
    out = _moe(x1, g_ffn[layer].reshape(1, D_MODEL), wr, br, w_exp_gate[layer].astype(BF16),
               w_exp_up[layer].astype(BF16), w_exp_down[layer].astype(BF16),
               g_final.reshape(1, D_MODEL), tmoe)
    return out.reshape(bsz, seq, D_MODEL)
```

```python
import functools
import math

import jax
import jax.numpy as jnp
from jax import lax
from jax.experimental import pallas as pl
from jax.experimental.pallas import tpu as pltpu

D_MODEL = 1024
HEADS = 8
HEAD_DIM = 64
FOX_WIDTH = HEADS * HEAD_DIM
SSM_WIDTH = 512
SSM_GROUP = 16
SSM_GROUPS = 32
SSM_STATE = 64
N_GROUPS = 4
EXPERTS_PER_GROUP = 4
N_EXPERTS = 16
EXPERT_FF = 256
EPS = 1e-6

LANES = 128
HEAD_BLOCK = LANES
AUX = HEAD_BLOCK - HEAD_DIM
V_ROWS = HEAD_DIM + 16
SUPER = 4
SG_CH = SSM_WIDTH // SUPER
SG_ST = SSM_GROUPS // SUPER * SSM_STATE
ROUTER_COLS = LANES

F32 = jnp.float32
BF16 = jnp.bfloat16
VMEM_LIMIT = 56 * 1024 * 1024
LOG2E = 1.4426950408889634
ATTN_HEADS = 4
ATTN_TILE = 512


def _cparams(sem):
    return pltpu.CompilerParams(dimension_semantics=sem, vmem_limit_bytes=VMEM_LIMIT)


def _rms(x, g):
    return x * lax.rsqrt(jnp.mean(x * x, axis=-1, keepdims=True) + EPS) * g


def _proj_kernel(x_ref, g_ref, w_ref, bf_ref, bg_ref, selq_ref, selk_ref, auxc_ref,
                 q_ref, k_ref, v_ref, u_ref, gate_ref, carry_scr, *, tiles_per_seq):
    tm = x_ref.shape[0]

    @pl.when(pl.program_id(0) % tiles_per_seq == 0)
    def _():
        carry_scr[...] = jnp.zeros_like(carry_scr)

    h = _rms(x_ref[...], g_ref[...]).astype(BF16)

    def mm(lo, hi):
        return jnp.dot(h, w_ref[:, lo:hi], preferred_element_type=F32)

    w = FOX_WIDTH
    f = mm(8 * w, 8 * w + LANES) + bf_ref[...]
    lane = lax.broadcasted_iota(jnp.int32, f.shape, 1)
    row = lax.broadcasted_iota(jnp.int32, f.shape, 0)
    c = jnp.where(lane < HEADS, (jnp.minimum(f, 0.0) - jnp.log(1.0 + jnp.exp(-jnp.abs(f)))) * LOG2E, 0.0)
    sh = 1
    while sh < tm:
        c = c + jnp.where(row >= sh, pltpu.roll(c, sh, axis=0), 0.0)
        sh *= 2
    c = c + carry_scr[...]
    carry_scr[...] = c[tm - 1:tm, :]
    c1 = c.astype(BF16)
    r = c - c1.astype(F32)
    c2 = r.astype(BF16)
    c3 = (r - c2.astype(F32)).astype(BF16)

    def place(sel_ref):
        return (jnp.dot(c1, sel_ref[0], preferred_element_type=F32)
                + jnp.dot(c2, sel_ref[1], preferred_element_type=F32)
                + jnp.dot(c3, sel_ref[2], preferred_element_type=F32))

    aux_q = place(selq_ref) + auxc_ref[0:1, :]
    aux_k = place(selk_ref) + auxc_ref[1:2, :]
    aux_v = jnp.broadcast_to(auxc_ref[2:3, :], aux_q.shape)

    for out_ref, lo, aux in ((q_ref, 0, aux_q), (k_ref, w, aux_k), (v_ref, 2 * w, aux_v)):
        val = mm(lo, lo + w)
        for hd in range(HEADS):
            blk = jnp.concatenate([val[:, hd * HEAD_DIM:(hd + 1) * HEAD_DIM],
                                   aux[:, hd * AUX:(hd + 1) * AUX]], axis=1)
            out_ref[:, hd * HEAD_BLOCK:(hd + 1) * HEAD_BLOCK] = blk.astype(BF16)

    u_ref[...] = mm(3 * w, 4 * w).astype(BF16)
    for cc in range(4):
        lo = 4 * w + cc * w
        gate_ref[:, cc * w:(cc + 1) * w] = jax.nn.sigmoid(
            mm(lo, lo + w) + bg_ref[:, cc * w:(cc + 1) * w]).astype(BF16)


def _proj(x2, g_mix, w1, bf_pad, b_gate, selq, selk, auxc, tm, seq):
    t = x2.shape[0]
    n1 = w1.shape[1]
    hb = HEADS * HEAD_BLOCK
    row = lambda i: (i, 0)
    const = lambda i: (0, 0)
    const3 = lambda i: (0, 0, 0)
    return pl.pallas_call(
        functools.partial(_proj_kernel, tiles_per_seq=seq // tm),
        grid=(t // tm,),
        in_specs=[
            pl.BlockSpec((tm, D_MODEL), row),
            pl.BlockSpec((1, D_MODEL), const),
            pl.BlockSpec((D_MODEL, n1), const),
            pl.BlockSpec((1, LANES), const),
            pl.BlockSpec((1, 2 * D_MODEL), const),
            pl.BlockSpec((3, LANES, HEADS * AUX), const3),
            pl.BlockSpec((3, LANES, HEADS * AUX), const3),
            pl.BlockSpec((8, HEADS * AUX), const),
        ],
        out_specs=[
            pl.BlockSpec((tm, hb), row),
            pl.BlockSpec((tm, hb), row),
            pl.BlockSpec((tm, hb), row),
            pl.BlockSpec((tm, SSM_WIDTH), row),
            pl.BlockSpec((tm, 2 * D_MODEL), row),
        ],
        out_shape=[
            jax.ShapeDtypeStruct((t, hb), BF16),
            jax.ShapeDtypeStruct((t, hb), BF16),
            jax.ShapeDtypeStruct((t, hb), BF16),
            jax.ShapeDtypeStruct((t, SSM_WIDTH), BF16),
            jax.ShapeDtypeStruct((t, 2 * D_MODEL), BF16),
        ],
        scratch_shapes=[pltpu.VMEM((1, LANES), F32)],
        compiler_params=_cparams(("arbitrary",)),
        name="proj",
    )(x2, g_mix, w1, bf_pad, b_gate, selq, selk, auxc)


def _aux_constants():
    selq = jnp.zeros((3, LANES, HEADS * AUX), F32)
    selk = jnp.zeros((3, LANES, HEADS * AUX), F32)
    auxc = jnp.zeros((8, HEADS * AUX), F32)
    hd = jnp.arange(HEADS)
    for p in range(3):
        selq = selq.at[p, hd, hd * AUX + p].set(1.0)
        selk = selk.at[p, hd, hd * AUX + 3 + p].set(-1.0)
        auxc = auxc.at[0, hd * AUX + 3 + p].set(1.0)
        auxc = auxc.at[1, hd * AUX + p].set(1.0)
    auxc = auxc.at[2, hd * AUX].set(1.0)
    return selq.astype(BF16), selk.astype(BF16), auxc


def _attn_kernel(q_ref, k_ref, v_ref, o_ref, vt_scr, s_scr, m_scr, acc_scr, *, tile, heads):
    i = pl.program_id(2)
    n = k_ref.shape[0] // tile
    nt = (((1,), (1,)), ((), ()))

    @pl.when(i == 0)
    def _():
        for g in range(heads):
            def tr(c, carry, g=g):
                r0 = pl.multiple_of(c * tile, tile)
                blk = v_ref[pl.ds(r0, tile), g * HEAD_BLOCK:(g + 1) * HEAD_BLOCK].astype(F32)
                vt_scr[g, c] = blk.T[:V_ROWS].astype(BF16)
                return carry
            lax.fori_loop(0, n, tr, 0)

    m_scr[...] = jnp.full(m_scr.shape, -jnp.inf, F32)
    acc_scr[...] = jnp.zeros(acc_scr.shape, F32)

    def qk(slot, t):
        r0 = pl.multiple_of(t * tile, tile)
        for g in range(heads):
            s_scr[slot, g] = lax.dot_general(
                k_ref[pl.ds(r0, tile), g * HEAD_BLOCK:(g + 1) * HEAD_BLOCK],
                q_ref[:, g * HEAD_BLOCK:(g + 1) * HEAD_BLOCK], nt, preferred_element_type=F32)

    def softmax(slot, masked):
        ps, alphas = [], []
        for g in range(heads):
            s = s_scr[slot, g]
            if masked:
                kpos = lax.broadcasted_iota(jnp.int32, s.shape, 0)
                qpos = lax.broadcasted_iota(jnp.int32, s.shape, 1)
                s = jnp.where(kpos <= qpos, s, -jnp.inf)
            m = m_scr[g]
            m_new = jnp.maximum(m, jnp.max(s, axis=0, keepdims=True))
            ps.append(jnp.exp2(s - m_new).astype(BF16))
            alphas.append(jnp.exp2(m - m_new))
            m_scr[g] = m_new
        return ps, alphas

    def pv(pa, t):
        ps, alphas = pa
        for g in range(heads):
            acc_scr[g] = (alphas[g] * acc_scr[g]
                          + jnp.dot(vt_scr[g, t], ps[g], preferred_element_type=F32))

    qk(0, 0)

    def body(tt, carry):
        t0 = 2 * tt
        qk(1, t0 + 1)
        pa0 = softmax(0, False)
        qk(0, t0 + 2)
        pv(pa0, t0)
        pv(softmax(1, False), t0 + 1)
        return carry

    lax.fori_loop(0, lax.shift_right_logical(i, 1), body, 0)

    @pl.when((i & 1) == 0)
    def _():
        pv(softmax(0, True), i)

    @pl.when((i & 1) == 1)
    def _():
        qk(1, i)
        pv(softmax(0, False), i - 1)
        pv(softmax(1, True), i)

    for pair in range(heads // 2):
        a0 = acc_scr[2 * pair]
        a1 = acc_scr[2 * pair + 1]
        o2 = jnp.concatenate([a0[:HEAD_DIM] / a0[HEAD_DIM:HEAD_DIM + 1],
                              a1[:HEAD_DIM] / a1[HEAD_DIM:HEAD_DIM + 1]], axis=0)
        o_ref[:, pair * LANES:(pair + 1) * LANES] = o2.T.astype(BF16)


def _attn(qx, kx, vx, tile, heads):
    b, s, _ = qx.shape
    n = s // tile
    hb = heads * HEAD_BLOCK
    return pl.pallas_call(
        functools.partial(_attn_kernel, tile=tile, heads=heads),
        grid=(b, HEADS // heads, n),
        in_specs=[
            pl.BlockSpec((None, tile, hb), lambda bi, hi, i: (bi, i, hi)),
            pl.BlockSpec((None, s, hb), lambda bi, hi, i: (bi, 0, hi)),
            pl.BlockSpec((None, s, hb), lambda bi, hi, i: (bi, 0, hi)),
        ],
        out_specs=pl.BlockSpec((None, tile, heads * HEAD_DIM), lambda bi, hi, i: (bi, i, hi)),
        out_shape=jax.ShapeDtypeStruct((b, s, FOX_WIDTH), BF16),
        scratch_shapes=[
            pltpu.VMEM((heads, n, V_ROWS, tile), BF16),
            pltpu.VMEM((2, heads, tile, tile), F32),
            pltpu.VMEM((heads, 1, tile), F32),
            pltpu.VMEM((heads, V_ROWS, tile), F32),
        ],
        compiler_params=_cparams(("arbitrary", "arbitrary", "arbitrary")),
        name="attn",
    )(qx, kx, vx)


def _ssm_kernel(u_ref, bm_ref, cm_ref, ar_ref, ai_ref, d_ref, y_ref, x_scr, s_scr, *, tb):
    rows = tb * 8

    @pl.when(pl.program_id(0) == 0)
    def _():
        s_scr[...] = jnp.zeros_like(s_scr)

    u = u_ref[...]
    is_re = (lax.broadcasted_iota(jnp.int32, (rows, SG_CH), 0) % 8) < 4
    zero = jnp.zeros((rows, SG_CH), BF16)
    for g in range(SUPER):
        ug = u[:, g * SG_CH:(g + 1) * SG_CH]
        lhs = jnp.concatenate([jnp.where(is_re, ug, zero), jnp.where(is_re, zero, ug)], axis=1)
        x_scr[:, g * SG_ST:(g + 1) * SG_ST] = jnp.dot(lhs, bm_ref[g], preferred_element_type=F32)

    half = 2 * SG_ST
    for c in range(SUPER * SG_ST // half):
        lo = c * half
        ar = ar_ref[:, lo:lo + half]
        ai = ai_ref[:, lo:lo + half]

        def body(t, s, lo=lo, ar=ar, ai=ai):
            r0 = pl.multiple_of(t * 8, 8)
            s = ar * s + ai * pltpu.roll(s, 4, axis=0) + x_scr[pl.ds(r0, 8), lo:lo + half]
            x_scr[pl.ds(r0, 8), lo:lo + half] = s
            return s

        s_scr[:, lo:lo + half] = lax.fori_loop(0, tb, body, s_scr[:, lo:lo + half], unroll=4)

    for g in range(SUPER):
        st = x_scr[:, g * SG_ST:(g + 1) * SG_ST].astype(BF16)
        o = jnp.dot(st, cm_ref[g], preferred_element_type=F32)
        y = o[:, :SG_CH] + pltpu.roll(o[:, SG_CH:], rows - 4, axis=0)
        y = y + d_ref[:, g * SG_CH:(g + 1) * SG_CH] * u[:, g * SG_CH:(g + 1) * SG_CH].astype(F32)
        y_ref[:, g * SG_CH:(g + 1) * SG_CH] = y.astype(BF16)


def _ssm(u8, bm, cm, ar, ai, d, tb):
    rows_total = u8.shape[0]
    rows = tb * 8
    return pl.pallas_call(
        functools.partial(_ssm_kernel, tb=tb),
        grid=(rows_total // rows,),
        in_specs=[
            pl.BlockSpec((rows, SSM_WIDTH), lambda i: (i, 0)),
            pl.BlockSpec((SUPER, 2 * SG_CH, SG_ST), lambda i: (0, 0, 0)),
            pl.BlockSpec((SUPER, SG_ST, 2 * SG_CH), lambda i: (0, 0, 0)),
            pl.BlockSpec((8, SUPER * SG_ST), lambda i: (0, 0)),
            pl.BlockSpec((8, SUPER * SG_ST), lambda i: (0, 0)),
            pl.BlockSpec((1, SSM_WIDTH), lambda i: (0, 0)),
        ],
        out_specs=pl.BlockSpec((rows, SSM_WIDTH), lambda i: (i, 0)),
        out_shape=jax.ShapeDtypeStruct((rows_total, SSM_WIDTH), BF16),
        scratch_shapes=[
            pltpu.VMEM((rows, SUPER * SG_ST), F32),
            pltpu.VMEM((8, SUPER * SG_ST), F32),
        ],
        compiler_params=_cparams(("arbitrary",)),
        name="ssm",
    )(u8, bm, cm, ar, ai, d)


def _merge_kernel(x_ref, o_ref, ys_ref, gate_ref, woa_ref, wglu_ref, bglu_ref, wob_ref, wout_ref,
                  x1_ref):
    y_a = jnp.dot(o_ref[...], woa_ref[...], preferred_element_type=F32)
    z = jax.nn.gelu(ys_ref[...].astype(F32))
    zg = jnp.dot(z.astype(BF16), wglu_ref[...], preferred_element_type=F32) + bglu_ref[...]
    z = z * jax.nn.sigmoid(zg)
    y_b = jnp.dot(z.astype(BF16), wob_ref[...], preferred_element_type=F32)
    g0 = gate_ref[:, :D_MODEL].astype(F32)
    g1 = gate_ref[:, D_MODEL:].astype(F32)
    merged = (g0 * y_a + g1 * y_b).astype(BF16)
    x1_ref[...] = x_ref[...] + jnp.dot(merged, wout_ref[...], preferred_element_type=F32)


def _merge(x2, o, ys, gates, woa, wglu, bglu, wob, wout, tm):
    t = x2.shape[0]
    row = lambda i: (i, 0)
    const = lambda i: (0, 0)
    return pl.pallas_call(
        _merge_kernel,
        grid=(t // tm,),
        in_specs=[
            pl.BlockSpec((tm, D_MODEL), row),
            pl.BlockSpec((tm, FOX_WIDTH), row),
            pl.BlockSpec((tm, SSM_WIDTH), row),
            pl.BlockSpec((tm, 2 * D_MODEL), row),
            pl.BlockSpec((FOX_WIDTH, D_MODEL), const),
            pl.BlockSpec((SSM_WIDTH, SSM_WIDTH), const),
            pl.BlockSpec((1, SSM_WIDTH), const),
            pl.BlockSpec((SSM_WIDTH, D_MODEL), const),
            pl.BlockSpec((D_MODEL, D_MODEL), const),
        ],
        out_specs=pl.BlockSpec((tm, D_MODEL), row),
        out_shape=jax.ShapeDtypeStruct((t, D_MODEL), F32),
        compiler_params=_cparams(("arbitrary",)),
        name="merge",
    )(x2, o, ys, gates, woa, wglu, bglu, wob, wout)


def _combine_weights(logits):
    lane = lax.broadcasted_iota(jnp.int32, logits.shape, 1)
    neg = -jnp.inf
    big = jnp.int32(ROUTER_COLS)
    is_g = lane < N_GROUPS
    lg = jnp.where(is_g, logits, neg)
    gmax = jnp.max(lg, axis=1, keepdims=True)
    p_sel = 1.0 / jnp.sum(jnp.exp(lg - gmax), axis=1, keepdims=True)
    g_idx = jnp.min(jnp.where(lg == gmax, lane, big), axis=1, keepdims=True)
    lo = N_GROUPS + g_idx * EXPERTS_PER_GROUP
    in_grp = (lane >= lo) & (lane < lo + EXPERTS_PER_GROUP)
    le = jnp.where(in_grp, logits, neg)
    v1 = jnp.max(le, axis=1, keepdims=True)
    i1 = jnp.min(jnp.where(le == v1, lane, big), axis=1, keepdims=True)
    le2 = jnp.where(lane == i1, neg, le)
    v2 = jnp.max(le2, axis=1, keepdims=True)
    i2 = jnp.min(jnp.where(le2 == v2, lane, big), axis=1, keepdims=True)
    e2 = jnp.exp(v2 - v1)
    w1 = p_sel / (1.0 + e2)
    w2 = p_sel * e2 / (1.0 + e2)
    return jnp.where(lane == i1, w1, 0.0) + jnp.where(lane == i2, w2, 0.0)


def _moe_kernel(x1_ref, gf_ref, wr_ref, br_ref, wg_ref, wu_ref, wd_ref, gfin_ref, out_ref,
                h_scr, cw_scr, acc_scr):
    e = pl.program_id(1)

    @pl.when(e == 0)
    def _():
        h = _rms(x1_ref[...], gf_ref[...])
        h_hi = h.astype(BF16)
        h_lo = (h - h_hi.astype(F32)).astype(BF16)
        logits = (jnp.dot(h_hi, wr_ref[0], preferred_element_type=F32)
                  + jnp.dot(h_lo, wr_ref[0], preferred_element_type=F32)
                  + jnp.dot(h_hi, wr_ref[1], preferred_element_type=F32)) + br_ref[...]
        h_scr[...] = h_hi
        cw_scr[...] = _combine_weights(logits)
        acc_scr[...] = jnp.zeros_like(acc_scr)

    h = h_scr[...]
    hg = jnp.dot(h, wg_ref[...], preferred_element_type=F32)
    hu = jnp.dot(h, wu_ref[...], preferred_element_type=F32)
    lane = lax.broadcasted_iota(jnp.int32, cw_scr.shape, 1)
    cw = jnp.sum(jnp.where(lane == N_GROUPS + e, cw_scr[...], 0.0), axis=1, keepdims=True)
    act = (hg * jax.nn.sigmoid(hg) * hu * cw).astype(BF16)
    acc_scr[...] += jnp.dot(act, wd_ref[...], preferred_element_type=F32)

    @pl.when(e == N_EXPERTS - 1)
    def _():
        out_ref[...] = _rms(x1_ref[...] + acc_scr[...], gfin_ref[...])


def _moe(x1, g_ffn, wr, br, wg, wu, wd, g_final, tm):
    t = x1.shape[0]
    return pl.pallas_call(
        _moe_kernel,
        grid=(t // tm, N_EXPERTS),
        in_specs=[
            pl.BlockSpec((tm, D_MODEL), lambda i, e: (i, 0)),
            pl.BlockSpec((1, D_MODEL), lambda i, e: (0, 0)),
            pl.BlockSpec((2, D_MODEL, ROUTER_COLS), lambda i, e: (0, 0, 0)),
            pl.BlockSpec((1, ROUTER_COLS), lambda i, e: (0, 0)),
            pl.BlockSpec((None, D_MODEL, EXPERT_FF), lambda i, e: (e, 0, 0)),
            pl.BlockSpec((None, D_MODEL, EXPERT_FF), lambda i, e: (e, 0, 0)),
            pl.BlockSpec((None, EXPERT_FF, D_MODEL), lambda i, e: (e, 0, 0)),
            pl.BlockSpec((1, D_MODEL), lambda i, e: (0, 0)),
        ],
        out_specs=pl.BlockSpec((tm, D_MODEL), lambda i, e: (i, 0)),
        out_shape=jax.ShapeDtypeStruct((t, D_MODEL), F32),
        scratch_shapes=[
            pltpu.VMEM((tm, D_MODEL), BF16),
            pltpu.VMEM((tm, ROUTER_COLS), F32),
            pltpu.VMEM((tm, D_MODEL), F32),
        ],
        compiler_params=_cparams(("arbitrary", "arbitrary")),
        name="moe",
    )(x1, g_ffn, wr, br, wg, wu, wd, g_final)


def _ssm_params(lambda_re, lambda_im, log_step, b_re, b_im, c_re, c_im):
    dt = jnp.exp(log_step)[:, None]
    mag = jnp.exp(lambda_re * dt)
    a_re = mag * jnp.cos(lambda_im * dt)
    a_im = mag * jnp.sin(lambda_im * dt)
    den = lambda_re * lambda_re + lambda_im * lambda_im
    num_re = a_re - 1.0
    z_re = (num_re * lambda_re + a_im * lambda_im) / den
    z_im = (a_im * lambda_re - num_re * lambda_im) / den
    bb_re = z_re[..., None] * b_re - z_im[..., None] * b_im
    bb_im = z_re[..., None] * b_im + z_im[..., None] * b_re
    gl = SSM_GROUPS // SUPER
    eye = jnp.eye(gl, dtype=F32)

    def in_blk(bb):
        t = bb.reshape(SUPER, gl, SSM_STATE, SSM_GROUP).transpose(0, 1, 3, 2)
        t = t[:, :, :, None, :] * eye[None, :, None, :, None]
        return t.reshape(SUPER, gl * SSM_GROUP, gl * SSM_STATE)

    def out_blk(cc):
        t = cc.reshape(SUPER, gl, SSM_GROUP, SSM_STATE).transpose(0, 1, 3, 2)
        t = t[:, :, :, None, :] * eye[None, :, None, :, None]
        return t.reshape(SUPER, gl * SSM_STATE, gl * SSM_GROUP)

    bm = jnp.concatenate([in_blk(bb_re), in_blk(bb_im)], axis=1).astype(BF16)
    cm = jnp.concatenate([out_blk(c_re), out_blk(-c_im)], axis=2).astype(BF16)
    ar_row = a_re.reshape(1, SSM_GROUPS * SSM_STATE)
    ai_row = a_im.reshape(1, SSM_GROUPS * SSM_STATE)
    ar = jnp.broadcast_to(ar_row, (8, SSM_GROUPS * SSM_STATE))
    ai = jnp.concatenate([jnp.broadcast_to(-ai_row, (4, ai_row.shape[1])),
                          jnp.broadcast_to(ai_row, (4, ai_row.shape[1]))], axis=0)
    return bm, cm, ar, ai


def _pick(n, pref):
    return pref if n % pref == 0 else n


def _mixers(x, g_mix, w_in, b_forget, b_gate, w_out_a, lambda_re, lambda_im, log_step,
            ssm_b_re, ssm_b_im, ssm_c_re, ssm_c_im, ssm_d, w_glu, b_glu, w_out_b, w_out):
    bsz, seq, _ = x.shape
    assert bsz == 4, "the SSM kernel packs (re/im) x 4 batches onto the 8 sublanes"
    t = bsz * seq
    layer = 0
    x2 = x.reshape(t, D_MODEL)

    w = w_in[layer]
    fw = FOX_WIDTH
    wq, wk, wv = w[:, :fw], w[:, fw:2 * fw], w[:, 2 * fw:3 * fw]
    wf = w[:, 3 * fw:3 * fw + HEADS]
    wu = w[:, 3 * fw + HEADS:3 * fw + HEADS + SSM_WIDTH]
    wgt = w[:, 3 * fw + HEADS + SSM_WIDTH:]
    scale = LOG2E / math.sqrt(HEAD_DIM)
    w1 = jnp.concatenate(
        [wq * scale, wk, wv, wu, wgt, jnp.pad(wf, ((0, 0), (0, LANES - HEADS)))], axis=1).astype(BF16)
    bf_pad = jnp.pad(b_forget[layer], (0, LANES - HEADS)).reshape(1, LANES)
    selq, selk, auxc = _aux_constants()

    tm = _pick(seq, 512)
    qx, kx, vx, u, gates = _proj(x2, g_mix[layer].reshape(1, D_MODEL), w1, bf_pad,
                                 b_gate[layer].reshape(1, 2 * D_MODEL), selq, selk, auxc, tm, seq)

    hb = HEADS * HEAD_BLOCK
    tile = _pick(seq, ATTN_TILE)
    o = _attn(qx.reshape(bsz, seq, hb), kx.reshape(bsz, seq, hb), vx.reshape(bsz, seq, hb),
              tile, ATTN_HEADS).reshape(t, FOX_WIDTH)

    bm, cm, ar, ai = _ssm_params(lambda_re[layer], lambda_im[layer], log_step[layer],
                                 ssm_b_re[layer], ssm_b_im[layer], ssm_c_re[layer], ssm_c_im[layer])
    ub = u.reshape(bsz, seq, SSM_WIDTH).transpose(1, 0, 2)
    u8 = jnp.concatenate([ub, ub], axis=1).reshape(seq * 8, SSM_WIDTH)
    tb = _pick(seq, 128)
    y8 = _ssm(u8, bm, cm, ar, ai, ssm_d[layer].reshape(1, SSM_WIDTH), tb)
    ys = y8.reshape(seq, 8, SSM_WIDTH)[:, :bsz].transpose(1, 0, 2).reshape(t, SSM_WIDTH)

    x1 = _merge(x2, o, ys, gates, w_out_a[layer].astype(BF16), w_glu[layer].astype(BF16),
                b_glu[layer].reshape(1, SSM_WIDTH), w_out_b[layer].astype(BF16),
                w_out[layer].astype(BF16), tm)
    return x1, o, ys


def kernel(x, g_mix, w_in, b_forget, b_gate, w_out_a, lambda_re, lambda_im, log_step, ssm_b_re, ssm_b_im, ssm_c_re, ssm_c_im, ssm_d, w_glu, b_glu, w_out_b, w_out, g_ffn, w_router_group, b_router_group, w_router_expert, b_router_expert, w_exp_gate, w_exp_up, w_exp_down, g_final):
    bsz, seq, _ = x.shape
    t = bsz * seq
    layer = 0
    x1, _, _ = _mixers(x, g_mix, w_in, b_forget, b_gate, w_out_a, lambda_re, lambda_im, log_step,
                       ssm_b_re, ssm_b_im, ssm_c_re, ssm_c_im, ssm_d, w_glu, b_glu, w_out_b, w_out)

    wr_f = jnp.concatenate([w_router_group[layer], w_router_expert[layer]], axis=1)
    wr_f = jnp.pad(wr_f, ((0, 0), (0, ROUTER_COLS - N_GROUPS - N_EXPERTS)))
    wr_hi = wr_f.astype(BF16)
    wr_lo = (wr_f - wr_hi.astype(F32)).astype(BF16)
    wr = jnp.stack([wr_hi, wr_lo])
    br = jnp.pad(jnp.concatenate([b_router_group[layer], b_router_expert[layer]]),
                 (0, ROUTER_COLS - N_GROUPS - N_EXPERTS)).reshape(1, ROUTER_COLS)
    tmoe = _pick(t, 1024)
    out = _moe(x1, g_ffn[layer].reshape(1, D_MODEL), wr, br, w_exp_gate[layer].astype(BF16),
               w_exp_up[layer].astype(BF16), w_exp_down[layer].astype(BF16),
               g_final.reshape(1, D_MODEL), tmoe)
    return out.reshape(bsz, seq, D_MODEL)
```

```python
import functools
import math

import jax
import jax.numpy as jnp
from jax import lax
from jax.experimental import pallas as pl
from jax.experimental.pallas import tpu as pltpu

D_MODEL = 1024
HEADS = 8
HEAD_DIM = 64
FOX_WIDTH = HEADS * HEAD_DIM
SSM_WIDTH = 512
SSM_GROUP = 16
SSM_GROUPS = 32
SSM_STATE = 64
N_GROUPS = 4
EXPERTS_PER_GROUP = 4
N_EXPERTS = 16
EXPERT_FF = 256
EPS = 1e-6

LANES = 128
HEAD_BLOCK = LANES
AUX = HEAD_BLOCK - HEAD_DIM
V_ROWS = HEAD_DIM + 16
SUPER = 4
SG_CH = SSM_WIDTH // SUPER
SG_ST = SSM_GROUPS // SUPER * SSM_STATE
ROUTER_COLS = LANES

F32 = jnp.float32
BF16 = jnp.bfloat16
VMEM_LIMIT = 56 * 1024 * 1024
LOG2E = 1.4426950408889634
ATTN_HEADS = 4
ATTN_TILE = 512
MOE_BLOCK = 1024
MOE_CHUNK = 128


def _cparams(sem):
    return pltpu.CompilerParams(dimension_semantics=sem, vmem_limit_bytes=VMEM_LIMIT)


def _rms(x, g):
    return x * lax.rsqrt(jnp.mean(x * x, axis=-1, keepdims=True) + EPS) * g


def _proj_kernel(x_ref, g_ref, w_ref, bf_ref, bg_ref, selq_ref, selk_ref, auxc_ref,
                 q_ref, k_ref, v_ref, u_ref, gate_ref, carry_scr, *, tiles_per_seq):
    tm = x_ref.shape[0]

    @pl.when(pl.program_id(0) % tiles_per_seq == 0)
    def _():
        carry_scr[...] = jnp.zeros_like(carry_scr)

    h = _rms(x_ref[...], g_ref[...]).astype(BF16)

    def mm(lo, hi):
        return jnp.dot(h, w_ref[:, lo:hi], preferred_element_type=F32)

    w = FOX_WIDTH
    f = mm(8 * w, 8 * w + LANES) + bf_ref[...]
    lane = lax.broadcasted_iota(jnp.int32, f.shape, 1)
    row = lax.broadcasted_iota(jnp.int32, f.shape, 0)
    c = jnp.where(lane < HEADS, (jnp.minimum(f, 0.0) - jnp.log(1.0 + jnp.exp(-jnp.abs(f)))) * LOG2E, 0.0)
    sh = 1
    while sh < tm:
        c = c + jnp.where(row >= sh, pltpu.roll(c, sh, axis=0), 0.0)
        sh *= 2
    c = c + carry_scr[...]
    carry_scr[...] = c[tm - 1:tm, :]
    c1 = c.astype(BF16)
    r = c - c1.astype(F32)
    c2 = r.astype(BF16)
    c3 = (r - c2.astype(F32)).astype(BF16)

    def place(sel_ref):
        return (jnp.dot(c1, sel_ref[0], preferred_element_type=F32)
                + jnp.dot(c2, sel_ref[1], preferred_element_type=F32)
                + jnp.dot(c3, sel_ref[2], preferred_element_type=F32))

    aux_q = place(selq_ref) + auxc_ref[0:1, :]
    aux_k = place(selk_ref) + auxc_ref[1:2, :]
    aux_v = jnp.broadcast_to(auxc_ref[2:3, :], aux_q.shape)

    for out_ref, lo, aux in ((q_ref, 0, aux_q), (k_ref, w, aux_k), (v_ref, 2 * w, aux_v)):
        val = mm(lo, lo + w)
        for hd in range(HEADS):
            blk = jnp.concatenate([val[:, hd * HEAD_DIM:(hd + 1) * HEAD_DIM],
                                   aux[:, hd * AUX:(hd + 1) * AUX]], axis=1)
            out_ref[:, hd * HEAD_BLOCK:(hd + 1) * HEAD_BLOCK] = blk.astype(BF16)

    u_ref[...] = mm(3 * w, 4 * w).astype(BF16)
    for cc in range(4):
        lo = 4 * w + cc * w
        gate_ref[:, cc * w:(cc + 1) * w] = jax.nn.sigmoid(
            mm(lo, lo + w) + bg_ref[:, cc * w:(cc + 1) * w]).astype(BF16)


def _proj(x2, g_mix, w1, bf_pad, b_gate, selq, selk, auxc, tm, seq):
    t = x2.shape[0]
    n1 = w1.shape[1]
    hb = HEADS * HEAD_BLOCK
    row = lambda i: (i, 0)
    const = lambda i: (0, 0)
    const3 = lambda i: (0, 0, 0)
    return pl.pallas_call(
        functools.partial(_proj_kernel, tiles_per_seq=seq // tm),
        grid=(t // tm,),
        in_specs=[
            pl.BlockSpec((tm, D_MODEL), row),
            pl.BlockSpec((1, D_MODEL), const),
            pl.BlockSpec((D_MODEL, n1), const),
            pl.BlockSpec((1, LANES), const),
            pl.BlockSpec((1, 2 * D_MODEL), const),
            pl.BlockSpec((3, LANES, HEADS * AUX), const3),
            pl.BlockSpec((3, LANES, HEADS * AUX), const3),
            pl.BlockSpec((8, HEADS * AUX), const),
        ],
        out_specs=[
            pl.BlockSpec((tm, hb), row),
            pl.BlockSpec((tm, hb), row),
            pl.BlockSpec((tm, hb), row),
            pl.BlockSpec((tm, SSM_WIDTH), row),
            pl.BlockSpec((tm, 2 * D_MODEL), row),
        ],
        out_shape=[
            jax.ShapeDtypeStruct((t, hb), BF16),
            jax.ShapeDtypeStruct((t, hb), BF16),
            jax.ShapeDtypeStruct((t, hb), BF16),
            jax.ShapeDtypeStruct((t, SSM_WIDTH), BF16),
            jax.ShapeDtypeStruct((t, 2 * D_MODEL), BF16),
        ],
        scratch_shapes=[pltpu.VMEM((1, LANES), F32)],
        compiler_params=_cparams(("arbitrary",)),
        name="proj",
    )(x2, g_mix, w1, bf_pad, b_gate, selq, selk, auxc)


def _aux_constants():
    selq = jnp.zeros((3, LANES, HEADS * AUX), F32)
    selk = jnp.zeros((3, LANES, HEADS * AUX), F32)
    auxc = jnp.zeros((8, HEADS * AUX), F32)
    hd = jnp.arange(HEADS)
    for p in range(3):
        selq = selq.at[p, hd, hd * AUX + p].set(1.0)
        selk = selk.at[p, hd, hd * AUX + 3 + p].set(-1.0)
        auxc = auxc.at[0, hd * AUX + 3 + p].set(1.0)
        auxc = auxc.at[1, hd * AUX + p].set(1.0)
    auxc = auxc.at[2, hd * AUX].set(1.0)
    return selq.astype(BF16), selk.astype(BF16), auxc


def _attn_kernel(q_ref, k_ref, v_ref, o_ref, vt_scr, s_scr, m_scr, acc_scr, *, tile, heads):
    i = pl.program_id(2)
    n = k_ref.shape[0] // tile
    nt = (((1,), (1,)), ((), ()))

    @pl.when(i == 0)
    def _():
        for g in range(heads):
            def tr(c, carry, g=g):
                r0 = pl.multiple_of(c * tile, tile)
                blk = v_ref[pl.ds(r0, tile), g * HEAD_BLOCK:(g + 1) * HEAD_BLOCK].astype(F32)
                vt_scr[g, c] = blk.T[:V_ROWS].astype(BF16)
                return carry
            lax.fori_loop(0, n, tr, 0)

    m_scr[...] = jnp.full(m_scr.shape, -jnp.inf, F32)
    acc_scr[...] = jnp.zeros(acc_scr.shape, F32)

    def qk(slot, t):
        r0 = pl.multiple_of(t * tile, tile)
        for g in range(heads):
            s_scr[slot, g] = lax.dot_general(
                k_ref[pl.ds(r0, tile), g * HEAD_BLOCK:(g + 1) * HEAD_BLOCK],
                q_ref[:, g * HEAD_BLOCK:(g + 1) * HEAD_BLOCK], nt, preferred_element_type=F32)

    def softmax(slot, masked):
        ps, alphas = [], []
        for g in range(heads):
            s = s_scr[slot, g]
            if masked:
                kpos = lax.broadcasted_iota(jnp.int32, s.shape, 0)
                qpos = lax.broadcasted_iota(jnp.int32, s.shape, 1)
                s = jnp.where(kpos <= qpos, s, -jnp.inf)
            m = m_scr[g]
            m_new = jnp.maximum(m, jnp.max(s, axis=0, keepdims=True))
            ps.append(jnp.exp2(s - m_new).astype(BF16))
            alphas.append(jnp.exp2(m - m_new))
            m_scr[g] = m_new
        return ps, alphas

    def pv(pa, t):
        ps, alphas = pa
        for g in range(heads):
            acc_scr[g] = (alphas[g] * acc_scr[g]
                          + jnp.dot(vt_scr[g, t], ps[g], preferred_element_type=F32))

    qk(0, 0)

    def body(tt, carry):
        t0 = 2 * tt
        qk(1, t0 + 1)
        pa0 = softmax(0, False)
        qk(0, t0 + 2)
        pv(pa0, t0)
        pv(softmax(1, False), t0 + 1)
        return carry

    lax.fori_loop(0, lax.shift_right_logical(i, 1), body, 0)

    @pl.when((i & 1) == 0)
    def _():
        pv(softmax(0, True), i)

    @pl.when((i & 1) == 1)
    def _():
        qk(1, i)
        pv(softmax(0, False), i - 1)
        pv(softmax(1, True), i)

    for pair in range(heads // 2):
        a0 = acc_scr[2 * pair]
        a1 = acc_scr[2 * pair + 1]
        o2 = jnp.concatenate([a0[:HEAD_DIM] / a0[HEAD_DIM:HEAD_DIM + 1],
                              a1[:HEAD_DIM] / a1[HEAD_DIM:HEAD_DIM + 1]], axis=0)
        o_ref[:, pair * LANES:(pair + 1) * LANES] = o2.T.astype(BF16)


def _attn(qx, kx, vx, tile, heads):
    b, s, _ = qx.shape
    n = s // tile
    hb = heads * HEAD_BLOCK
    return pl.pallas_call(
        functools.partial(_attn_kernel, tile=tile, heads=heads),
        grid=(b, HEADS // heads, n),
        in_specs=[
            pl.BlockSpec((None, tile, hb), lambda bi, hi, i: (bi, i, hi)),
            pl.BlockSpec((None, s, hb), lambda bi, hi, i: (bi, 0, hi)),
            pl.BlockSpec((None, s, hb), lambda bi, hi, i: (bi, 0, hi)),
        ],
        out_specs=pl.BlockSpec((None, tile, heads * HEAD_DIM), lambda bi, hi, i: (bi, i, hi)),
        out_shape=jax.ShapeDtypeStruct((b, s, FOX_WIDTH), BF16),
        scratch_shapes=[
            pltpu.VMEM((heads, n, V_ROWS, tile), BF16),
            pltpu.VMEM((2, heads, tile, tile), F32),
            pltpu.VMEM((heads, 1, tile), F32),
            pltpu.VMEM((heads, V_ROWS, tile), F32),
        ],
        compiler_params=_cparams(("arbitrary", "arbitrary", "arbitrary")),
        name="attn",
    )(qx, kx, vx)


def _ssm_kernel(u_ref, bm_ref, cm_ref, ar_ref, ai_ref, d_ref, y_ref, x_scr, s_scr, *, tb):
    rows = tb * 8

    @pl.when(pl.program_id(0) == 0)
    def _():
        s_scr[...] = jnp.zeros_like(s_scr)

    u = u_ref[...]
    is_re = (lax.broadcasted_iota(jnp.int32, (rows, SG_CH), 0) % 8) < 4
    zero = jnp.zeros((rows, SG_CH), BF16)
    for g in range(SUPER):
        ug = u[:, g * SG_CH:(g + 1) * SG_CH]
        lhs = jnp.concatenate([jnp.where(is_re, ug, zero), jnp.where(is_re, zero, ug)], axis=1)
        x_scr[:, g * SG_ST:(g + 1) * SG_ST] = jnp.dot(lhs, bm_ref[g], preferred_element_type=F32)

    half = 2 * SG_ST
    for c in range(SUPER * SG_ST // half):
        lo = c * half
        ar = ar_ref[:, lo:lo + half]
        ai = ai_ref[:, lo:lo + half]

        def body(t, s, lo=lo, ar=ar, ai=ai):
            r0 = pl.multiple_of(t * 8, 8)
            s = ar * s + ai * pltpu.roll(s, 4, axis=0) + x_scr[pl.ds(r0, 8), lo:lo + half]
            x_scr[pl.ds(r0, 8), lo:lo + half] = s
            return s

        s_scr[:, lo:lo + half] = lax.fori_loop(0, tb, body, s_scr[:, lo:lo + half], unroll=4)

    for g in range(SUPER):
        st = x_scr[:, g * SG_ST:(g + 1) * SG_ST].astype(BF16)
        o = jnp.dot(st, cm_ref[g], preferred_element_type=F32)
        y = o[:, :SG_CH] + pltpu.roll(o[:, SG_CH:], rows - 4, axis=0)
        y = y + d_ref[:, g * SG_CH:(g + 1) * SG_CH] * u[:, g * SG_CH:(g + 1) * SG_CH].astype(F32)
        y_ref[:, g * SG_CH:(g + 1) * SG_CH] = y.astype(BF16)


def _ssm(u8, bm, cm, ar, ai, d, tb):
    rows_total = u8.shape[0]
    rows = tb * 8
    return pl.pallas_call(
        functools.partial(_ssm_kernel, tb=tb),
        grid=(rows_total // rows,),
        in_specs=[
            pl.BlockSpec((rows, SSM_WIDTH), lambda i: (i, 0)),
            pl.BlockSpec((SUPER, 2 * SG_CH, SG_ST), lambda i: (0, 0, 0)),
            pl.BlockSpec((SUPER, SG_ST, 2 * SG_CH), lambda i: (0, 0, 0)),
            pl.BlockSpec((8, SUPER * SG_ST), lambda i: (0, 0)),
            pl.BlockSpec((8, SUPER * SG_ST), lambda i: (0, 0)),
            pl.BlockSpec((1, SSM_WIDTH), lambda i: (0, 0)),
        ],
        out_specs=pl.BlockSpec((rows, SSM_WIDTH), lambda i: (i, 0)),
        out_shape=jax.ShapeDtypeStruct((rows_total, SSM_WIDTH), BF16),
        scratch_shapes=[
            pltpu.VMEM((rows, SUPER * SG_ST), F32),
            pltpu.VMEM((8, SUPER * SG_ST), F32),
        ],
        compiler_params=_cparams(("arbitrary",)),
        name="ssm",
    )(u8, bm, cm, ar, ai, d)


def _merge_kernel(x_ref, o_ref, ys_ref, gate_ref, woa_ref, wglu_ref, bglu_ref, wob_ref, wout_ref,
                  x1_ref):
    y_a = jnp.dot(o_ref[...], woa_ref[...], preferred_element_type=F32)
    z = jax.nn.gelu(ys_ref[...].astype(F32))
    zg = jnp.dot(z.astype(BF16), wglu_ref[...], preferred_element_type=F32) + bglu_ref[...]
    z = z * jax.nn.sigmoid(zg)
    y_b = jnp.dot(z.astype(BF16), wob_ref[...], preferred_element_type=F32)
    g0 = gate_ref[:, :D_MODEL].astype(F32)
    g1 = gate_ref[:, D_MODEL:].astype(F32)
    merged = (g0 * y_a + g1 * y_b).astype(BF16)
    x1_ref[...] = x_ref[...] + jnp.dot(merged, wout_ref[...], preferred_element_type=F32)


def _merge(x2, o, ys, gates, woa, wglu, bglu, wob, wout, tm):
    t = x2.shape[0]
    row = lambda i: (i, 0)
    const = lambda i: (0, 0)
    return pl.pallas_call(
        _merge_kernel,
        grid=(t // tm,),
        in_specs=[
            pl.BlockSpec((tm, D_MODEL), row),
            pl.BlockSpec((tm, FOX_WIDTH), row),
            pl.BlockSpec((tm, SSM_WIDTH), row),
            pl.BlockSpec((tm, 2 * D_MODEL), row),
            pl.BlockSpec((FOX_WIDTH, D_MODEL), const),
            pl.BlockSpec((SSM_WIDTH, SSM_WIDTH), const),
            pl.BlockSpec((1, SSM_WIDTH), const),
            pl.BlockSpec((SSM_WIDTH, D_MODEL), const),
            pl.BlockSpec((D_MODEL, D_MODEL), const),
        ],
        out_specs=pl.BlockSpec((tm, D_MODEL), row),
        out_shape=jax.ShapeDtypeStruct((t, D_MODEL), F32),
        compiler_params=_cparams(("arbitrary",)),
        name="merge",
    )(x2, o, ys, gates, woa, wglu, bglu, wob, wout)


def _route(logits):
    lane = lax.broadcasted_iota(jnp.int32, logits.shape, 1)
    neg = -jnp.inf
    big = jnp.int32(ROUTER_COLS)
    is_g = lane < N_GROUPS
    lg = jnp.where(is_g, logits, neg)
    gmax = jnp.max(lg, axis=1, keepdims=True)
    p_sel = 1.0 / jnp.sum(jnp.exp(lg - gmax), axis=1, keepdims=True)
    g_idx = jnp.min(jnp.where(lg == gmax, lane, big), axis=1, keepdims=True)
    lo = N_GROUPS + g_idx * EXPERTS_PER_GROUP
    in_grp = (lane >= lo) & (lane < lo + EXPERTS_PER_GROUP)
    le = jnp.where(in_grp, logits, neg)
    v1 = jnp.max(le, axis=1, keepdims=True)
    i1 = jnp.min(jnp.where(le == v1, lane, big), axis=1, keepdims=True)
    le2 = jnp.where(lane == i1, neg, le)
    v2 = jnp.max(le2, axis=1, keepdims=True)
    i2 = jnp.min(jnp.where(le2 == v2, lane, big), axis=1, keepdims=True)
    e2 = jnp.exp(v2 - v1)
    w1 = p_sel / (1.0 + e2)
    w2 = p_sel * e2 / (1.0 + e2)
    cw = jnp.where(lane == i1, w1, 0.0) + jnp.where(lane == i2, w2, 0.0)
    cw4 = jnp.zeros_like(cw)
    for grp in range(N_GROUPS):
        shift = N_GROUPS + grp * EXPERTS_PER_GROUP
        cw4 = cw4 + jnp.where(g_idx == grp, pltpu.roll(cw, ROUTER_COLS - shift, axis=1), 0.0)
    onehot = jnp.where(lane == g_idx, 1.0, 0.0)
    return cw4, onehot


def _route_kernel(x1_ref, gf_ref, wr_ref, br_ref, ltri_ref, hs_ref, cws_ref, pos_ref, gid_ref):
    tmb = x1_ref.shape[0]
    mp = hs_ref.shape[0]
    h = _rms(x1_ref[...], gf_ref[...])
    h_hi = h.astype(BF16)
    h_lo = (h - h_hi.astype(F32)).astype(BF16)
    logits = (jnp.dot(h_hi, wr_ref[0], preferred_element_type=F32)
              + jnp.dot(h_lo, wr_ref[0], preferred_element_type=F32)
              + jnp.dot(h_hi, wr_ref[1], preferred_element_type=F32)) + br_ref[...]
    cw4, onehot = _route(logits)

    lane1 = lax.broadcasted_iota(jnp.int32, (1, ROUTER_COLS), 1)
    counts = jnp.sum(onehot, axis=0, keepdims=True)
    nch = jnp.floor((counts + (MOE_CHUNK - 1)) * (1.0 / MOE_CHUNK))
    inc = nch
    for sh in (1, 2):
        inc = inc + jnp.where(lane1 >= sh, pltpu.roll(inc, sh, axis=1), 0.0)
    start = (inc - nch) * MOE_CHUNK
    rank = jnp.dot(ltri_ref[...], onehot.astype(BF16), preferred_element_type=F32)
    pos = jnp.sum(onehot * (start + rank), axis=1, keepdims=True)
    pos_rep = jnp.broadcast_to(pos, (tmb, ROUTER_COLS))
    pos_ref[...] = pos_rep
    pos_row = pos_rep.T[0:1, :]
    prow = lax.broadcasted_iota(jnp.int32, (mp, tmb), 0).astype(F32)
    perm = jnp.where(prow == pos_row, 1.0, 0.0).astype(BF16)
    hs_ref[...] = jnp.dot(perm, h_hi, preferred_element_type=F32).astype(BF16)
    cw_hi = cw4.astype(BF16)
    cw_lo = (cw4 - cw_hi.astype(F32)).astype(BF16)
    cws_ref[...] = (jnp.dot(perm, cw_hi, preferred_element_type=F32)
                    + jnp.dot(perm, cw_lo, preferred_element_type=F32))
    chunk = lax.broadcasted_iota(jnp.int32, (8, ROUTER_COLS), 1).astype(F32)
    gid = jnp.zeros((8, ROUTER_COLS), F32)
    for grp in range(N_GROUPS):
        end = jnp.sum(jnp.where(lane1 == grp, inc, 0.0), axis=1, keepdims=True)
        gid = gid + jnp.where(chunk >= end, 1.0, 0.0)
    gid_ref[...] = gid.astype(jnp.int32)


def _route_call(x1, g_ffn, wr, br, ltri, tmb, mp):
    t = x1.shape[0]
    nb = t // tmb
    return pl.pallas_call(
        _route_kernel,
        grid=(nb,),
        in_specs=[
            pl.BlockSpec((tmb, D_MODEL), lambda i: (i, 0)),
            pl.BlockSpec((1, D_MODEL), lambda i: (0, 0)),
            pl.BlockSpec((2, D_MODEL, ROUTER_COLS), lambda i: (0, 0, 0)),
            pl.BlockSpec((1, ROUTER_COLS), lambda i: (0, 0)),
            pl.BlockSpec((tmb, tmb), lambda i: (0, 0)),
        ],
        out_specs=[
            pl.BlockSpec((mp, D_MODEL), lambda i: (i, 0)),
            pl.BlockSpec((mp, ROUTER_COLS), lambda i: (i, 0)),
            pl.BlockSpec((tmb, ROUTER_COLS), lambda i: (i, 0)),
            pl.BlockSpec((None, 8, ROUTER_COLS), lambda i: (i, 0, 0)),
        ],
        out_shape=[
            jax.ShapeDtypeStruct((nb * mp, D_MODEL), BF16),
            jax.ShapeDtypeStruct((nb * mp, ROUTER_COLS), F32),
            jax.ShapeDtypeStruct((t, ROUTER_COLS), F32),
            jax.ShapeDtypeStruct((nb, 8, ROUTER_COLS), jnp.int32),
        ],
        compiler_params=_cparams(("arbitrary",)),
        name="route",
    )(x1, g_ffn, wr, br, ltri)


def _experts_kernel(order_ref, gids_ref, hs_ref, cws_ref, wgu_ref, wd_ref, exp_ref, ys_ref):
    valid = gids_ref[pl.program_id(0)] < N_GROUPS
    ff = EXPERTS_PER_GROUP * EXPERT_FF

    @pl.when(valid)
    def _():
        gu = jnp.dot(hs_ref[...], wgu_ref[...], preferred_element_type=F32)
        cw = cws_ref[...]
        cw_hi = cw.astype(BF16)
        cw_lo = (cw - cw_hi.astype(F32)).astype(BF16)
        cwx = (jnp.dot(cw_hi, exp_ref[...], preferred_element_type=F32)
               + jnp.dot(cw_lo, exp_ref[...], preferred_element_type=F32))
        hg = gu[:, :ff]
        act = (hg * jax.nn.sigmoid(hg) * gu[:, ff:] * cwx).astype(BF16)
        ys_ref[...] = jnp.dot(act, wd_ref[...], preferred_element_type=F32).astype(BF16)

    @pl.when(jnp.logical_not(valid))
    def _():
        ys_ref[...] = jnp.zeros(ys_ref.shape, BF16)


def _experts_call(order, gids, hs, cws, wgu, wd, expand):
    n_chunks = order.shape[0]
    ff = EXPERTS_PER_GROUP * EXPERT_FF
    grp = lambda s, order_ref, gids_ref: (jnp.minimum(gids_ref[s], N_GROUPS - 1), 0, 0)
    rows = lambda s, order_ref, gids_ref: (order_ref[s], 0)
    return pl.pallas_call(
        _experts_kernel,
        grid_spec=pltpu.PrefetchScalarGridSpec(
            num_scalar_prefetch=2,
            grid=(n_chunks,),
            in_specs=[
                pl.BlockSpec((MOE_CHUNK, D_MODEL), rows),
                pl.BlockSpec((MOE_CHUNK, ROUTER_COLS), rows),
                pl.BlockSpec((None, D_MODEL, 2 * ff), grp),
                pl.BlockSpec((None, ff, D_MODEL), grp),
                pl.BlockSpec((ROUTER_COLS, ff), lambda s, order_ref, gids_ref: (0, 0)),
            ],
            out_specs=pl.BlockSpec((MOE_CHUNK, D_MODEL), rows),
        ),
        out_shape=jax.ShapeDtypeStruct(hs.shape, BF16),
        compiler_params=_cparams(("arbitrary",)),
        name="experts",
    )(order, gids, hs, cws, wgu, wd, expand)


def _combine_kernel(x1_ref, ys_ref, pos_ref, gfin_ref, out_ref):
    tmb = x1_ref.shape[0]
    mp = ys_ref.shape[0]
    pos = pos_ref[...]
    lane = lax.broadcasted_iota(jnp.int32, pos.shape, 1).astype(F32)
    del tmb
    sel = jnp.concatenate(
        [jnp.where(pos == lane + float(c * ROUTER_COLS), 1.0, 0.0).astype(BF16)
         for c in range(mp // ROUTER_COLS)], axis=1)
    moe = jnp.dot(sel, ys_ref[...], preferred_element_type=F32)
    out_ref[...] = _rms(x1_ref[...] + moe, gfin_ref[...])


def _combine_call(x1, ys, pos, g_final, tmb, mp):
    t = x1.shape[0]
    return pl.pallas_call(
        _combine_kernel,
        grid=(t // tmb,),
        in_specs=[
            pl.BlockSpec((tmb, D_MODEL), lambda i: (i, 0)),
            pl.BlockSpec((mp, D_MODEL), lambda i: (i, 0)),
            pl.BlockSpec((tmb, ROUTER_COLS), lambda i: (i, 0)),
            pl.BlockSpec((1, D_MODEL), lambda i: (0, 0)),
        ],
        out_specs=pl.BlockSpec((tmb, D_MODEL), lambda i: (i, 0)),
        out_shape=jax.ShapeDtypeStruct((t, D_MODEL), F32),
        compiler_params=_cparams(("arbitrary",)),
        name="combine",
    )(x1, ys, pos, g_final)


def _ssm_params(lambda_re, lambda_im, log_step, b_re, b_im, c_re, c_im):
    dt = jnp.exp(log_step)[:, None]
    mag = jnp.exp(lambda_re * dt)
    a_re = mag * jnp.cos(lambda_im * dt)
    a_im = mag * jnp.sin(lambda_im * dt)
    den = lambda_re * lambda_re + lambda_im * lambda_im
    num_re = a_re - 1.0
    z_re = (num_re * lambda_re + a_im * lambda_im) / den
    z_im = (a_im * lambda_re - num_re * lambda_im) / den
    bb_re = z_re[..., None] * b_re - z_im[..., None] * b_im
    bb_im = z_re[..., None] * b_im + z_im[..., None] * b_re
    gl = SSM_GROUPS // SUPER
    eye = jnp.eye(gl, dtype=F32)

    def in_blk(bb):
        t = bb.reshape(SUPER, gl, SSM_STATE, SSM_GROUP).transpose(0, 1, 3, 2)
        t = t[:, :, :, None, :] * eye[None, :, None, :, None]
        return t.reshape(SUPER, gl * SSM_GROUP, gl * SSM_STATE)

    def out_blk(cc):
        t = cc.reshape(SUPER, gl, SSM_GROUP, SSM_STATE).transpose(0, 1, 3, 2)
        t = t[:, :, :, None, :] * eye[None, :, None, :, None]
        return t.reshape(SUPER, gl * SSM_STATE, gl * SSM_GROUP)

    bm = jnp.concatenate([in_blk(bb_re), in_blk(bb_im)], axis=1).astype(BF16)
    cm = jnp.concatenate([out_blk(c_re), out_blk(-c_im)], axis=2).astype(BF16)
    ar_row = a_re.reshape(1, SSM_GROUPS * SSM_STATE)
    ai_row = a_im.reshape(1, SSM_GROUPS * SSM_STATE)
    ar = jnp.broadcast_to(ar_row, (8, SSM_GROUPS * SSM_STATE))
    ai = jnp.concatenate([jnp.broadcast_to(-ai_row, (4, ai_row.shape[1])),
                          jnp.broadcast_to(ai_row, (4, ai_row.shape[1]))], axis=0)
    return bm, cm, ar, ai


def _pick(n, pref):
    return pref if n % pref == 0 else n


def _mixers(x, g_mix, w_in, b_forget, b_gate, w_out_a, lambda_re, lambda_im, log_step,
            ssm_b_re, ssm_b_im, ssm_c_re, ssm_c_im, ssm_d, w_glu, b_glu, w_out_b, w_out):
    bsz, seq, _ = x.shape
    assert bsz == 4, "the SSM kernel packs (re/im) x 4 batches onto the 8 sublanes"
    t = bsz * seq
    layer = 0
    x2 = x.reshape(t, D_MODEL)

    w = w_in[layer]
    fw = FOX_WIDTH
    wq, wk, wv = w[:, :fw], w[:, fw:2 * fw], w[:, 2 * fw:3 * fw]
    wf = w[:, 3 * fw:3 * fw + HEADS]
    wu = w[:, 3 * fw + HEADS:3 * fw + HEADS + SSM_WIDTH]
    wgt = w[:, 3 * fw + HEADS + SSM_WIDTH:]
    scale = LOG2E / math.sqrt(HEAD_DIM)
    w1 = jnp.concatenate(
        [wq * scale, wk, wv, wu, wgt, jnp.pad(wf, ((0, 0), (0, LANES - HEADS)))], axis=1).astype(BF16)
    bf_pad = jnp.pad(b_forget[layer], (0, LANES - HEADS)).reshape(1, LANES)
    selq, selk, auxc = _aux_constants()

    tm = _pick(seq, 512)
    qx, kx, vx, u, gates = _proj(x2, g_mix[layer].reshape(1, D_MODEL), w1, bf_pad,
                                 b_gate[layer].reshape(1, 2 * D_MODEL), selq, selk, auxc, tm, seq)

    hb = HEADS * HEAD_BLOCK
    tile = _pick(seq, ATTN_TILE)
    o = _attn(qx.reshape(bsz, seq, hb), kx.reshape(bsz, seq, hb), vx.reshape(bsz, seq, hb),
              tile, ATTN_HEADS).reshape(t, FOX_WIDTH)

    bm, cm, ar, ai = _ssm_params(lambda_re[layer], lambda_im[layer], log_step[layer],
                                 ssm_b_re[layer], ssm_b_im[layer], ssm_c_re[layer], ssm_c_im[layer])
    ub = u.reshape(bsz, seq, SSM_WIDTH).transpose(1, 0, 2)
    u8 = jnp.concatenate([ub, ub], axis=1).reshape(seq * 8, SSM_WIDTH)
    tb = _pick(seq, 128)
    y8 = _ssm(u8, bm, cm, ar, ai, ssm_d[layer].reshape(1, SSM_WIDTH), tb)
    ys = y8.reshape(seq, 8, SSM_WIDTH)[:, :bsz].transpose(1, 0, 2).reshape(t, SSM_WIDTH)

    x1 = _merge(x2, o, ys, gates, w_out_a[layer].astype(BF16), w_glu[layer].astype(BF16),
                b_glu[layer].reshape(1, SSM_WIDTH), w_out_b[layer].astype(BF16),
                w_out[layer].astype(BF16), tm)
    return x1, o, ys


def kernel(x, g_mix, w_in, b_forget, b_gate, w_out_a, lambda_re, lambda_im, log_step, ssm_b_re, ssm_b_im, ssm_c_re, ssm_c_im, ssm_d, w_glu, b_glu, w_out_b, w_out, g_ffn, w_router_group, b_router_group, w_router_expert, b_router_expert, w_exp_gate, w_exp_up, w_exp_down, g_final):
    bsz, seq, _ = x.shape
    t = bsz * seq
    layer = 0
    x1, _, _ = _mixers(x, g_mix, w_in, b_forget, b_gate, w_out_a, lambda_re, lambda_im, log_step,
                       ssm_b_re, ssm_b_im, ssm_c_re, ssm_c_im, ssm_d, w_glu, b_glu, w_out_b, w_out)

    wr_f = jnp.concatenate([w_router_group[layer], w_router_expert[layer]], axis=1)
    wr_f = jnp.pad(wr_f, ((0, 0), (0, ROUTER_COLS - N_GROUPS - N_EXPERTS)))
    wr_hi = wr_f.astype(BF16)
    wr_lo = (wr_f - wr_hi.astype(F32)).astype(BF16)
    wr = jnp.stack([wr_hi, wr_lo])
    br = jnp.pad(jnp.concatenate([b_router_group[layer], b_router_expert[layer]]),
                 (0, ROUTER_COLS - N_GROUPS - N_EXPERTS)).reshape(1, ROUTER_COLS)
    ff = EXPERTS_PER_GROUP * EXPERT_FF

    def cols(wexp):
        return wexp.reshape(N_GROUPS, EXPERTS_PER_GROUP, D_MODEL, EXPERT_FF).transpose(0, 2, 1, 3).reshape(
            N_GROUPS, D_MODEL, ff)

    wgu = jnp.concatenate([cols(w_exp_gate[layer]), cols(w_exp_up[layer])], axis=2).astype(BF16)
    wd = w_exp_down[layer].reshape(N_GROUPS, ff, D_MODEL).astype(BF16)
    expand = (jnp.arange(ROUTER_COLS)[:, None] == jnp.arange(ff)[None, :] // EXPERT_FF).astype(BF16)

    tmb = _pick(t, MOE_BLOCK)
    mp = tmb + N_GROUPS * MOE_CHUNK
    ltri = jnp.tril(jnp.ones((tmb, tmb), BF16), -1)
    hs, cws, pos, gid = _route_call(x1, g_ffn[layer].reshape(1, D_MODEL), wr, br, ltri, tmb, mp)

    cpb = mp // MOE_CHUNK
    gid_flat = gid[:, 0, :cpb].reshape(-1)
    n_chunks = gid_flat.shape[0]
    key = jnp.sort(gid_flat * n_chunks + jnp.arange(n_chunks, dtype=jnp.int32))
    ys = _experts_call(key % n_chunks, key // n_chunks, hs, cws, wgu, wd, expand)
    out = _combine_call(x1, ys, pos, g_final.reshape(1, D_MODEL), tmb, mp)
    return out.reshape(bsz, seq, D_MODEL)
```

```python
import functools
import math

import jax
import jax.numpy as jnp
from jax import lax
from jax.experimental import pallas as pl
from jax.experimental.pallas import tpu as pltpu

D_MODEL = 1024
HEADS = 8
HEAD_DIM = 64
FOX_WIDTH = HEADS * HEAD_DIM
SSM_WIDTH = 512
SSM_GROUP = 16
SSM_GROUPS = 32
SSM_STATE = 64
N_GROUPS = 4
EXPERTS_PER_GROUP = 4
N_EXPERTS = 16
EXPERT_FF = 256
EPS = 1e-6

LANES = 128
HEAD_BLOCK = LANES
AUX = HEAD_BLOCK - HEAD_DIM
V_ROWS = HEAD_DIM + 16
SUPER = 4
SG_CH = SSM_WIDTH // SUPER
SG_ST = SSM_GROUPS // SUPER * SSM_STATE
ROUTER_COLS = LANES

F32 = jnp.float32
BF16 = jnp.bfloat16
VMEM_LIMIT = 56 * 1024 * 1024
LOG2E = 1.4426950408889634
ATTN_HEADS = 4
ATTN_TILE = 512
MOE_BLOCK = 1024
MOE_CHUNK = 128


def _cparams(sem):
    return pltpu.CompilerParams(dimension_semantics=sem, vmem_limit_bytes=VMEM_LIMIT)


def _rms(x, g):
    return x * lax.rsqrt(jnp.mean(x * x, axis=-1, keepdims=True) + EPS) * g


def _proj_kernel(x_ref, g_ref, w_ref, bf_ref, bg_ref, selq_ref, selk_ref, auxc_ref,
                 q_ref, k_ref, v_ref, u_ref, gate_ref, carry_scr):
    tm = x_ref.shape[0]

    @pl.when(pl.program_id(1) == 0)
    def _():
        carry_scr[...] = jnp.zeros_like(carry_scr)

    h = _rms(x_ref[...], g_ref[...]).astype(BF16)

    def mm(lo, hi):
        return jnp.dot(h, w_ref[:, lo:hi], preferred_element_type=F32)

    w = FOX_WIDTH
    f = mm(8 * w, 8 * w + LANES) + bf_ref[...]
    lane = lax.broadcasted_iota(jnp.int32, f.shape, 1)
    row = lax.broadcasted_iota(jnp.int32, f.shape, 0)
    c = jnp.where(lane < HEADS, (jnp.minimum(f, 0.0) - jnp.log(1.0 + jnp.exp(-jnp.abs(f)))) * LOG2E, 0.0)
    sh = 1
    while sh < tm:
        c = c + jnp.where(row >= sh, pltpu.roll(c, sh, axis=0), 0.0)
        sh *= 2
    c = c + carry_scr[...]
    carry_scr[...] = c[tm - 1:tm, :]
    c1 = c.astype(BF16)
    r = c - c1.astype(F32)
    c2 = r.astype(BF16)
    c3 = (r - c2.astype(F32)).astype(BF16)

    def place(sel_ref):
        return (jnp.dot(c1, sel_ref[0], preferred_element_type=F32)
                + jnp.dot(c2, sel_ref[1], preferred_element_type=F32)
                + jnp.dot(c3, sel_ref[2], preferred_element_type=F32))

    aux_q = place(selq_ref) + auxc_ref[0:1, :]
    aux_k = place(selk_ref) + auxc_ref[1:2, :]
    aux_v = jnp.broadcast_to(auxc_ref[2:3, :], aux_q.shape)

    for out_ref, lo, aux in ((q_ref, 0, aux_q), (k_ref, w, aux_k), (v_ref, 2 * w, aux_v)):
        val = mm(lo, lo + w)
        for hd in range(HEADS):
            blk = jnp.concatenate([val[:, hd * HEAD_DIM:(hd + 1) * HEAD_DIM],
                                   aux[:, hd * AUX:(hd + 1) * AUX]], axis=1)
            out_ref[:, hd * HEAD_BLOCK:(hd + 1) * HEAD_BLOCK] = blk.astype(BF16)

    u_ref[...] = mm(3 * w, 4 * w).astype(BF16)
    for cc in range(4):
        lo = 4 * w + cc * w
        gate_ref[:, cc * w:(cc + 1) * w] = jax.nn.sigmoid(
            mm(lo, lo + w) + bg_ref[:, cc * w:(cc + 1) * w]).astype(BF16)


def _proj(x2, g_mix, w1, bf_pad, b_gate, selq, selk, auxc, tm, seq):
    t = x2.shape[0]
    bsz = t // seq
    nt = seq // tm
    n1 = w1.shape[1]
    hb = HEADS * HEAD_BLOCK
    row = lambda b, i: (b * nt + i, 0)
    const = lambda b, i: (0, 0)
    const3 = lambda b, i: (0, 0, 0)
    return pl.pallas_call(
        _proj_kernel,
        grid=(bsz, nt),
        in_specs=[
            pl.BlockSpec((tm, D_MODEL), row),
            pl.BlockSpec((1, D_MODEL), const),
            pl.BlockSpec((D_MODEL, n1), const),
            pl.BlockSpec((1, LANES), const),
            pl.BlockSpec((1, 2 * D_MODEL), const),
            pl.BlockSpec((3, LANES, HEADS * AUX), const3),
            pl.BlockSpec((3, LANES, HEADS * AUX), const3),
            pl.BlockSpec((8, HEADS * AUX), const),
        ],
        out_specs=[
            pl.BlockSpec((tm, hb), row),
            pl.BlockSpec((tm, hb), row),
            pl.BlockSpec((tm, hb), row),
            pl.BlockSpec((tm, SSM_WIDTH), lambda b, i: (i, b)),
            pl.BlockSpec((tm, 2 * D_MODEL), row),
        ],
        out_shape=[
            jax.ShapeDtypeStruct((t, hb), BF16),
            jax.ShapeDtypeStruct((t, hb), BF16),
            jax.ShapeDtypeStruct((t, hb), BF16),
            jax.ShapeDtypeStruct((seq, bsz * SSM_WIDTH), BF16),
            jax.ShapeDtypeStruct((t, 2 * D_MODEL), BF16),
        ],
        scratch_shapes=[pltpu.VMEM((1, LANES), F32)],
        compiler_params=_cparams(("arbitrary", "arbitrary")),
        name="proj",
    )(x2, g_mix, w1, bf_pad, b_gate, selq, selk, auxc)


def _aux_constants():
    selq = jnp.zeros((3, LANES, HEADS * AUX), F32)
    selk = jnp.zeros((3, LANES, HEADS * AUX), F32)
    auxc = jnp.zeros((8, HEADS * AUX), F32)
    hd = jnp.arange(HEADS)
    for p in range(3):
        selq = selq.at[p, hd, hd * AUX + p].set(1.0)
        selk = selk.at[p, hd, hd * AUX + 3 + p].set(-1.0)
        auxc = auxc.at[0, hd * AUX + 3 + p].set(1.0)
        auxc = auxc.at[1, hd * AUX + p].set(1.0)
    auxc = auxc.at[2, hd * AUX].set(1.0)
    return selq.astype(BF16), selk.astype(BF16), auxc


def _attn_kernel(q_ref, k_ref, v_ref, o_ref, vt_scr, s_scr, m_scr, acc_scr, *, tile, heads):
    i = pl.program_id(2)
    n = k_ref.shape[0] // tile
    nt = (((1,), (1,)), ((), ()))

    @pl.when(i == 0)
    def _():
        for g in range(heads):
            def tr(c, carry, g=g):
                r0 = pl.multiple_of(c * tile, tile)
                blk = v_ref[pl.ds(r0, tile), g * HEAD_BLOCK:(g + 1) * HEAD_BLOCK].astype(F32)
                vt_scr[g, c] = blk.T[:V_ROWS].astype(BF16)
                return carry
            lax.fori_loop(0, n, tr, 0)

    m_scr[...] = jnp.full(m_scr.shape, -jnp.inf, F32)
    acc_scr[...] = jnp.zeros(acc_scr.shape, F32)

    def qk(slot, t):
        r0 = pl.multiple_of(t * tile, tile)
        for g in range(heads):
            s_scr[slot, g] = lax.dot_general(
                k_ref[pl.ds(r0, tile), g * HEAD_BLOCK:(g + 1) * HEAD_BLOCK],
                q_ref[:, g * HEAD_BLOCK:(g + 1) * HEAD_BLOCK], nt, preferred_element_type=F32)

    def softmax(slot, masked):
        ps, alphas = [], []
        for g in range(heads):
            s = s_scr[slot, g]
            if masked:
                kpos = lax.broadcasted_iota(jnp.int32, s.shape, 0)
                qpos = lax.broadcasted_iota(jnp.int32, s.shape, 1)
                s = jnp.where(kpos <= qpos, s, -jnp.inf)
            m = m_scr[g]
            m_new = jnp.maximum(m, jnp.max(s, axis=0, keepdims=True))
            ps.append(jnp.exp2(s - m_new).astype(BF16))
            alphas.append(jnp.exp2(m - m_new))
            m_scr[g] = m_new
        return ps, alphas

    def pv(pa, t):
        ps, alphas = pa
        for g in range(heads):
            acc_scr[g] = (alphas[g] * acc_scr[g]
                          + jnp.dot(vt_scr[g, t], ps[g], preferred_element_type=F32))

    qk(0, 0)

    def body(tt, carry):
        t0 = 2 * tt
        qk(1, t0 + 1)
        pa0 = softmax(0, False)
        qk(0, t0 + 2)
        pv(pa0, t0)
        pv(softmax(1, False), t0 + 1)
        return carry

    lax.fori_loop(0, lax.shift_right_logical(i, 1), body, 0)

    @pl.when((i & 1) == 0)
    def _():
        pv(softmax(0, True), i)

    @pl.when((i & 1) == 1)
    def _():
        qk(1, i)
        pv(softmax(0, False), i - 1)
        pv(softmax(1, True), i)

    for pair in range(heads // 2):
        a0 = acc_scr[2 * pair]
        a1 = acc_scr[2 * pair + 1]
        o2 = jnp.concatenate([a0[:HEAD_DIM] / a0[HEAD_DIM:HEAD_DIM + 1],
                              a1[:HEAD_DIM] / a1[HEAD_DIM:HEAD_DIM + 1]], axis=0)
        o_ref[:, pair * LANES:(pair + 1) * LANES] = o2.T.astype(BF16)


def _attn(qx, kx, vx, tile, heads):
    b, s, _ = qx.shape
    n = s // tile
    hb = heads * HEAD_BLOCK
    return pl.pallas_call(
        functools.partial(_attn_kernel, tile=tile, heads=heads),
        grid=(b, HEADS // heads, n),
        in_specs=[
            pl.BlockSpec((None, tile, hb), lambda bi, hi, i: (bi, i, hi)),
            pl.BlockSpec((None, s, hb), lambda bi, hi, i: (bi, 0, hi)),
            pl.BlockSpec((None, s, hb), lambda bi, hi, i: (bi, 0, hi)),
        ],
        out_specs=pl.BlockSpec((None, tile, heads * HEAD_DIM), lambda bi, hi, i: (bi, i, hi)),
        out_shape=jax.ShapeDtypeStruct((b, s, FOX_WIDTH), BF16),
        scratch_shapes=[
            pltpu.VMEM((heads, n, V_ROWS, tile), BF16),
            pltpu.VMEM((2, heads, tile, tile), F32),
            pltpu.VMEM((heads, 1, tile), F32),
            pltpu.VMEM((heads, V_ROWS, tile), F32),
        ],
        compiler_params=_cparams(("arbitrary", "arbitrary", "arbitrary")),
        name="attn",
    )(qx, kx, vx)


def _ssm_kernel(u_ref, bm_ref, cm_ref, ar_ref, ai_ref, d_ref, y_ref, x_scr, s_scr, *, tb):
    rows = tb * 8

    @pl.when(pl.program_id(0) == 0)
    def _():
        s_scr[...] = jnp.zeros_like(s_scr)

    u = u_ref[...]
    is_re = (lax.broadcasted_iota(jnp.int32, (rows, SG_CH), 0) % 8) < 4
    zero = jnp.zeros((rows, SG_CH), BF16)
    for g in range(SUPER):
        ug = u[:, g * SG_CH:(g + 1) * SG_CH]
        lhs = jnp.concatenate([jnp.where(is_re, ug, zero), jnp.where(is_re, zero, ug)], axis=1)
        x_scr[:, g * SG_ST:(g + 1) * SG_ST] = jnp.dot(lhs, bm_ref[g], preferred_element_type=F32)

    half = 2 * SG_ST
    for c in range(SUPER * SG_ST // half):
        lo = c * half
        ar = ar_ref[:, lo:lo + half]
        ai = ai_ref[:, lo:lo + half]

        def body(t, s, lo=lo, ar=ar, ai=ai):
            r0 = pl.multiple_of(t * 8, 8)
            s = ar * s + ai * pltpu.roll(s, 4, axis=0) + x_scr[pl.ds(r0, 8), lo:lo + half]
            x_scr[pl.ds(r0, 8), lo:lo + half] = s
            return s

        s_scr[:, lo:lo + half] = lax.fori_loop(0, tb, body, s_scr[:, lo:lo + half], unroll=4)

    for g in range(SUPER):
        st = x_scr[:, g * SG_ST:(g + 1) * SG_ST].astype(BF16)
        o = jnp.dot(st, cm_ref[g], preferred_element_type=F32)
        y = o[:, :SG_CH] + pltpu.roll(o[:, SG_CH:], rows - 4, axis=0)
        y = y + d_ref[:, g * SG_CH:(g + 1) * SG_CH] * u[:, g * SG_CH:(g + 1) * SG_CH].astype(F32)
        y_ref[:, g * SG_CH:(g + 1) * SG_CH] = y.astype(BF16)


def _ssm(u8, bm, cm, ar, ai, d, tb):
    rows_total = u8.shape[0]
    rows = tb * 8
    return pl.pallas_call(
        functools.partial(_ssm_kernel, tb=tb),
        grid=(rows_total // rows,),
        in_specs=[
            pl.BlockSpec((rows, SSM_WIDTH), lambda i: (i, 0)),
            pl.BlockSpec((SUPER, 2 * SG_CH, SG_ST), lambda i: (0, 0, 0)),
            pl.BlockSpec((SUPER, SG_ST, 2 * SG_CH), lambda i: (0, 0, 0)),
            pl.BlockSpec((8, SUPER * SG_ST), lambda i: (0, 0)),
            pl.BlockSpec((8, SUPER * SG_ST), lambda i: (0, 0)),
            pl.BlockSpec((1, SSM_WIDTH), lambda i: (0, 0)),
        ],
        out_specs=pl.BlockSpec((rows, SSM_WIDTH), lambda i: (i, 0)),
        out_shape=jax.ShapeDtypeStruct((rows_total, SSM_WIDTH), BF16),
        scratch_shapes=[
            pltpu.VMEM((rows, SUPER * SG_ST), F32),
            pltpu.VMEM((8, SUPER * SG_ST), F32),
        ],
        compiler_params=_cparams(("arbitrary",)),
        name="ssm",
    )(u8, bm, cm, ar, ai, d)


def _merge_kernel(x_ref, o_ref, ys_ref, gate_ref, woa_ref, wglu_ref, bglu_ref, wob_ref, wout_ref,
                  x1_ref):
    y_a = jnp.dot(o_ref[...], woa_ref[...], preferred_element_type=F32)
    z = jax.nn.gelu(ys_ref[...].astype(F32))
    zg = jnp.dot(z.astype(BF16), wglu_ref[...], preferred_element_type=F32) + bglu_ref[...]
    z = z * jax.nn.sigmoid(zg)
    y_b = jnp.dot(z.astype(BF16), wob_ref[...], preferred_element_type=F32)
    g0 = gate_ref[:, :D_MODEL].astype(F32)
    g1 = gate_ref[:, D_MODEL:].astype(F32)
    merged = (g0 * y_a + g1 * y_b).astype(BF16)
    x1_ref[...] = x_ref[...] + jnp.dot(merged, wout_ref[...], preferred_element_type=F32)


def _merge(x2, o, y8, gates, woa, wglu, bglu, wob, wout, tm):
    t = x2.shape[0]
    seq = y8.shape[0]
    nt = seq // tm
    row = lambda b, i: (b * nt + i, 0)
    const = lambda b, i: (0, 0)
    return pl.pallas_call(
        _merge_kernel,
        grid=(t // seq, nt),
        in_specs=[
            pl.BlockSpec((tm, D_MODEL), row),
            pl.BlockSpec((tm, FOX_WIDTH), row),
            pl.BlockSpec((tm, SSM_WIDTH), lambda b, i: (i, b)),
            pl.BlockSpec((tm, 2 * D_MODEL), row),
            pl.BlockSpec((FOX_WIDTH, D_MODEL), const),
            pl.BlockSpec((SSM_WIDTH, SSM_WIDTH), const),
            pl.BlockSpec((1, SSM_WIDTH), const),
            pl.BlockSpec((SSM_WIDTH, D_MODEL), const),
            pl.BlockSpec((D_MODEL, D_MODEL), const),
        ],
        out_specs=pl.BlockSpec((tm, D_MODEL), row),
        out_shape=jax.ShapeDtypeStruct((t, D_MODEL), F32),
        compiler_params=_cparams(("arbitrary", "arbitrary")),
        name="merge",
    )(x2, o, y8, gates, woa, wglu, bglu, wob, wout)


def _route(logits):
    lane = lax.broadcasted_iota(jnp.int32, logits.shape, 1)
    neg = -jnp.inf
    big = jnp.int32(ROUTER_COLS)
    is_g = lane < N_GROUPS
    lg = jnp.where(is_g, logits, neg)
    gmax = jnp.max(lg, axis=1, keepdims=True)
    p_sel = 1.0 / jnp.sum(jnp.exp(lg - gmax), axis=1, keepdims=True)
    g_idx = jnp.min(jnp.where(lg == gmax, lane, big), axis=1, keepdims=True)
    lo = N_GROUPS + g_idx * EXPERTS_PER_GROUP
    in_grp = (lane >= lo) & (lane < lo + EXPERTS_PER_GROUP)
    le = jnp.where(in_grp, logits, neg)
    v1 = jnp.max(le, axis=1, keepdims=True)
    i1 = jnp.min(jnp.where(le == v1, lane, big), axis=1, keepdims=True)
    le2 = jnp.where(lane == i1, neg, le)
    v2 = jnp.max(le2, axis=1, keepdims=True)
    i2 = jnp.min(jnp.where(le2 == v2, lane, big), axis=1, keepdims=True)
    e2 = jnp.exp(v2 - v1)
    w1 = p_sel / (1.0 + e2)
    w2 = p_sel * e2 / (1.0 + e2)
    cw = jnp.where(lane == i1, w1, 0.0) + jnp.where(lane == i2, w2, 0.0)
    cw4 = jnp.zeros_like(cw)
    for grp in range(N_GROUPS):
        shift = N_GROUPS + grp * EXPERTS_PER_GROUP
        cw4 = cw4 + jnp.where(g_idx == grp, pltpu.roll(cw, ROUTER_COLS - shift, axis=1), 0.0)
    onehot = jnp.where(lane == g_idx, 1.0, 0.0)
    return cw4, onehot


def _route_kernel(x1_ref, gf_ref, wr_ref, br_ref, ltri_ref, hs_ref, cws_ref, pos_ref, gid_ref):
    tmb = x1_ref.shape[0]
    mp = hs_ref.shape[0]
    h = _rms(x1_ref[...], gf_ref[...])
    h_hi = h.astype(BF16)
    h_lo = (h - h_hi.astype(F32)).astype(BF16)
    l_hi = jnp.dot(h_hi, jnp.concatenate([wr_ref[0], wr_ref[1]], axis=1), preferred_element_type=F32)
    logits = (l_hi[:, :ROUTER_COLS] + l_hi[:, ROUTER_COLS:]
              + jnp.dot(h_lo, wr_ref[0], preferred_element_type=F32)) + br_ref[...]
    cw4, onehot = _route(logits)

    lane1 = lax.broadcasted_iota(jnp.int32, (1, ROUTER_COLS), 1)
    counts = jnp.sum(onehot, axis=0, keepdims=True)
    nch = jnp.floor((counts + (MOE_CHUNK - 1)) * (1.0 / MOE_CHUNK))
    inc = nch
    for sh in (1, 2):
        inc = inc + jnp.where(lane1 >= sh, pltpu.roll(inc, sh, axis=1), 0.0)
    start = (inc - nch) * MOE_CHUNK
    rank = jnp.dot(ltri_ref[...], onehot.astype(BF16), preferred_element_type=F32)
    pos = jnp.sum(onehot * (start + rank), axis=1, keepdims=True)
    pos_rep = jnp.broadcast_to(pos, (tmb, ROUTER_COLS))
    pos_ref[...] = pos_rep
    pos_row = pos_rep.T[0:1, :]
    prow = lax.broadcasted_iota(jnp.int32, (mp, tmb), 0).astype(F32)
    perm = jnp.where(prow == pos_row, 1.0, 0.0).astype(BF16)
    hs_ref[...] = jnp.dot(perm, h_hi, preferred_element_type=F32).astype(BF16)
    cw_hi = cw4.astype(BF16)
    cw_lo = (cw4 - cw_hi.astype(F32)).astype(BF16)
    cw2 = jnp.dot(perm, jnp.concatenate([cw_hi, cw_lo], axis=1), preferred_element_type=F32)
    cws_ref[...] = cw2[:, :ROUTER_COLS] + cw2[:, ROUTER_COLS:]
    chunk = lax.broadcasted_iota(jnp.int32, (8, ROUTER_COLS), 1).astype(F32)
    gid = jnp.zeros((8, ROUTER_COLS), F32)
    for grp in range(N_GROUPS):
        end = jnp.sum(jnp.where(lane1 == grp, inc, 0.0), axis=1, keepdims=True)
        gid = gid + jnp.where(chunk >= end, 1.0, 0.0)
    gid_ref[...] = gid.astype(jnp.int32)


def _route_call(x1, g_ffn, wr, br, ltri, tmb, mp):
    t = x1.shape[0]
    nb = t // tmb
    return pl.pallas_call(
        _route_kernel,
        grid=(nb,),
        in_specs=[
            pl.BlockSpec((tmb, D_MODEL), lambda i: (i, 0)),
            pl.BlockSpec((1, D_MODEL), lambda i: (0, 0)),
            pl.BlockSpec((2, D_MODEL, ROUTER_COLS), lambda i: (0, 0, 0)),
            pl.BlockSpec((1, ROUTER_COLS), lambda i: (0, 0)),
            pl.BlockSpec((tmb, tmb), lambda i: (0, 0)),
        ],
        out_specs=[
            pl.BlockSpec((mp, D_MODEL), lambda i: (i, 0)),
            pl.BlockSpec((mp, ROUTER_COLS), lambda i: (i, 0)),
            pl.BlockSpec((tmb, ROUTER_COLS), lambda i: (i, 0)),
            pl.BlockSpec((None, 8, ROUTER_COLS), lambda i: (i, 0, 0)),
        ],
        out_shape=[
            jax.ShapeDtypeStruct((nb * mp, D_MODEL), BF16),
            jax.ShapeDtypeStruct((nb * mp, ROUTER_COLS), F32),
            jax.ShapeDtypeStruct((t, ROUTER_COLS), F32),
            jax.ShapeDtypeStruct((nb, 8, ROUTER_COLS), jnp.int32),
        ],
        compiler_params=_cparams(("arbitrary",)),
        name="route",
    )(x1, g_ffn, wr, br, ltri)


def _experts_kernel(order_ref, gids_ref, hs_ref, cws_ref, wg_ref, wu_ref, wd_ref, exp_ref, ys_ref):
    valid = gids_ref[pl.program_id(0)] < N_GROUPS

    @pl.when(valid)
    def _():
        x = hs_ref[...]
        hg = jnp.concatenate([jnp.dot(x, wg_ref[j], preferred_element_type=F32)
                              for j in range(EXPERTS_PER_GROUP)], axis=1)
        hu = jnp.concatenate([jnp.dot(x, wu_ref[j], preferred_element_type=F32)
                              for j in range(EXPERTS_PER_GROUP)], axis=1)
        cw = cws_ref[...]
        cw_hi = cw.astype(BF16)
        cw_lo = (cw - cw_hi.astype(F32)).astype(BF16)
        cwx = (jnp.dot(cw_hi, exp_ref[...], preferred_element_type=F32)
               + jnp.dot(cw_lo, exp_ref[...], preferred_element_type=F32))
        act = (hg * jax.nn.sigmoid(hg) * hu * cwx).astype(BF16)
        ys_ref[...] = jnp.dot(act, wd_ref[...], preferred_element_type=F32).astype(BF16)

    @pl.when(jnp.logical_not(valid))
    def _():
        ys_ref[...] = jnp.zeros(ys_ref.shape, BF16)


def _experts_call(order, gids, hs, cws, wg, wu, wd, expand):
    n_chunks = order.shape[0]
    ff = EXPERTS_PER_GROUP * EXPERT_FF
    grp = lambda s, order_ref, gids_ref: (jnp.minimum(gids_ref[s], N_GROUPS - 1), 0, 0)
    rows = lambda s, order_ref, gids_ref: (order_ref[s], 0)
    return pl.pallas_call(
        _experts_kernel,
        grid_spec=pltpu.PrefetchScalarGridSpec(
            num_scalar_prefetch=2,
            grid=(n_chunks,),
            in_specs=[
                pl.BlockSpec((MOE_CHUNK, D_MODEL), rows),
                pl.BlockSpec((MOE_CHUNK, ROUTER_COLS), rows),
                pl.BlockSpec((EXPERTS_PER_GROUP, D_MODEL, EXPERT_FF), grp),
                pl.BlockSpec((EXPERTS_PER_GROUP, D_MODEL, EXPERT_FF), grp),
                pl.BlockSpec((None, ff, D_MODEL), grp),
                pl.BlockSpec((ROUTER_COLS, ff), lambda s, order_ref, gids_ref: (0, 0)),
            ],
            out_specs=pl.BlockSpec((MOE_CHUNK, D_MODEL), rows),
        ),
        out_shape=jax.ShapeDtypeStruct(hs.shape, BF16),
        compiler_params=_cparams(("arbitrary",)),
        name="experts",
    )(order, gids, hs, cws, wg, wu, wd, expand)


def _combine_kernel(x1_ref, ys_ref, pos_ref, gfin_ref, out_ref):
    tmb = x1_ref.shape[0]
    mp = ys_ref.shape[0]
    pos = pos_ref[...]
    lane = lax.broadcasted_iota(jnp.int32, pos.shape, 1).astype(F32)
    del tmb
    sel = jnp.concatenate(
        [jnp.where(pos == lane + float(c * ROUTER_COLS), 1.0, 0.0).astype(BF16)
         for c in range(mp // ROUTER_COLS)], axis=1)
    moe = jnp.dot(sel, ys_ref[...], preferred_element_type=F32)
    out_ref[...] = _rms(x1_ref[...] + moe, gfin_ref[...])


def _combine_call(x1, ys, pos, g_final, tmb, mp):
    t = x1.shape[0]
    return pl.pallas_call(
        _combine_kernel,
        grid=(t // tmb,),
        in_specs=[
            pl.BlockSpec((tmb, D_MODEL), lambda i: (i, 0)),
            pl.BlockSpec((mp, D_MODEL), lambda i: (i, 0)),
            pl.BlockSpec((tmb, ROUTER_COLS), lambda i: (i, 0)),
            pl.BlockSpec((1, D_MODEL), lambda i: (0, 0)),
        ],
        out_specs=pl.BlockSpec((tmb, D_MODEL), lambda i: (i, 0)),
        out_shape=jax.ShapeDtypeStruct((t, D_MODEL), F32),
        compiler_params=_cparams(("arbitrary",)),
        name="combine",
    )(x1, ys, pos, g_final)


def _ssm_params(lambda_re, lambda_im, log_step, b_re, b_im, c_re, c_im):
    dt = jnp.exp(log_step)[:, None]
    mag = jnp.exp(lambda_re * dt)
    a_re = mag * jnp.cos(lambda_im * dt)
    a_im = mag * jnp.sin(lambda_im * dt)
    den = lambda_re * lambda_re + lambda_im * lambda_im
    num_re = a_re - 1.0
    z_re = (num_re * lambda_re + a_im * lambda_im) / den
    z_im = (a_im * lambda_re - num_re * lambda_im) / den
    bb_re = z_re[..., None] * b_re - z_im[..., None] * b_im
    bb_im = z_re[..., None] * b_im + z_im[..., None] * b_re
    gl = SSM_GROUPS // SUPER
    eye = jnp.eye(gl, dtype=F32)

    def in_blk(bb):
        t = bb.reshape(SUPER, gl, SSM_STATE, SSM_GROUP).transpose(0, 1, 3, 2)
        t = t[:, :, :, None, :] * eye[None, :, None, :, None]
        return t.reshape(SUPER, gl * SSM_GROUP, gl * SSM_STATE)

    def out_blk(cc):
        t = cc.reshape(SUPER, gl, SSM_GROUP, SSM_STATE).transpose(0, 1, 3, 2)
        t = t[:, :, :, None, :] * eye[None, :, None, :, None]
        return t.reshape(SUPER, gl * SSM_STATE, gl * SSM_GROUP)

    bm = jnp.concatenate([in_blk(bb_re), in_blk(bb_im)], axis=1).astype(BF16)
    cm = jnp.concatenate([out_blk(c_re), out_blk(-c_im)], axis=2).astype(BF16)
    ar_row = a_re.reshape(1, SSM_GROUPS * SSM_STATE)
    ai_row = a_im.reshape(1, SSM_GROUPS * SSM_STATE)
    ar = jnp.broadcast_to(ar_row, (8, SSM_GROUPS * SSM_STATE))
    ai = jnp.concatenate([jnp.broadcast_to(-ai_row, (4, ai_row.shape[1])),
                          jnp.broadcast_to(ai_row, (4, ai_row.shape[1]))], axis=0)
    return bm, cm, ar, ai


def _pick(n, pref):
    return pref if n % pref == 0 else n


def _mixers(x, g_mix, w_in, b_forget, b_gate, w_out_a, lambda_re, lambda_im, log_step,
            ssm_b_re, ssm_b_im, ssm_c_re, ssm_c_im, ssm_d, w_glu, b_glu, w_out_b, w_out):
    bsz, seq, _ = x.shape
    assert bsz == 4, "the SSM kernel packs (re/im) x 4 batches onto the 8 sublanes"
    t = bsz * seq
    layer = 0
    x2 = x.reshape(t, D_MODEL)

    w = w_in[layer]
    fw = FOX_WIDTH
    wq, wk, wv = w[:, :fw], w[:, fw:2 * fw], w[:, 2 * fw:3 * fw]
    wf = w[:, 3 * fw:3 * fw + HEADS]
    wu = w[:, 3 * fw + HEADS:3 * fw + HEADS + SSM_WIDTH]
    wgt = w[:, 3 * fw + HEADS + SSM_WIDTH:]
    scale = LOG2E / math.sqrt(HEAD_DIM)
    w1 = jnp.concatenate(
        [wq * scale, wk, wv, wu, wgt, jnp.pad(wf, ((0, 0), (0, LANES - HEADS)))], axis=1).astype(BF16)
    bf_pad = jnp.pad(b_forget[layer], (0, LANES - HEADS)).reshape(1, LANES)
    selq, selk, auxc = _aux_constants()

    tm = _pick(seq, 512)
    qx, kx, vx, u, gates = _proj(x2, g_mix[layer].reshape(1, D_MODEL), w1, bf_pad,
                                 b_gate[layer].reshape(1, 2 * D_MODEL), selq, selk, auxc, tm, seq)

    hb = HEADS * HEAD_BLOCK
    tile = _pick(seq, ATTN_TILE)
    o = _attn(qx.reshape(bsz, seq, hb), kx.reshape(bsz, seq, hb), vx.reshape(bsz, seq, hb),
              tile, ATTN_HEADS).reshape(t, FOX_WIDTH)

    bm, cm, ar, ai = _ssm_params(lambda_re[layer], lambda_im[layer], log_step[layer],
                                 ssm_b_re[layer], ssm_b_im[layer], ssm_c_re[layer], ssm_c_im[layer])
    u8 = jnp.concatenate([u, u], axis=1).reshape(seq * 8, SSM_WIDTH)
    tb = _pick(seq, 128)
    y8 = _ssm(u8, bm, cm, ar, ai, ssm_d[layer].reshape(1, SSM_WIDTH), tb).reshape(seq, 8 * SSM_WIDTH)

    x1 = _merge(x2, o, y8, gates, w_out_a[layer].astype(BF16), w_glu[layer].astype(BF16),
                b_glu[layer].reshape(1, SSM_WIDTH), w_out_b[layer].astype(BF16),
                w_out[layer].astype(BF16), tm)
    ys = y8.reshape(seq, 8, SSM_WIDTH)[:, :bsz].transpose(1, 0, 2).reshape(t, SSM_WIDTH)
    return x1, o, ys


def kernel(x, g_mix, w_in, b_forget, b_gate, w_out_a, lambda_re, lambda_im, log_step, ssm_b_re, ssm_b_im, ssm_c_re, ssm_c_im, ssm_d, w_glu, b_glu, w_out_b, w_out, g_ffn, w_router_group, b_router_group, w_router_expert, b_router_expert, w_exp_gate, w_exp_up, w_exp_down, g_final):
    bsz, seq, _ = x.shape
    t = bsz * seq
    layer = 0
    x1, _, _ = _mixers(x, g_mix, w_in, b_forget, b_gate, w_out_a, lambda_re, lambda_im, log_step,
                       ssm_b_re, ssm_b_im, ssm_c_re, ssm_c_im, ssm_d, w_glu, b_glu, w_out_b, w_out)

    wr_f = jnp.concatenate([w_router_group[layer], w_router_expert[layer]], axis=1)
    wr_f = jnp.pad(wr_f, ((0, 0), (0, ROUTER_COLS - N_GROUPS - N_EXPERTS)))
    wr_hi = wr_f.astype(BF16)
    wr_lo = (wr_f - wr_hi.astype(F32)).astype(BF16)
    wr = jnp.stack([wr_hi, wr_lo])
    br = jnp.pad(jnp.concatenate([b_router_group[layer], b_router_expert[layer]]),
                 (0, ROUTER_COLS - N_GROUPS - N_EXPERTS)).reshape(1, ROUTER_COLS)
    ff = EXPERTS_PER_GROUP * EXPERT_FF
    wd = w_exp_down[layer].reshape(N_GROUPS, ff, D_MODEL).astype(BF16)
    expand = (jnp.arange(ROUTER_COLS)[:, None] == jnp.arange(ff)[None, :] // EXPERT_FF).astype(BF16)

    tmb = _pick(t, MOE_BLOCK)
    mp = tmb + N_GROUPS * MOE_CHUNK
    ltri = jnp.tril(jnp.ones((tmb, tmb), BF16), -1)
    hs, cws, pos, gid = _route_call(x1, g_ffn[layer].reshape(1, D_MODEL), wr, br, ltri, tmb, mp)

    cpb = mp // MOE_CHUNK
    gid_flat = gid[:, 0, :cpb].reshape(-1)
    n_chunks = gid_flat.shape[0]
    key = jnp.sort(gid_flat * n_chunks + jnp.arange(n_chunks, dtype=jnp.int32))
    ys = _experts_call(key % n_chunks, key // n_chunks, hs, cws, w_exp_gate[layer].astype(BF16),
                       w_exp_up[layer].astype(BF16), wd, expand)
    out = _combine_call(x1, ys, pos, g_final.reshape(1, D_MODEL), tmb, mp)
    return out.reshape(bsz, seq, D_MODEL)
```

```python
import functools
import math

import jax
import jax.numpy as jnp
from jax import lax
from jax.experimental import pallas as pl
from jax.experimental.pallas import tpu as pltpu

D_MODEL = 1024
HEADS = 8
HEAD_DIM = 64
FOX_WIDTH = HEADS * HEAD_DIM
SSM_WIDTH = 512
SSM_GROUP = 16
SSM_GROUPS = 32
SSM_STATE = 64
N_GROUPS = 4
EXPERTS_PER_GROUP = 4
N_EXPERTS = 16
EXPERT_FF = 256
EPS = 1e-6

LANES = 128
HEAD_BLOCK = LANES
AUX = HEAD_BLOCK - HEAD_DIM
V_ROWS = HEAD_DIM + 16
SUPER = 4
SG_CH = SSM_WIDTH // SUPER
SG_ST = SSM_GROUPS // SUPER * SSM_STATE
ROUTER_COLS = LANES

F32 = jnp.float32
BF16 = jnp.bfloat16
VMEM_LIMIT = 56 * 1024 * 1024
LOG2E = 1.4426950408889634
ATTN_HEADS = 4
ATTN_TILE = 512
MOE_BLOCK = 1024
MOE_CHUNK = 128


def _cparams(sem):
    return pltpu.CompilerParams(dimension_semantics=sem, vmem_limit_bytes=VMEM_LIMIT)


def _rms(x, g):
    return x * lax.rsqrt(jnp.mean(x * x, axis=-1, keepdims=True) + EPS) * g


def _proj_kernel(x_ref, g_ref, w_ref, bf_ref, bg_ref, selq_ref, selk_ref, auxc_ref,
                 q_ref, k_ref, v_ref, u_ref, gate_ref, carry_scr):
    tm = x_ref.shape[0]

    @pl.when(pl.program_id(1) == 0)
    def _():
        carry_scr[...] = jnp.zeros_like(carry_scr)

    h = _rms(x_ref[...], g_ref[...]).astype(BF16)

    def mm(lo, hi):
        return jnp.dot(h, w_ref[:, lo:hi], preferred_element_type=F32)

    w = FOX_WIDTH
    f = mm(8 * w, 8 * w + LANES) + bf_ref[...]
    lane = lax.broadcasted_iota(jnp.int32, f.shape, 1)
    row = lax.broadcasted_iota(jnp.int32, f.shape, 0)
    c = jnp.where(lane < HEADS, (jnp.minimum(f, 0.0) - jnp.log(1.0 + jnp.exp(-jnp.abs(f)))) * LOG2E, 0.0)
    sh = 1
    while sh < tm:
        c = c + jnp.where(row >= sh, pltpu.roll(c, sh, axis=0), 0.0)
        sh *= 2
    c = c + carry_scr[...]
    carry_scr[...] = c[tm - 1:tm, :]
    c1 = c.astype(BF16)
    r = c - c1.astype(F32)
    c2 = r.astype(BF16)
    c3 = (r - c2.astype(F32)).astype(BF16)

    def place(sel_ref):
        return (jnp.dot(c1, sel_ref[0], preferred_element_type=F32)
                + jnp.dot(c2, sel_ref[1], preferred_element_type=F32)
                + jnp.dot(c3, sel_ref[2], preferred_element_type=F32))

    aux_q = place(selq_ref) + auxc_ref[0:1, :]
    aux_k = place(selk_ref) + auxc_ref[1:2, :]
    aux_v = jnp.broadcast_to(auxc_ref[2:3, :], aux_q.shape)

    for out_ref, lo, aux in ((q_ref, 0, aux_q), (k_ref, w, aux_k), (v_ref, 2 * w, aux_v)):
        val = mm(lo, lo + w)
        for hd in range(HEADS):
            blk = jnp.concatenate([val[:, hd * HEAD_DIM:(hd + 1) * HEAD_DIM],
                                   aux[:, hd * AUX:(hd + 1) * AUX]], axis=1)
            out_ref[:, hd * HEAD_BLOCK:(hd + 1) * HEAD_BLOCK] = blk.astype(BF16)

    u_ref[...] = mm(3 * w, 4 * w).astype(BF16)
    for cc in range(4):
        lo = 4 * w + cc * w
        gate_ref[:, cc * w:(cc + 1) * w] = jax.nn.sigmoid(
            mm(lo, lo + w) + bg_ref[:, cc * w:(cc + 1) * w]).astype(BF16)


def _proj(x2, g_mix, w1, bf_pad, b_gate, selq, selk, auxc, tm, seq):
    t = x2.shape[0]
    bsz = t // seq
    nt = seq // tm
    n1 = w1.shape[1]
    hb = HEADS * HEAD_BLOCK
    row = lambda b, i: (b * nt + i, 0)
    const = lambda b, i: (0, 0)
    const3 = lambda b, i: (0, 0, 0)
    return pl.pallas_call(
        _proj_kernel,
        grid=(bsz, nt),
        in_specs=[
            pl.BlockSpec((tm, D_MODEL), row),
            pl.BlockSpec((1, D_MODEL), const),
            pl.BlockSpec((D_MODEL, n1), const),
            pl.BlockSpec((1, LANES), const),
            pl.BlockSpec((1, 2 * D_MODEL), const),
            pl.BlockSpec((3, LANES, HEADS * AUX), const3),
            pl.BlockSpec((3, LANES, HEADS * AUX), const3),
            pl.BlockSpec((8, HEADS * AUX), const),
        ],
        out_specs=[
            pl.BlockSpec((tm, hb), row),
            pl.BlockSpec((tm, hb), row),
            pl.BlockSpec((tm, hb), row),
            pl.BlockSpec((tm, SSM_WIDTH), row),
            pl.BlockSpec((tm, 2 * D_MODEL), row),
        ],
        out_shape=[
            jax.ShapeDtypeStruct((t, hb), BF16),
            jax.ShapeDtypeStruct((t, hb), BF16),
            jax.ShapeDtypeStruct((t, hb), BF16),
            jax.ShapeDtypeStruct((t, SSM_WIDTH), BF16),
            jax.ShapeDtypeStruct((t, 2 * D_MODEL), BF16),
        ],
        scratch_shapes=[pltpu.VMEM((1, LANES), F32)],
        compiler_params=_cparams(("arbitrary", "arbitrary")),
        name="proj",
    )(x2, g_mix, w1, bf_pad, b_gate, selq, selk, auxc)


def _aux_constants():
    selq = jnp.zeros((3, LANES, HEADS * AUX), F32)
    selk = jnp.zeros((3, LANES, HEADS * AUX), F32)
    auxc = jnp.zeros((8, HEADS * AUX), F32)
    hd = jnp.arange(HEADS)
    for p in range(3):
        selq = selq.at[p, hd, hd * AUX + p].set(1.0)
        selk = selk.at[p, hd, hd * AUX + 3 + p].set(-1.0)
        auxc = auxc.at[0, hd * AUX + 3 + p].set(1.0)
        auxc = auxc.at[1, hd * AUX + p].set(1.0)
    auxc = auxc.at[2, hd * AUX].set(1.0)
    return selq.astype(BF16), selk.astype(BF16), auxc


def _attn_kernel(q_ref, k_ref, v_ref, o_ref, vt_scr, s_scr, m_scr, acc_scr, *, tile, heads):
    i = pl.program_id(2)
    n = k_ref.shape[0] // tile
    nt = (((1,), (1,)), ((), ()))

    @pl.when(i == 0)
    def _():
        for g in range(heads):
            def tr(c, carry, g=g):
                r0 = pl.multiple_of(c * tile, tile)
                blk = v_ref[pl.ds(r0, tile), g * HEAD_BLOCK:(g + 1) * HEAD_BLOCK].astype(F32)
                vt_scr[g, c] = blk.T[:V_ROWS].astype(BF16)
                return carry
            lax.fori_loop(0, n, tr, 0)

    m_scr[...] = jnp.full(m_scr.shape, -jnp.inf, F32)
    acc_scr[...] = jnp.zeros(acc_scr.shape, F32)

    def qk(slot, t):
        r0 = pl.multiple_of(t * tile, tile)
        for g in range(heads):
            s_scr[slot, g] = lax.dot_general(
                k_ref[pl.ds(r0, tile), g * HEAD_BLOCK:(g + 1) * HEAD_BLOCK],
                q_ref[:, g * HEAD_BLOCK:(g + 1) * HEAD_BLOCK], nt, preferred_element_type=F32)

    def softmax(slot, masked):
        ps, alphas = [], []
        for g in range(heads):
            s = s_scr[slot, g]
            if masked:
                kpos = lax.broadcasted_iota(jnp.int32, s.shape, 0)
                qpos = lax.broadcasted_iota(jnp.int32, s.shape, 1)
                s = jnp.where(kpos <= qpos, s, -jnp.inf)
            m = m_scr[g]
            m_new = jnp.maximum(m, jnp.max(s, axis=0, keepdims=True))
            ps.append(jnp.exp2(s - m_new).astype(BF16))
            alphas.append(jnp.exp2(m - m_new))
            m_scr[g] = m_new
        return ps, alphas

    def pv(pa, t):
        ps, alphas = pa
        for g in range(heads):
            acc_scr[g] = (alphas[g] * acc_scr[g]
                          + jnp.dot(vt_scr[g, t], ps[g], preferred_element_type=F32))

    qk(0, 0)

    def body(tt, carry):
        t0 = 2 * tt
        qk(1, t0 + 1)
        pa0 = softmax(0, False)
        qk(0, t0 + 2)
        pv(pa0, t0)
        pv(softmax(1, False), t0 + 1)
        return carry

    lax.fori_loop(0, lax.shift_right_logical(i, 1), body, 0)

    @pl.when((i & 1) == 0)
    def _():
        pv(softmax(0, True), i)

    @pl.when((i & 1) == 1)
    def _():
        qk(1, i)
        pv(softmax(0, False), i - 1)
        pv(softmax(1, True), i)

    for pair in range(heads // 2):
        a0 = acc_scr[2 * pair]
        a1 = acc_scr[2 * pair + 1]
        o2 = jnp.concatenate([a0[:HEAD_DIM] / a0[HEAD_DIM:HEAD_DIM + 1],
                              a1[:HEAD_DIM] / a1[HEAD_DIM:HEAD_DIM + 1]], axis=0)
        o_ref[:, pair * LANES:(pair + 1) * LANES] = o2.T.astype(BF16)


def _attn(qx, kx, vx, tile, heads):
    b, s, _ = qx.shape
    n = s // tile
    hb = heads * HEAD_BLOCK
    return pl.pallas_call(
        functools.partial(_attn_kernel, tile=tile, heads=heads),
        grid=(b, HEADS // heads, n),
        in_specs=[
            pl.BlockSpec((None, tile, hb), lambda bi, hi, i: (bi, i, hi)),
            pl.BlockSpec((None, s, hb), lambda bi, hi, i: (bi, 0, hi)),
            pl.BlockSpec((None, s, hb), lambda bi, hi, i: (bi, 0, hi)),
        ],
        out_specs=pl.BlockSpec((None, tile, heads * HEAD_DIM), lambda bi, hi, i: (bi, i, hi)),
        out_shape=jax.ShapeDtypeStruct((b, s, FOX_WIDTH), BF16),
        scratch_shapes=[
            pltpu.VMEM((heads, n, V_ROWS, tile), BF16),
            pltpu.VMEM((2, heads, tile, tile), F32),
            pltpu.VMEM((heads, 1, tile), F32),
            pltpu.VMEM((heads, V_ROWS, tile), F32),
        ],
        compiler_params=_cparams(("arbitrary", "arbitrary", "arbitrary")),
        name="attn",
    )(qx, kx, vx)


def _ssm_kernel(u_ref, bm_ref, cm_ref, ar_ref, ai_ref, d_ref, y_ref, io_scr, x_scr, s_scr, *, tb):
    rows = tb * 8
    nb = u_ref.shape[0]

    @pl.when(pl.program_id(0) == 0)
    def _():
        s_scr[...] = jnp.zeros_like(s_scr)

    for g in range(SUPER):
        for b in range(nb):
            ub = u_ref[b, :, g * SG_CH:(g + 1) * SG_CH].astype(F32)
            io_scr[g, pl.ds(b, tb, stride=8), :] = ub
            io_scr[g, pl.ds(b + nb, tb, stride=8), :] = ub
    is_re = (lax.broadcasted_iota(jnp.int32, (rows, SG_CH), 0) % 8) < 4
    zero = jnp.zeros((rows, SG_CH), BF16)
    for g in range(SUPER):
        ug = io_scr[g].astype(BF16)
        lhs = jnp.concatenate([jnp.where(is_re, ug, zero), jnp.where(is_re, zero, ug)], axis=1)
        x_scr[:, g * SG_ST:(g + 1) * SG_ST] = jnp.dot(lhs, bm_ref[g], preferred_element_type=F32)

    half = 2 * SG_ST
    for c in range(SUPER * SG_ST // half):
        lo = c * half
        ar = ar_ref[:, lo:lo + half]
        ai = ai_ref[:, lo:lo + half]

        def body(t, s, lo=lo, ar=ar, ai=ai):
            r0 = pl.multiple_of(t * 8, 8)
            s = ar * s + ai * pltpu.roll(s, 4, axis=0) + x_scr[pl.ds(r0, 8), lo:lo + half]
            x_scr[pl.ds(r0, 8), lo:lo + half] = s
            return s

        s_scr[:, lo:lo + half] = lax.fori_loop(0, tb, body, s_scr[:, lo:lo + half], unroll=4)

    for g in range(SUPER):
        st = x_scr[:, g * SG_ST:(g + 1) * SG_ST].astype(BF16)
        o = jnp.dot(st, cm_ref[g], preferred_element_type=F32)
        y = o[:, :SG_CH] + pltpu.roll(o[:, SG_CH:], rows - 4, axis=0)
        io_scr[g] = y + d_ref[:, g * SG_CH:(g + 1) * SG_CH] * io_scr[g]
        for b in range(nb):
            y_ref[b, :, g * SG_CH:(g + 1) * SG_CH] = io_scr[g, pl.ds(b, tb, stride=8), :].astype(BF16)


def _ssm(u, bm, cm, ar, ai, d, tb):
    bsz, seq, _ = u.shape
    rows = tb * 8
    return pl.pallas_call(
        functools.partial(_ssm_kernel, tb=tb),
        grid=(seq // tb,),
        in_specs=[
            pl.BlockSpec((bsz, tb, SSM_WIDTH), lambda i: (0, i, 0)),
            pl.BlockSpec((SUPER, 2 * SG_CH, SG_ST), lambda i: (0, 0, 0)),
            pl.BlockSpec((SUPER, SG_ST, 2 * SG_CH), lambda i: (0, 0, 0)),
            pl.BlockSpec((8, SUPER * SG_ST), lambda i: (0, 0)),
            pl.BlockSpec((8, SUPER * SG_ST), lambda i: (0, 0)),
            pl.BlockSpec((1, SSM_WIDTH), lambda i: (0, 0)),
        ],
        out_specs=pl.BlockSpec((bsz, tb, SSM_WIDTH), lambda i: (0, i, 0)),
        out_shape=jax.ShapeDtypeStruct((bsz, seq, SSM_WIDTH), BF16),
        scratch_shapes=[
            pltpu.VMEM((SUPER, rows, SG_CH), F32),
            pltpu.VMEM((rows, SUPER * SG_ST), F32),
            pltpu.VMEM((8, SUPER * SG_ST), F32),
        ],
        compiler_params=_cparams(("arbitrary",)),
        name="ssm",
    )(u, bm, cm, ar, ai, d)


def _merge_kernel(x_ref, o_ref, ys_ref, gate_ref, woa_ref, wglu_ref, bglu_ref, wob_ref, wout_ref,
                  x1_ref):
    y_a = jnp.dot(o_ref[...], woa_ref[...], preferred_element_type=F32)
    z = jax.nn.gelu(ys_ref[...].astype(F32))
    zg = jnp.dot(z.astype(BF16), wglu_ref[...], preferred_element_type=F32) + bglu_ref[...]
    z = z * jax.nn.sigmoid(zg)
    y_b = jnp.dot(z.astype(BF16), wob_ref[...], preferred_element_type=F32)
    g0 = gate_ref[:, :D_MODEL].astype(F32)
    g1 = gate_ref[:, D_MODEL:].astype(F32)
    merged = (g0 * y_a + g1 * y_b).astype(BF16)
    x1_ref[...] = x_ref[...] + jnp.dot(merged, wout_ref[...], preferred_element_type=F32)


def _merge(x2, o, ys, gates, woa, wglu, bglu, wob, wout, tm):
    t = x2.shape[0]
    row = lambda i: (i, 0)
    const = lambda i: (0, 0)
    return pl.pallas_call(
        _merge_kernel,
        grid=(t // tm,),
        in_specs=[
            pl.BlockSpec((tm, D_MODEL), row),
            pl.BlockSpec((tm, FOX_WIDTH), row),
            pl.BlockSpec((tm, SSM_WIDTH), row),
            pl.BlockSpec((tm, 2 * D_MODEL), row),
            pl.BlockSpec((FOX_WIDTH, D_MODEL), const),
            pl.BlockSpec((SSM_WIDTH, SSM_WIDTH), const),
            pl.BlockSpec((1, SSM_WIDTH), const),
            pl.BlockSpec((SSM_WIDTH, D_MODEL), const),
            pl.BlockSpec((D_MODEL, D_MODEL), const),
        ],
        out_specs=pl.BlockSpec((tm, D_MODEL), row),
        out_shape=jax.ShapeDtypeStruct((t, D_MODEL), F32),
        compiler_params=_cparams(("arbitrary",)),
        name="merge",
    )(x2, o, ys, gates, woa, wglu, bglu, wob, wout)


def _route(logits):
    lane = lax.broadcasted_iota(jnp.int32, logits.shape, 1)
    neg = -jnp.inf
    big = jnp.int32(ROUTER_COLS)
    is_g = lane < N_GROUPS
    lg = jnp.where(is_g, logits, neg)
    gmax = jnp.max(lg, axis=1, keepdims=True)
    p_sel = 1.0 / jnp.sum(jnp.exp(lg - gmax), axis=1, keepdims=True)
    g_idx = jnp.min(jnp.where(lg == gmax, lane, big), axis=1, keepdims=True)
    lo = N_GROUPS + g_idx * EXPERTS_PER_GROUP
    in_grp = (lane >= lo) & (lane < lo + EXPERTS_PER_GROUP)
    le = jnp.where(in_grp, logits, neg)
    v1 = jnp.max(le, axis=1, keepdims=True)
    i1 = jnp.min(jnp.where(le == v1, lane, big), axis=1, keepdims=True)
    le2 = jnp.where(lane == i1, neg, le)
    v2 = jnp.max(le2, axis=1, keepdims=True)
    i2 = jnp.min(jnp.where(le2 == v2, lane, big), axis=1, keepdims=True)
    e2 = jnp.exp(v2 - v1)
    w1 = p_sel / (1.0 + e2)
    w2 = p_sel * e2 / (1.0 + e2)
    cw = jnp.where(lane == i1, w1, 0.0) + jnp.where(lane == i2, w2, 0.0)
    cw4 = jnp.zeros_like(cw)
    for grp in range(N_GROUPS):
        shift = N_GROUPS + grp * EXPERTS_PER_GROUP
        cw4 = cw4 + jnp.where(g_idx == grp, pltpu.roll(cw, ROUTER_COLS - shift, axis=1), 0.0)
    onehot = jnp.where(lane == g_idx, 1.0, 0.0)
    return cw4, onehot


def _route_kernel(x1_ref, gf_ref, wr_ref, br_ref, ltri_ref, hs_ref, cws_ref, pos_ref, gid_ref):
    tmb = x1_ref.shape[0]
    mp = hs_ref.shape[0]
    h = _rms(x1_ref[...], gf_ref[...])
    h_hi = h.astype(BF16)
    h_lo = (h - h_hi.astype(F32)).astype(BF16)
    l_hi = jnp.dot(h_hi, jnp.concatenate([wr_ref[0], wr_ref[1]], axis=1), preferred_element_type=F32)
    logits = (l_hi[:, :ROUTER_COLS] + l_hi[:, ROUTER_COLS:]
              + jnp.dot(h_lo, wr_ref[0], preferred_element_type=F32)) + br_ref[...]
    cw4, onehot = _route(logits)

    lane1 = lax.broadcasted_iota(jnp.int32, (1, ROUTER_COLS), 1)
    counts = jnp.sum(onehot, axis=0, keepdims=True)
    nch = jnp.floor((counts + (MOE_CHUNK - 1)) * (1.0 / MOE_CHUNK))
    inc = nch
    for sh in (1, 2):
        inc = inc + jnp.where(lane1 >= sh, pltpu.roll(inc, sh, axis=1), 0.0)
    start = (inc - nch) * MOE_CHUNK
    rank = jnp.dot(ltri_ref[...], onehot.astype(BF16), preferred_element_type=F32)
    pos = jnp.sum(onehot * (start + rank), axis=1, keepdims=True)
    pos_rep = jnp.broadcast_to(pos, (tmb, ROUTER_COLS))
    pos_ref[...] = pos_rep
    pos_row = pos_rep.T[0:1, :]
    prow = lax.broadcasted_iota(jnp.int32, (mp, tmb), 0).astype(F32)
    perm = jnp.where(prow == pos_row, 1.0, 0.0).astype(BF16)
    hs_ref[...] = jnp.dot(perm, h_hi, preferred_element_type=F32).astype(BF16)
    cw_hi = cw4.astype(BF16)
    cw_lo = (cw4 - cw_hi.astype(F32)).astype(BF16)
    cw2 = jnp.dot(perm, jnp.concatenate([cw_hi, cw_lo], axis=1), preferred_element_type=F32)
    cws_ref[...] = cw2[:, :ROUTER_COLS] + cw2[:, ROUTER_COLS:]
    chunk = lax.broadcasted_iota(jnp.int32, (8, ROUTER_COLS), 1).astype(F32)
    gid = jnp.zeros((8, ROUTER_COLS), F32)
    for grp in range(N_GROUPS):
        end = jnp.sum(jnp.where(lane1 == grp, inc, 0.0), axis=1, keepdims=True)
        gid = gid + jnp.where(chunk >= end, 1.0, 0.0)
    gid_ref[...] = gid.astype(jnp.int32)


def _route_call(x1, g_ffn, wr, br, ltri, tmb, mp):
    t = x1.shape[0]
    nb = t // tmb
    return pl.pallas_call(
        _route_kernel,
        grid=(nb,),
        in_specs=[
            pl.BlockSpec((tmb, D_MODEL), lambda i: (i, 0)),
            pl.BlockSpec((1, D_MODEL), lambda i: (0, 0)),
            pl.BlockSpec((2, D_MODEL, ROUTER_COLS), lambda i: (0, 0, 0)),
            pl.BlockSpec((1, ROUTER_COLS), lambda i: (0, 0)),
            pl.BlockSpec((tmb, tmb), lambda i: (0, 0)),
        ],
        out_specs=[
            pl.BlockSpec((mp, D_MODEL), lambda i: (i, 0)),
            pl.BlockSpec((mp, ROUTER_COLS), lambda i: (i, 0)),
            pl.BlockSpec((tmb, ROUTER_COLS), lambda i: (i, 0)),
            pl.BlockSpec((None, 8, ROUTER_COLS), lambda i: (i, 0, 0)),
        ],
        out_shape=[
            jax.ShapeDtypeStruct((nb * mp, D_MODEL), BF16),
            jax.ShapeDtypeStruct((nb * mp, ROUTER_COLS), F32),
            jax.ShapeDtypeStruct((t, ROUTER_COLS), F32),
            jax.ShapeDtypeStruct((nb, 8, ROUTER_COLS), jnp.int32),
        ],
        compiler_params=_cparams(("arbitrary",)),
        name="route",
    )(x1, g_ffn, wr, br, ltri)


def _experts_kernel(order_ref, gids_ref, hs_ref, cws_ref, wg_ref, wu_ref, wd_ref, exp_ref, ys_ref):
    valid = gids_ref[pl.program_id(0)] < N_GROUPS

    @pl.when(valid)
    def _():
        x = hs_ref[...]
        hg = jnp.concatenate([jnp.dot(x, wg_ref[j], preferred_element_type=F32)
                              for j in range(EXPERTS_PER_GROUP)], axis=1)
        hu = jnp.concatenate([jnp.dot(x, wu_ref[j], preferred_element_type=F32)
                              for j in range(EXPERTS_PER_GROUP)], axis=1)
        cw = cws_ref[...]
        cw_hi = cw.astype(BF16)
        cw_lo = (cw - cw_hi.astype(F32)).astype(BF16)
        cwx = (jnp.dot(cw_hi, exp_ref[...], preferred_element_type=F32)
               + jnp.dot(cw_lo, exp_ref[...], preferred_element_type=F32))
        act = (hg * jax.nn.sigmoid(hg) * hu * cwx).astype(BF16)
        ys_ref[...] = jnp.dot(act, wd_ref[...], preferred_element_type=F32).astype(BF16)

    @pl.when(jnp.logical_not(valid))
    def _():
        ys_ref[...] = jnp.zeros(ys_ref.shape, BF16)


def _experts_call(order, gids, hs, cws, wg, wu, wd, expand):
    n_chunks = order.shape[0]
    ff = EXPERTS_PER_GROUP * EXPERT_FF
    grp = lambda s, order_ref, gids_ref: (jnp.minimum(gids_ref[s], N_GROUPS - 1), 0, 0)
    rows = lambda s, order_ref, gids_ref: (order_ref[s], 0)
    return pl.pallas_call(
        _experts_kernel,
        grid_spec=pltpu.PrefetchScalarGridSpec(
            num_scalar_prefetch=2,
            grid=(n_chunks,),
            in_specs=[
                pl.BlockSpec((MOE_CHUNK, D_MODEL), rows),
                pl.BlockSpec((MOE_CHUNK, ROUTER_COLS), rows),
                pl.BlockSpec((EXPERTS_PER_GROUP, D_MODEL, EXPERT_FF), grp),
                pl.BlockSpec((EXPERTS_PER_GROUP, D_MODEL, EXPERT_FF), grp),
                pl.BlockSpec((None, ff, D_MODEL), grp),
                pl.BlockSpec((ROUTER_COLS, ff), lambda s, order_ref, gids_ref: (0, 0)),
            ],
            out_specs=pl.BlockSpec((MOE_CHUNK, D_MODEL), rows),
        ),
        out_shape=jax.ShapeDtypeStruct(hs.shape, BF16),
        compiler_params=_cparams(("arbitrary",)),
        name="experts",
    )(order, gids, hs, cws, wg, wu, wd, expand)


def _combine_kernel(x1_ref, ys_ref, pos_ref, gfin_ref, out_ref):
    tmb = x1_ref.shape[0]
    mp = ys_ref.shape[0]
    pos = pos_ref[...]
    lane = lax.broadcasted_iota(jnp.int32, pos.shape, 1).astype(F32)
    del tmb
    sel = jnp.concatenate(
        [jnp.where(pos == lane + float(c * ROUTER_COLS), 1.0, 0.0).astype(BF16)
         for c in range(mp // ROUTER_COLS)], axis=1)
    moe = jnp.dot(sel, ys_ref[...], preferred_element_type=F32)
    out_ref[...] = _rms(x1_ref[...] + moe, gfin_ref[...])


def _combine_call(x1, ys, pos, g_final, tmb, mp):
    t = x1.shape[0]
    return pl.pallas_call(
        _combine_kernel,
        grid=(t // tmb,),
        in_specs=[
            pl.BlockSpec((tmb, D_MODEL), lambda i: (i, 0)),
            pl.BlockSpec((mp, D_MODEL), lambda i: (i, 0)),
            pl.BlockSpec((tmb, ROUTER_COLS), lambda i: (i, 0)),
            pl.BlockSpec((1, D_MODEL), lambda i: (0, 0)),
        ],
        out_specs=pl.BlockSpec((tmb, D_MODEL), lambda i: (i, 0)),
        out_shape=jax.ShapeDtypeStruct((t, D_MODEL), F32),
        compiler_params=_cparams(("arbitrary",)),
        name="combine",
    )(x1, ys, pos, g_final)


def _ssm_params(lambda_re, lambda_im, log_step, b_re, b_im, c_re, c_im):
    dt = jnp.exp(log_step)[:, None]
    mag = jnp.exp(lambda_re * dt)
    a_re = mag * jnp.cos(lambda_im * dt)
    a_im = mag * jnp.sin(lambda_im * dt)
    den = lambda_re * lambda_re + lambda_im * lambda_im
    num_re = a_re - 1.0
    z_re = (num_re * lambda_re + a_im * lambda_im) / den
    z_im = (a_im * lambda_re - num_re * lambda_im) / den
    bb_re = z_re[..., None] * b_re - z_im[..., None] * b_im
    bb_im = z_re[..., None] * b_im + z_im[..., None] * b_re
    gl = SSM_GROUPS // SUPER
    eye = jnp.eye(gl, dtype=F32)

    def in_blk(bb):
        t = bb.reshape(SUPER, gl, SSM_STATE, SSM_GROUP).transpose(0, 1, 3, 2)
        t = t[:, :, :, None, :] * eye[None, :, None, :, None]
        return t.reshape(SUPER, gl * SSM_GROUP, gl * SSM_STATE)

    def out_blk(cc):
        t = cc.reshape(SUPER, gl, SSM_GROUP, SSM_STATE).transpose(0, 1, 3, 2)
        t = t[:, :, :, None, :] * eye[None, :, None, :, None]
        return t.reshape(SUPER, gl * SSM_STATE, gl * SSM_GROUP)

    bm = jnp.concatenate([in_blk(bb_re), in_blk(bb_im)], axis=1).astype(BF16)
    cm = jnp.concatenate([out_blk(c_re), out_blk(-c_im)], axis=2).astype(BF16)
    ar_row = a_re.reshape(1, SSM_GROUPS * SSM_STATE)
    ai_row = a_im.reshape(1, SSM_GROUPS * SSM_STATE)
    ar = jnp.broadcast_to(ar_row, (8, SSM_GROUPS * SSM_STATE))
    ai = jnp.concatenate([jnp.broadcast_to(-ai_row, (4, ai_row.shape[1])),
                          jnp.broadcast_to(ai_row, (4, ai_row.shape[1]))], axis=0)
    return bm, cm, ar, ai


def _pick(n, pref):
    return pref if n % pref == 0 else n


def _mixers(x, g_mix, w_in, b_forget, b_gate, w_out_a, lambda_re, lambda_im, log_step,
            ssm_b_re, ssm_b_im, ssm_c_re, ssm_c_im, ssm_d, w_glu, b_glu, w_out_b, w_out):
    bsz, seq, _ = x.shape
    assert bsz == 4, "the SSM kernel packs (re/im) x 4 batches onto the 8 sublanes"
    t = bsz * seq
    layer = 0
    x2 = x.reshape(t, D_MODEL)

    w = w_in[layer]
    fw = FOX_WIDTH
    wq, wk, wv = w[:, :fw], w[:, fw:2 * fw], w[:, 2 * fw:3 * fw]
    wf = w[:, 3 * fw:3 * fw + HEADS]
    wu = w[:, 3 * fw + HEADS:3 * fw + HEADS + SSM_WIDTH]
    wgt = w[:, 3 * fw + HEADS + SSM_WIDTH:]
    scale = LOG2E / math.sqrt(HEAD_DIM)
    w1 = jnp.concatenate(
        [wq * scale, wk, wv, wu, wgt, jnp.pad(wf, ((0, 0), (0, LANES - HEADS)))], axis=1).astype(BF16)
    bf_pad = jnp.pad(b_forget[layer], (0, LANES - HEADS)).reshape(1, LANES)
    selq, selk, auxc = _aux_constants()

    tm = _pick(seq, 512)
    qx, kx, vx, u, gates = _proj(x2, g_mix[layer].reshape(1, D_MODEL), w1, bf_pad,
                                 b_gate[layer].reshape(1, 2 * D_MODEL), selq, selk, auxc, tm, seq)

    hb = HEADS * HEAD_BLOCK
    tile = _pick(seq, ATTN_TILE)
    o = _attn(qx.reshape(bsz, seq, hb), kx.reshape(bsz, seq, hb), vx.reshape(bsz, seq, hb),
              tile, ATTN_HEADS).reshape(t, FOX_WIDTH)

    bm, cm, ar, ai = _ssm_params(lambda_re[layer], lambda_im[layer], log_step[layer],
                                 ssm_b_re[layer], ssm_b_im[layer], ssm_c_re[layer], ssm_c_im[layer])
    tb = _pick(seq, 128)
    ys = _ssm(u.reshape(bsz, seq, SSM_WIDTH), bm, cm, ar, ai, ssm_d[layer].reshape(1, SSM_WIDTH),
              tb).reshape(t, SSM_WIDTH)

    x1 = _merge(x2, o, ys, gates, w_out_a[layer].astype(BF16), w_glu[layer].astype(BF16),
                b_glu[layer].reshape(1, SSM_WIDTH), w_out_b[layer].astype(BF16),
                w_out[layer].astype(BF16), tm)
    return x1, o, ys


def kernel(x, g_mix, w_in, b_forget, b_gate, w_out_a, lambda_re, lambda_im, log_step, ssm_b_re, ssm_b_im, ssm_c_re, ssm_c_im, ssm_d, w_glu, b_glu, w_out_b, w_out, g_ffn, w_router_group, b_router_group, w_router_expert, b_router_expert, w_exp_gate, w_exp_up, w_exp_down, g_final):
    bsz, seq, _ = x.shape
    t = bsz * seq
    layer = 0
    x1, _, _ = _mixers(x, g_mix, w_in, b_forget, b_gate, w_out_a, lambda_re, lambda_im, log_step,
                       ssm_b_re, ssm_b_im, ssm_c_re, ssm_c_im, ssm_d, w_glu, b_glu, w_out_b, w_out)

    wr_f = jnp.concatenate([w_router_group[layer], w_router_expert[layer]], axis=1)
    wr_f = jnp.pad(wr_f, ((0, 0), (0, ROUTER_COLS - N_GROUPS - N_EXPERTS)))
    wr_hi = wr_f.astype(BF16)
    wr_lo = (wr_f - wr_hi.astype(F32)).astype(BF16)
    wr = jnp.stack([wr_hi, wr_lo])
    br = jnp.pad(jnp.concatenate([b_router_group[layer], b_router_expert[layer]]),
                 (0, ROUTER_COLS - N_GROUPS - N_EXPERTS)).reshape(1, ROUTER_COLS)
    ff = EXPERTS_PER_GROUP * EXPERT_FF
    wd = w_exp_down[layer].reshape(N_GROUPS, ff, D_MODEL).astype(BF16)
    expand = (jnp.arange(ROUTER_COLS)[:, None] == jnp.arange(ff)[None, :] // EXPERT_FF).astype(BF16)

    tmb = _pick(t, MOE_BLOCK)
    mp = tmb + N_GROUPS * MOE_CHUNK
    ltri = jnp.tril(jnp.ones((tmb, tmb), BF16), -1)
    hs, cws, pos, gid = _route_call(x1, g_ffn[layer].reshape(1, D_MODEL), wr, br, ltri, tmb, mp)

    cpb = mp // MOE_CHUNK
    gid_flat = gid[:, 0, :cpb].reshape(-1)
    n_chunks = gid_flat.shape[0]
    key = jnp.sort(gid_flat * n_chunks + jnp.arange(n_chunks, dtype=jnp.int32))
    ys = _experts_call(key % n_chunks, key // n_chunks, hs, cws, w_exp_gate[layer].astype(BF16),
                       w_exp_up[layer].astype(BF16), wd, expand)
    out = _combine_call(x1, ys, pos, g_final.reshape(1, D_MODEL), tmb, mp)
    return out.reshape(bsz, seq, D_MODEL)
```

```python
import functools
import math

import jax
import jax.numpy as jnp
from jax import lax
from jax.experimental import pallas as pl
from jax.experimental.pallas import tpu as pltpu

D_MODEL = 1024
HEADS = 8
HEAD_DIM = 64
FOX_WIDTH = HEADS * HEAD_DIM
SSM_WIDTH = 512
SSM_GROUP = 16
SSM_GROUPS = 32
SSM_STATE = 64
N_GROUPS = 4
EXPERTS_PER_GROUP = 4
N_EXPERTS = 16
EXPERT_FF = 256
EPS = 1e-6

LANES = 128
HEAD_BLOCK = LANES
AUX = HEAD_BLOCK - HEAD_DIM
V_ROWS = HEAD_DIM + 16
SUPER = 4
SG_CH = SSM_WIDTH // SUPER
SG_ST = SSM_GROUPS // SUPER * SSM_STATE
ROUTER_COLS = LANES

F32 = jnp.float32
BF16 = jnp.bfloat16
VMEM_LIMIT = 56 * 1024 * 1024
LOG2E = 1.4426950408889634
ATTN_HEADS = 4
ATTN_TILE = 512
MOE_BLOCK = 1024
MOE_CHUNK = 128


def _cparams(sem):
    return pltpu.CompilerParams(dimension_semantics=sem, vmem_limit_bytes=VMEM_LIMIT)


def _rms(x, g):
    return x * lax.rsqrt(jnp.mean(x * x, axis=-1, keepdims=True) + EPS) * g


def _proj_kernel(x_ref, g_ref, w_ref, bf_ref, bg_ref, selq_ref, selk_ref, auxc_ref,
                 q_ref, k_ref, v_ref, u_ref, gate_ref, carry_scr):
    tm = x_ref.shape[0]

    @pl.when(pl.program_id(1) == 0)
    def _():
        carry_scr[...] = jnp.zeros_like(carry_scr)

    h = _rms(x_ref[...], g_ref[...]).astype(BF16)

    def mm(lo, hi):
        return jnp.dot(h, w_ref[:, lo:hi], preferred_element_type=F32)

    w = FOX_WIDTH
    f = mm(8 * w, 8 * w + LANES) + bf_ref[...]
    lane = lax.broadcasted_iota(jnp.int32, f.shape, 1)
    row = lax.broadcasted_iota(jnp.int32, f.shape, 0)
    c = jnp.where(lane < HEADS, (jnp.minimum(f, 0.0) - jnp.log(1.0 + jnp.exp(-jnp.abs(f)))) * LOG2E, 0.0)
    sh = 1
    while sh < tm:
        c = c + jnp.where(row >= sh, pltpu.roll(c, sh, axis=0), 0.0)
        sh *= 2
    c = c + carry_scr[...]
    carry_scr[...] = c[tm - 1:tm, :]
    c1 = c.astype(BF16)
    r = c - c1.astype(F32)
    c2 = r.astype(BF16)
    c3 = (r - c2.astype(F32)).astype(BF16)

    def place(sel_ref):
        return (jnp.dot(c1, sel_ref[0], preferred_element_type=F32)
                + jnp.dot(c2, sel_ref[1], preferred_element_type=F32)
                + jnp.dot(c3, sel_ref[2], preferred_element_type=F32))

    aux_q = place(selq_ref) + auxc_ref[0:1, :]
    aux_k = place(selk_ref) + auxc_ref[1:2, :]
    aux_v = jnp.broadcast_to(auxc_ref[2:3, :], aux_q.shape)

    for out_ref, lo, aux in ((q_ref, 0, aux_q), (k_ref, w, aux_k), (v_ref, 2 * w, aux_v)):
        val = mm(lo, lo + w)
        for hd in range(HEADS):
            blk = jnp.concatenate([val[:, hd * HEAD_DIM:(hd + 1) * HEAD_DIM],
                                   aux[:, hd * AUX:(hd + 1) * AUX]], axis=1)
            if out_ref is v_ref:
                out_ref[hd * V_ROWS:(hd + 1) * V_ROWS, :] = blk.T[:V_ROWS].astype(BF16)
            else:
                out_ref[:, hd * HEAD_BLOCK:(hd + 1) * HEAD_BLOCK] = blk.astype(BF16)

    u_ref[...] = mm(3 * w, 4 * w).astype(BF16)
    for cc in range(4):
        lo = 4 * w + cc * w
        gate_ref[:, cc * w:(cc + 1) * w] = jax.nn.sigmoid(
            mm(lo, lo + w) + bg_ref[:, cc * w:(cc + 1) * w]).astype(BF16)


def _proj(x2, g_mix, w1, bf_pad, b_gate, selq, selk, auxc, tm, seq):
    t = x2.shape[0]
    bsz = t // seq
    nt = seq // tm
    n1 = w1.shape[1]
    hb = HEADS * HEAD_BLOCK
    row = lambda b, i: (b * nt + i, 0)
    const = lambda b, i: (0, 0)
    const3 = lambda b, i: (0, 0, 0)
    return pl.pallas_call(
        _proj_kernel,
        grid=(bsz, nt),
        in_specs=[
            pl.BlockSpec((tm, D_MODEL), row),
            pl.BlockSpec((1, D_MODEL), const),
            pl.BlockSpec((D_MODEL, n1), const),
            pl.BlockSpec((1, LANES), const),
            pl.BlockSpec((1, 2 * D_MODEL), const),
            pl.BlockSpec((3, LANES, HEADS * AUX), const3),
            pl.BlockSpec((3, LANES, HEADS * AUX), const3),
            pl.BlockSpec((8, HEADS * AUX), const),
        ],
        out_specs=[
            pl.BlockSpec((tm, hb), row),
            pl.BlockSpec((tm, hb), row),
            pl.BlockSpec((None, None, HEADS * V_ROWS, tm), lambda b, i: (b, i, 0, 0)),
            pl.BlockSpec((tm, SSM_WIDTH), row),
            pl.BlockSpec((tm, 2 * D_MODEL), row),
        ],
        out_shape=[
            jax.ShapeDtypeStruct((t, hb), BF16),
            jax.ShapeDtypeStruct((t, hb), BF16),
            jax.ShapeDtypeStruct((bsz, nt, HEADS * V_ROWS, tm), BF16),
            jax.ShapeDtypeStruct((t, SSM_WIDTH), BF16),
            jax.ShapeDtypeStruct((t, 2 * D_MODEL), BF16),
        ],
        scratch_shapes=[pltpu.VMEM((1, LANES), F32)],
        compiler_params=_cparams(("arbitrary", "arbitrary")),
        name="proj",
    )(x2, g_mix, w1, bf_pad, b_gate, selq, selk, auxc)


def _aux_constants():
    selq = jnp.zeros((3, LANES, HEADS * AUX), F32)
    selk = jnp.zeros((3, LANES, HEADS * AUX), F32)
    auxc = jnp.zeros((8, HEADS * AUX), F32)
    hd = jnp.arange(HEADS)
    for p in range(3):
        selq = selq.at[p, hd, hd * AUX + p].set(1.0)
        selk = selk.at[p, hd, hd * AUX + 3 + p].set(-1.0)
        auxc = auxc.at[0, hd * AUX + 3 + p].set(1.0)
        auxc = auxc.at[1, hd * AUX + p].set(1.0)
    auxc = auxc.at[2, hd * AUX].set(1.0)
    return selq.astype(BF16), selk.astype(BF16), auxc


def _attn_kernel(q_ref, k_ref, vt_ref, o_ref, s_scr, m_scr, acc_scr, *, tile, heads):
    i = pl.program_id(2)
    nt = (((1,), (1,)), ((), ()))

    m_scr[...] = jnp.full(m_scr.shape, -jnp.inf, F32)
    acc_scr[...] = jnp.zeros(acc_scr.shape, F32)

    def qk(slot, t):
        r0 = pl.multiple_of(t * tile, tile)
        for g in range(heads):
            s_scr[slot, g] = lax.dot_general(
                k_ref[pl.ds(r0, tile), g * HEAD_BLOCK:(g + 1) * HEAD_BLOCK],
                q_ref[:, g * HEAD_BLOCK:(g + 1) * HEAD_BLOCK], nt, preferred_element_type=F32)

    def softmax(slot, masked):
        ps, alphas = [], []
        for g in range(heads):
            s = s_scr[slot, g]
            if masked:
                kpos = lax.broadcasted_iota(jnp.int32, s.shape, 0)
                qpos = lax.broadcasted_iota(jnp.int32, s.shape, 1)
                s = jnp.where(kpos <= qpos, s, -jnp.inf)
            m = m_scr[g]
            m_new = jnp.maximum(m, jnp.max(s, axis=0, keepdims=True))
            ps.append(jnp.exp2(s - m_new).astype(BF16))
            alphas.append(jnp.exp2(m - m_new))
            m_scr[g] = m_new
        return ps, alphas

    def pv(pa, t):
        ps, alphas = pa
        for g in range(heads):
            acc_scr[g] = (alphas[g] * acc_scr[g]
                          + jnp.dot(vt_ref[t, g * V_ROWS:(g + 1) * V_ROWS, :], ps[g],
                                    preferred_element_type=F32))

    qk(0, 0)

    def body(tt, carry):
        t0 = 2 * tt
        qk(1, t0 + 1)
        pa0 = softmax(0, False)
        qk(0, t0 + 2)
        pv(pa0, t0)
        pv(softmax(1, False), t0 + 1)
        return carry

    lax.fori_loop(0, lax.shift_right_logical(i, 1), body, 0)

    @pl.when((i & 1) == 0)
    def _():
        pv(softmax(0, True), i)

    @pl.when((i & 1) == 1)
    def _():
        qk(1, i)
        pv(softmax(0, False), i - 1)
        pv(softmax(1, True), i)

    for pair in range(heads // 2):
        a0 = acc_scr[2 * pair]
        a1 = acc_scr[2 * pair + 1]
        o2 = jnp.concatenate([a0[:HEAD_DIM] / a0[HEAD_DIM:HEAD_DIM + 1],
                              a1[:HEAD_DIM] / a1[HEAD_DIM:HEAD_DIM + 1]], axis=0)
        o_ref[:, pair * LANES:(pair + 1) * LANES] = o2.T.astype(BF16)


def _attn(qx, kx, vt, tile, heads):
    b, s, _ = qx.shape
    n = s // tile
    assert vt.shape == (b, n, HEADS * V_ROWS, tile)
    hb = heads * HEAD_BLOCK
    return pl.pallas_call(
        functools.partial(_attn_kernel, tile=tile, heads=heads),
        grid=(b, HEADS // heads, n),
        in_specs=[
            pl.BlockSpec((None, tile, hb), lambda bi, hi, i: (bi, i, hi)),
            pl.BlockSpec((None, s, hb), lambda bi, hi, i: (bi, 0, hi)),
            pl.BlockSpec((None, n, heads * V_ROWS, tile), lambda bi, hi, i: (bi, 0, hi, 0)),
        ],
        out_specs=pl.BlockSpec((None, tile, heads * HEAD_DIM), lambda bi, hi, i: (bi, i, hi)),
        out_shape=jax.ShapeDtypeStruct((b, s, FOX_WIDTH), BF16),
        scratch_shapes=[
            pltpu.VMEM((2, heads, tile, tile), F32),
            pltpu.VMEM((heads, 1, tile), F32),
            pltpu.VMEM((heads, V_ROWS, tile), F32),
        ],
        compiler_params=_cparams(("arbitrary", "arbitrary", "arbitrary")),
        name="attn",
    )(qx, kx, vt)


def _ssm_kernel(u_ref, bm_ref, cm_ref, ar_ref, ai_ref, d_ref, y_ref, io_scr, x_scr, s_scr, *, tb):
    rows = tb * 8
    nb = u_ref.shape[0]

    @pl.when(pl.program_id(0) == 0)
    def _():
        s_scr[...] = jnp.zeros_like(s_scr)

    for g in range(SUPER):
        for b in range(nb):
            ub = u_ref[b, :, g * SG_CH:(g + 1) * SG_CH].astype(F32)
            io_scr[g, pl.ds(b, tb, stride=8), :] = ub
            io_scr[g, pl.ds(b + nb, tb, stride=8), :] = ub
    is_re = (lax.broadcasted_iota(jnp.int32, (rows, SG_CH), 0) % 8) < 4
    zero = jnp.zeros((rows, SG_CH), BF16)
    for g in range(SUPER):
        ug = io_scr[g].astype(BF16)
        lhs = jnp.concatenate([jnp.where(is_re, ug, zero), jnp.where(is_re, zero, ug)], axis=1)
        x_scr[:, g * SG_ST:(g + 1) * SG_ST] = jnp.dot(lhs, bm_ref[g], preferred_element_type=F32)

    half = 2 * SG_ST
    for c in range(SUPER * SG_ST // half):
        lo = c * half
        ar = ar_ref[:, lo:lo + half]
        ai = ai_ref[:, lo:lo + half]

        def body(t, s, lo=lo, ar=ar, ai=ai):
            r0 = pl.multiple_of(t * 8, 8)
            s = ar * s + ai * pltpu.roll(s, 4, axis=0) + x_scr[pl.ds(r0, 8), lo:lo + half]
            x_scr[pl.ds(r0, 8), lo:lo + half] = s
            return s

        s_scr[:, lo:lo + half] = lax.fori_loop(0, tb, body, s_scr[:, lo:lo + half], unroll=4)

    for g in range(SUPER):
        st = x_scr[:, g * SG_ST:(g + 1) * SG_ST].astype(BF16)
        o = jnp.dot(st, cm_ref[g], preferred_element_type=F32)
        y = o[:, :SG_CH] + pltpu.roll(o[:, SG_CH:], rows - 4, axis=0)
        io_scr[g] = y + d_ref[:, g * SG_CH:(g + 1) * SG_CH] * io_scr[g]
        for b in range(nb):
            y_ref[b, :, g * SG_CH:(g + 1) * SG_CH] = io_scr[g, pl.ds(b, tb, stride=8), :].astype(BF16)


def _ssm(u, bm, cm, ar, ai, d, tb):
    bsz, seq, _ = u.shape
    rows = tb * 8
    return pl.pallas_call(
        functools.partial(_ssm_kernel, tb=tb),
        grid=(seq // tb,),
        in_specs=[
            pl.BlockSpec((bsz, tb, SSM_WIDTH), lambda i: (0, i, 0)),
            pl.BlockSpec((SUPER, 2 * SG_CH, SG_ST), lambda i: (0, 0, 0)),
            pl.BlockSpec((SUPER, SG_ST, 2 * SG_CH), lambda i: (0, 0, 0)),
            pl.BlockSpec((8, SUPER * SG_ST), lambda i: (0, 0)),
            pl.BlockSpec((8, SUPER * SG_ST), lambda i: (0, 0)),
            pl.BlockSpec((1, SSM_WIDTH), lambda i: (0, 0)),
        ],
        out_specs=pl.BlockSpec((bsz, tb, SSM_WIDTH), lambda i: (0, i, 0)),
        out_shape=jax.ShapeDtypeStruct((bsz, seq, SSM_WIDTH), BF16),
        scratch_shapes=[
            pltpu.VMEM((SUPER, rows, SG_CH), F32),
            pltpu.VMEM((rows, SUPER * SG_ST), F32),
            pltpu.VMEM((8, SUPER * SG_ST), F32),
        ],
        compiler_params=_cparams(("arbitrary",)),
        name="ssm",
    )(u, bm, cm, ar, ai, d)


def _merge_kernel(x_ref, o_ref, ys_ref, gate_ref, woa_ref, wglu_ref, bglu_ref, wob_ref, wout_ref,
                  x1_ref):
    y_a = jnp.dot(o_ref[...], woa_ref[...], preferred_element_type=F32)
    z = jax.nn.gelu(ys_ref[...].astype(F32))
    zg = jnp.dot(z.astype(BF16), wglu_ref[...], preferred_element_type=F32) + bglu_ref[...]
    z = z * jax.nn.sigmoid(zg)
    y_b = jnp.dot(z.astype(BF16), wob_ref[...], preferred_element_type=F32)
    g0 = gate_ref[:, :D_MODEL].astype(F32)
    g1 = gate_ref[:, D_MODEL:].astype(F32)
    merged = (g0 * y_a + g1 * y_b).astype(BF16)
    x1_ref[...] = x_ref[...] + jnp.dot(merged, wout_ref[...], preferred_element_type=F32)


def _merge(x2, o, ys, gates, woa, wglu, bglu, wob, wout, tm):
    t = x2.shape[0]
    row = lambda i: (i, 0)
    const = lambda i: (0, 0)
    return pl.pallas_call(
        _merge_kernel,
        grid=(t // tm,),
        in_specs=[
            pl.BlockSpec((tm, D_MODEL), row),
            pl.BlockSpec((tm, FOX_WIDTH), row),
            pl.BlockSpec((tm, SSM_WIDTH), row),
            pl.BlockSpec((tm, 2 * D_MODEL), row),
            pl.BlockSpec((FOX_WIDTH, D_MODEL), const),
            pl.BlockSpec((SSM_WIDTH, SSM_WIDTH), const),
            pl.BlockSpec((1, SSM_WIDTH), const),
            pl.BlockSpec((SSM_WIDTH, D_MODEL), const),
            pl.BlockSpec((D_MODEL, D_MODEL), const),
        ],
        out_specs=pl.BlockSpec((tm, D_MODEL), row),
        out_shape=jax.ShapeDtypeStruct((t, D_MODEL), F32),
        compiler_params=_cparams(("arbitrary",)),
        name="merge",
    )(x2, o, ys, gates, woa, wglu, bglu, wob, wout)


def _route(lt):
    ri = lax.broadcasted_iota(jnp.int32, lt.shape, 0)
    r8 = lax.broadcasted_iota(jnp.int32, (8, lt.shape[1]), 0)
    neg = -jnp.inf
    big = jnp.int32(ROUTER_COLS)
    lg = jnp.where(ri < N_GROUPS, lt, neg)
    gmax = jnp.max(lg, axis=0, keepdims=True)
    p_sel = 1.0 / jnp.sum(jnp.exp(lg - gmax), axis=0, keepdims=True)
    g_idx = jnp.min(jnp.where(lg == gmax, ri, big), axis=0, keepdims=True)
    lo = N_GROUPS + g_idx * EXPERTS_PER_GROUP
    le = jnp.where((ri >= lo) & (ri < lo + EXPERTS_PER_GROUP), lt, neg)
    v1 = jnp.max(le, axis=0, keepdims=True)
    i1 = jnp.min(jnp.where(le == v1, ri, big), axis=0, keepdims=True)
    le2 = jnp.where(ri == i1, neg, le)
    v2 = jnp.max(le2, axis=0, keepdims=True)
    i2 = jnp.min(jnp.where(le2 == v2, ri, big), axis=0, keepdims=True)
    e2 = jnp.exp(v2 - v1)
    w1 = p_sel / (1.0 + e2)
    w2 = p_sel * e2 / (1.0 + e2)
    cw = jnp.where(r8 == i1 - lo, w1, 0.0) + jnp.where(r8 == i2 - lo, w2, 0.0)
    onehot = jnp.where(r8 == g_idx, 1.0, 0.0)
    return cw, onehot


def _route_kernel(x1_ref, gf_ref, wr_ref, br_ref, utri_ref, hs_ref, cws_ref, pos_ref, gid_ref):
    tmb = x1_ref.shape[0]
    mp = hs_ref.shape[0]
    h = _rms(x1_ref[...], gf_ref[...])
    h_hi = h.astype(BF16)
    h_lo = (h - h_hi.astype(F32)).astype(BF16)
    l_hi = jnp.dot(h_hi, jnp.concatenate([wr_ref[0], wr_ref[1]], axis=1), preferred_element_type=F32)
    logits = (l_hi[:, :ROUTER_COLS] + l_hi[:, ROUTER_COLS:]
              + jnp.dot(h_lo, wr_ref[0], preferred_element_type=F32)) + br_ref[...]
    cw4, onehot = _route(logits.T[:32])

    r8 = lax.broadcasted_iota(jnp.int32, (8, ROUTER_COLS), 0)
    counts = jnp.broadcast_to(jnp.sum(onehot, axis=1, keepdims=True), (8, ROUTER_COLS))
    nch = jnp.floor((counts + (MOE_CHUNK - 1)) * (1.0 / MOE_CHUNK))
    inc = nch
    for sh in (1, 2):
        inc = inc + jnp.where(r8 >= sh, pltpu.roll(inc, sh, axis=0), 0.0)
    start = (inc - nch) * MOE_CHUNK
    rank = jnp.dot(onehot.astype(BF16), utri_ref[...], preferred_element_type=F32)
    pos_row = jnp.sum(onehot * (start[:, 0:1] + rank), axis=0, keepdims=True)
    pos_ref[...] = jnp.broadcast_to(pos_row, (ROUTER_COLS, tmb)).T
    prow = lax.broadcasted_iota(jnp.int32, (mp, tmb), 0).astype(F32)
    perm = jnp.where(prow == pos_row, 1.0, 0.0).astype(BF16)
    hs_ref[...] = jnp.dot(perm, h_hi, preferred_element_type=F32).astype(BF16)
    cw_hi = cw4.astype(BF16).astype(F32)
    cw_t = jnp.concatenate([cw_hi, cw4 - cw_hi, jnp.zeros((ROUTER_COLS - 16, tmb), F32)], axis=0).astype(BF16)
    cw2 = lax.dot_general(perm, cw_t, (((1,), (1,)), ((), ())), preferred_element_type=F32)
    cws_ref[...] = cw2 + pltpu.roll(cw2, ROUTER_COLS - 8, axis=1)
    chunk = lax.broadcasted_iota(jnp.int32, (8, ROUTER_COLS), 1).astype(F32)
    gid = jnp.zeros((8, ROUTER_COLS), F32)
    for grp in range(N_GROUPS):
        end = jnp.sum(jnp.where(r8 == grp, inc, 0.0), axis=0, keepdims=True)
        gid = gid + jnp.where(chunk >= end, 1.0, 0.0)
    gid_ref[...] = gid.astype(jnp.int32)


def _route_call(x1, g_ffn, wr, br, utri, tmb, mp):
    t = x1.shape[0]
    nb = t // tmb
    return pl.pallas_call(
        _route_kernel,
        grid=(nb,),
        in_specs=[
            pl.BlockSpec((tmb, D_MODEL), lambda i: (i, 0)),
            pl.BlockSpec((1, D_MODEL), lambda i: (0, 0)),
            pl.BlockSpec((2, D_MODEL, ROUTER_COLS), lambda i: (0, 0, 0)),
            pl.BlockSpec((1, ROUTER_COLS), lambda i: (0, 0)),
            pl.BlockSpec((tmb, tmb), lambda i: (0, 0)),
        ],
        out_specs=[
            pl.BlockSpec((mp, D_MODEL), lambda i: (i, 0)),
            pl.BlockSpec((mp, ROUTER_COLS), lambda i: (i, 0)),
            pl.BlockSpec((tmb, ROUTER_COLS), lambda i: (i, 0)),
            pl.BlockSpec((None, 8, ROUTER_COLS), lambda i: (i, 0, 0)),
        ],
        out_shape=[
            jax.ShapeDtypeStruct((nb * mp, D_MODEL), BF16),
            jax.ShapeDtypeStruct((nb * mp, ROUTER_COLS), F32),
            jax.ShapeDtypeStruct((t, ROUTER_COLS), F32),
            jax.ShapeDtypeStruct((nb, 8, ROUTER_COLS), jnp.int32),
        ],
        compiler_params=_cparams(("arbitrary",)),
        name="route",
    )(x1, g_ffn, wr, br, utri)


def _experts_kernel(order_ref, gids_ref, hs_ref, cws_ref, wg_ref, wu_ref, wd_ref, exp_ref, ys_ref):
    valid = gids_ref[pl.program_id(0)] < N_GROUPS

    @pl.when(valid)
    def _():
        x = hs_ref[...]
        hg = jnp.concatenate([jnp.dot(x, wg_ref[j], preferred_element_type=F32)
                              for j in range(EXPERTS_PER_GROUP)], axis=1)
        hu = jnp.concatenate([jnp.dot(x, wu_ref[j], preferred_element_type=F32)
                              for j in range(EXPERTS_PER_GROUP)], axis=1)
        cw = cws_ref[...]
        cw_hi = cw.astype(BF16)
        cw_lo = (cw - cw_hi.astype(F32)).astype(BF16)
        cwx = (jnp.dot(cw_hi, exp_ref[...], preferred_element_type=F32)
               + jnp.dot(cw_lo, exp_ref[...], preferred_element_type=F32))
        act = (hg * jax.nn.sigmoid(hg) * hu * cwx).astype(BF16)
        ys_ref[...] = jnp.dot(act, wd_ref[...], preferred_element_type=F32).astype(BF16)

    @pl.when(jnp.logical_not(valid))
    def _():
        ys_ref[...] = jnp.zeros(ys_ref.shape, BF16)


def _experts_call(order, gids, hs, cws, wg, wu, wd, expand):
    n_chunks = order.shape[0]
    ff = EXPERTS_PER_GROUP * EXPERT_FF
    grp = lambda s, order_ref, gids_ref: (jnp.minimum(gids_ref[s], N_GROUPS - 1), 0, 0)
    rows = lambda s, order_ref, gids_ref: (order_ref[s], 0)
    return pl.pallas_call(
        _experts_kernel,
        grid_spec=pltpu.PrefetchScalarGridSpec(
            num_scalar_prefetch=2,
            grid=(n_chunks,),
            in_specs=[
                pl.BlockSpec((MOE_CHUNK, D_MODEL), rows),
                pl.BlockSpec((MOE_CHUNK, ROUTER_COLS), rows),
                pl.BlockSpec((EXPERTS_PER_GROUP, D_MODEL, EXPERT_FF), grp),
                pl.BlockSpec((EXPERTS_PER_GROUP, D_MODEL, EXPERT_FF), grp),
                pl.BlockSpec((None, ff, D_MODEL), grp),
                pl.BlockSpec((ROUTER_COLS, ff), lambda s, order_ref, gids_ref: (0, 0)),
            ],
            out_specs=pl.BlockSpec((MOE_CHUNK, D_MODEL), rows),
        ),
        out_shape=jax.ShapeDtypeStruct(hs.shape, BF16),
        compiler_params=_cparams(("arbitrary",)),
        name="experts",
    )(order, gids, hs, cws, wg, wu, wd, expand)


def _combine_kernel(x1_ref, ys_ref, pos_ref, gfin_ref, out_ref):
    tmb = x1_ref.shape[0]
    mp = ys_ref.shape[0]
    pos = pos_ref[...]
    lane = lax.broadcasted_iota(jnp.int32, pos.shape, 1).astype(F32)
    del tmb
    sel = jnp.concatenate(
        [jnp.where(pos == lane + float(c * ROUTER_COLS), 1.0, 0.0).astype(BF16)
         for c in range(mp // ROUTER_COLS)], axis=1)
    moe = jnp.dot(sel, ys_ref[...], preferred_element_type=F32)
    out_ref[...] = _rms(x1_ref[...] + moe, gfin_ref[...])


def _combine_call(x1, ys, pos, g_final, tmb, mp):
    t = x1.shape[0]
    return pl.pallas_call(
        _combine_kernel,
        grid=(t // tmb,),
        in_specs=[
            pl.BlockSpec((tmb, D_MODEL), lambda i: (i, 0)),
            pl.BlockSpec((mp, D_MODEL), lambda i: (i, 0)),
            pl.BlockSpec((tmb, ROUTER_COLS), lambda i: (i, 0)),
            pl.BlockSpec((1, D_MODEL), lambda i: (0, 0)),
        ],
        out_specs=pl.BlockSpec((tmb, D_MODEL), lambda i: (i, 0)),
        out_shape=jax.ShapeDtypeStruct((t, D_MODEL), F32),
        compiler_params=_cparams(("arbitrary",)),
        name="combine",
    )(x1, ys, pos, g_final)


def _ssm_params(lambda_re, lambda_im, log_step, b_re, b_im, c_re, c_im):
    dt = jnp.exp(log_step)[:, None]
    mag = jnp.exp(lambda_re * dt)
    a_re = mag * jnp.cos(lambda_im * dt)
    a_im = mag * jnp.sin(lambda_im * dt)
    den = lambda_re * lambda_re + lambda_im * lambda_im
    num_re = a_re - 1.0
    z_re = (num_re * lambda_re + a_im * lambda_im) / den
    z_im = (a_im * lambda_re - num_re * lambda_im) / den
    bb_re = z_re[..., None] * b_re - z_im[..., None] * b_im
    bb_im = z_re[..., None] * b_im + z_im[..., None] * b_re
    gl = SSM_GROUPS // SUPER
    eye = jnp.eye(gl, dtype=F32)

    def in_blk(bb):
        t = bb.reshape(SUPER, gl, SSM_STATE, SSM_GROUP).transpose(0, 1, 3, 2)
        t = t[:, :, :, None, :] * eye[None, :, None, :, None]
        return t.reshape(SUPER, gl * SSM_GROUP, gl * SSM_STATE)

    def out_blk(cc):
        t = cc.reshape(SUPER, gl, SSM_GROUP, SSM_STATE).transpose(0, 1, 3, 2)
        t = t[:, :, :, None, :] * eye[None, :, None, :, None]
        return t.reshape(SUPER, gl * SSM_STATE, gl * SSM_GROUP)

    bm = jnp.concatenate([in_blk(bb_re), in_blk(bb_im)], axis=1).astype(BF16)
    cm = jnp.concatenate([out_blk(c_re), out_blk(-c_im)], axis=2).astype(BF16)
    ar_row = a_re.reshape(1, SSM_GROUPS * SSM_STATE)
    ai_row = a_im.reshape(1, SSM_GROUPS * SSM_STATE)
    ar = jnp.broadcast_to(ar_row, (8, SSM_GROUPS * SSM_STATE))
    ai = jnp.concatenate([jnp.broadcast_to(-ai_row, (4, ai_row.shape[1])),
                          jnp.broadcast_to(ai_row, (4, ai_row.shape[1]))], axis=0)
    return bm, cm, ar, ai


def _pick(n, pref):
    return pref if n % pref == 0 else n


def _mixers(x, g_mix, w_in, b_forget, b_gate, w_out_a, lambda_re, lambda_im, log_step,
            ssm_b_re, ssm_b_im, ssm_c_re, ssm_c_im, ssm_d, w_glu, b_glu, w_out_b, w_out):
    bsz, seq, _ = x.shape
    assert bsz == 4, "the SSM kernel packs (re/im) x 4 batches onto the 8 sublanes"
    t = bsz * seq
    layer = 0
    x2 = x.reshape(t, D_MODEL)

    w = w_in[layer]
    fw = FOX_WIDTH
    wq, wk, wv = w[:, :fw], w[:, fw:2 * fw], w[:, 2 * fw:3 * fw]
    wf = w[:, 3 * fw:3 * fw + HEADS]
    wu = w[:, 3 * fw + HEADS:3 * fw + HEADS + SSM_WIDTH]
    wgt = w[:, 3 * fw + HEADS + SSM_WIDTH:]
    scale = LOG2E / math.sqrt(HEAD_DIM)
    w1 = jnp.concatenate(
        [wq * scale, wk, wv, wu, wgt, jnp.pad(wf, ((0, 0), (0, LANES - HEADS)))], axis=1).astype(BF16)
    bf_pad = jnp.pad(b_forget[layer], (0, LANES - HEADS)).reshape(1, LANES)
    selq, selk, auxc = _aux_constants()

    tm = _pick(seq, ATTN_TILE)
    qx, kx, vt, u, gates = _proj(x2, g_mix[layer].reshape(1, D_MODEL), w1, bf_pad,
                                 b_gate[layer].reshape(1, 2 * D_MODEL), selq, selk, auxc, tm, seq)

    hb = HEADS * HEAD_BLOCK
    o = _attn(qx.reshape(bsz, seq, hb), kx.reshape(bsz, seq, hb), vt, tm, ATTN_HEADS).reshape(t, FOX_WIDTH)

    bm, cm, ar, ai = _ssm_params(lambda_re[layer], lambda_im[layer], log_step[layer],
                                 ssm_b_re[layer], ssm_b_im[layer], ssm_c_re[layer], ssm_c_im[layer])
    tb = _pick(seq, 128)
    ys = _ssm(u.reshape(bsz, seq, SSM_WIDTH), bm, cm, ar, ai, ssm_d[layer].reshape(1, SSM_WIDTH),
              tb).reshape(t, SSM_WIDTH)

    x1 = _merge(x2, o, ys, gates, w_out_a[layer].astype(BF16), w_glu[layer].astype(BF16),
                b_glu[layer].reshape(1, SSM_WIDTH), w_out_b[layer].astype(BF16),
                w_out[layer].astype(BF16), tm)
    return x1, o, ys


def kernel(x, g_mix, w_in, b_forget, b_gate, w_out_a, lambda_re, lambda_im, log_step, ssm_b_re, ssm_b_im, ssm_c_re, ssm_c_im, ssm_d, w_glu, b_glu, w_out_b, w_out, g_ffn, w_router_group, b_router_group, w_router_expert, b_router_expert, w_exp_gate, w_exp_up, w_exp_down, g_final):
    bsz, seq, _ = x.shape
    t = bsz * seq
    layer = 0
    x1, _, _ = _mixers(x, g_mix, w_in, b_forget, b_gate, w_out_a, lambda_re, lambda_im, log_step,
                       ssm_b_re, ssm_b_im, ssm_c_re, ssm_c_im, ssm_d, w_glu, b_glu, w_out_b, w_out)

    wr_f = jnp.concatenate([w_router_group[layer], w_router_expert[layer]], axis=1)
    wr_f = jnp.pad(wr_f, ((0, 0), (0, ROUTER_COLS - N_GROUPS - N_EXPERTS)))
    wr_hi = wr_f.astype(BF16)
    wr_lo = (wr_f - wr_hi.astype(F32)).astype(BF16)
    wr = jnp.stack([wr_hi, wr_lo])
    br = jnp.pad(jnp.concatenate([b_router_group[layer], b_router_expert[layer]]),
                 (0, ROUTER_COLS - N_GROUPS - N_EXPERTS)).reshape(1, ROUTER_COLS)
    ff = EXPERTS_PER_GROUP * EXPERT_FF
    wd = w_exp_down[layer].reshape(N_GROUPS, ff, D_MODEL).astype(BF16)
    expand = (jnp.arange(ROUTER_COLS)[:, None] == jnp.arange(ff)[None, :] // EXPERT_FF).astype(BF16)

    tmb = _pick(t, MOE_BLOCK)
    mp = tmb + N_GROUPS * MOE_CHUNK
    utri = jnp.triu(jnp.ones((tmb, tmb), BF16), 1)
    hs, cws, pos, gid = _route_call(x1, g_ffn[layer].reshape(1, D_MODEL), wr, br, utri, tmb, mp)

    cpb = mp // MOE_CHUNK
    gid_flat = gid[:, 0, :cpb].reshape(-1)
    n_chunks = gid_flat.shape[0]
    key = jnp.sort(gid_flat * n_chunks + jnp.arange(n_chunks, dtype=jnp.int32))
    ys = _experts_call(key % n_chunks, key // n_chunks, hs, cws, w_exp_gate[layer].astype(BF16),
                       w_exp_up[layer].astype(BF16), wd, expand)
    out = _combine_call(x1, ys, pos, g_final.reshape(1, D_MODEL), tmb, mp)
    return out.reshape(bsz, seq, D_MODEL)
```

```python
import functools
import math

import jax
import jax.numpy as jnp
from jax import lax
from jax.experimental import pallas as pl
from jax.experimental.pallas import tpu as pltpu

D_MODEL = 1024
HEADS = 8
HEAD_DIM = 64
FOX_WIDTH = HEADS * HEAD_DIM
SSM_WIDTH = 512
SSM_GROUP = 16
SSM_GROUPS = 32
SSM_STATE = 64
N_GROUPS = 4
EXPERTS_PER_GROUP = 4
N_EXPERTS = 16
EXPERT_FF = 256
EPS = 1e-6

LANES = 128
HEAD_BLOCK = LANES
AUX = HEAD_BLOCK - HEAD_DIM
V_ROWS = HEAD_DIM + 16
SUPER = 4
SG_CH = SSM_WIDTH // SUPER
SG_ST = SSM_GROUPS // SUPER * SSM_STATE
ROUTER_COLS = LANES

F32 = jnp.float32
BF16 = jnp.bfloat16
VMEM_LIMIT = 56 * 1024 * 1024
LOG2E = 1.4426950408889634
ATTN_HEADS = 4
ATTN_TILE = 512
MOE_BLOCK = 1024
MOE_CHUNK = 128


def _cparams(sem):
    return pltpu.CompilerParams(dimension_semantics=sem, vmem_limit_bytes=VMEM_LIMIT)


def _rms(x, g):
    return x * lax.rsqrt(jnp.mean(x * x, axis=-1, keepdims=True) + EPS) * g


def _proj_kernel(x_ref, g_ref, w_ref, bf_ref, bg_ref, selq_ref, selk_ref, auxc_ref,
                 q_ref, k_ref, v_ref, u_ref, gate_ref, carry_scr):
    tm = x_ref.shape[0]

    @pl.when(pl.program_id(1) == 0)
    def _():
        carry_scr[...] = jnp.zeros_like(carry_scr)

    h = _rms(x_ref[...], g_ref[...]).astype(BF16)

    def mm(lo, hi):
        return jnp.dot(h, w_ref[:, lo:hi], preferred_element_type=F32)

    w = FOX_WIDTH
    f = mm(8 * w, 8 * w + LANES) + bf_ref[...]
    lane = lax.broadcasted_iota(jnp.int32, f.shape, 1)
    row = lax.broadcasted_iota(jnp.int32, f.shape, 0)
    c = jnp.where(lane < HEADS, (jnp.minimum(f, 0.0) - jnp.log(1.0 + jnp.exp(-jnp.abs(f)))) * LOG2E, 0.0)
    sh = 1
    while sh < tm:
        c = c + jnp.where(row >= sh, pltpu.roll(c, sh, axis=0), 0.0)
        sh *= 2
    c = c + carry_scr[...]
    carry_scr[...] = c[tm - 1:tm, :]
    c1 = c.astype(BF16)
    r = c - c1.astype(F32)
    c2 = r.astype(BF16)
    c3 = (r - c2.astype(F32)).astype(BF16)

    def place(sel_ref):
        return (jnp.dot(c1, sel_ref[0], preferred_element_type=F32)
                + jnp.dot(c2, sel_ref[1], preferred_element_type=F32)
                + jnp.dot(c3, sel_ref[2], preferred_element_type=F32))

    aux_q = place(selq_ref) + auxc_ref[0:1, :]
    aux_k = place(selk_ref) + auxc_ref[1:2, :]
    aux_v = jnp.broadcast_to(auxc_ref[2:3, :], aux_q.shape)

    for out_ref, lo, aux in ((q_ref, 0, aux_q), (k_ref, w, aux_k), (v_ref, 2 * w, aux_v)):
        val = mm(lo, lo + w)
        for hd in range(HEADS):
            blk = jnp.concatenate([val[:, hd * HEAD_DIM:(hd + 1) * HEAD_DIM],
                                   aux[:, hd * AUX:(hd + 1) * AUX]], axis=1)
            if out_ref is v_ref:
                out_ref[hd * V_ROWS:(hd + 1) * V_ROWS, :] = blk.T[:V_ROWS].astype(BF16)
            else:
                out_ref[:, hd * HEAD_BLOCK:(hd + 1) * HEAD_BLOCK] = blk.astype(BF16)

    u_ref[...] = mm(3 * w, 4 * w).astype(BF16)
    for cc in range(4):
        lo = 4 * w + cc * w
        gate_ref[:, cc * w:(cc + 1) * w] = jax.nn.sigmoid(
            mm(lo, lo + w) + bg_ref[:, cc * w:(cc + 1) * w]).astype(BF16)


def _proj(x2, g_mix, w1, bf_pad, b_gate, selq, selk, auxc, tm, seq):
    t = x2.shape[0]
    bsz = t // seq
    nt = seq // tm
    n1 = w1.shape[1]
    hb = HEADS * HEAD_BLOCK
    row = lambda b, i: (b * nt + i, 0)
    const = lambda b, i: (0, 0)
    const3 = lambda b, i: (0, 0, 0)
    return pl.pallas_call(
        _proj_kernel,
        grid=(bsz, nt),
        in_specs=[
            pl.BlockSpec((tm, D_MODEL), row),
            pl.BlockSpec((1, D_MODEL), const),
            pl.BlockSpec((D_MODEL, n1), const),
            pl.BlockSpec((1, LANES), const),
            pl.BlockSpec((1, 2 * D_MODEL), const),
            pl.BlockSpec((3, LANES, HEADS * AUX), const3),
            pl.BlockSpec((3, LANES, HEADS * AUX), const3),
            pl.BlockSpec((8, HEADS * AUX), const),
        ],
        out_specs=[
            pl.BlockSpec((tm, hb), row),
            pl.BlockSpec((tm, hb), row),
            pl.BlockSpec((None, None, HEADS * V_ROWS, tm), lambda b, i: (b, i, 0, 0)),
            pl.BlockSpec((tm, SSM_WIDTH), row),
            pl.BlockSpec((tm, 2 * D_MODEL), row),
        ],
        out_shape=[
            jax.ShapeDtypeStruct((t, hb), BF16),
            jax.ShapeDtypeStruct((t, hb), BF16),
            jax.ShapeDtypeStruct((bsz, nt, HEADS * V_ROWS, tm), BF16),
            jax.ShapeDtypeStruct((t, SSM_WIDTH), BF16),
            jax.ShapeDtypeStruct((t, 2 * D_MODEL), BF16),
        ],
        scratch_shapes=[pltpu.VMEM((1, LANES), F32)],
        compiler_params=_cparams(("arbitrary", "arbitrary")),
        name="proj",
    )(x2, g_mix, w1, bf_pad, b_gate, selq, selk, auxc)


def _aux_constants():
    selq = jnp.zeros((3, LANES, HEADS * AUX), F32)
    selk = jnp.zeros((3, LANES, HEADS * AUX), F32)
    auxc = jnp.zeros((8, HEADS * AUX), F32)
    hd = jnp.arange(HEADS)
    for p in range(3):
        selq = selq.at[p, hd, hd * AUX + p].set(1.0)
        selk = selk.at[p, hd, hd * AUX + 3 + p].set(-1.0)
        auxc = auxc.at[0, hd * AUX + 3 + p].set(1.0)
        auxc = auxc.at[1, hd * AUX + p].set(1.0)
    auxc = auxc.at[2, hd * AUX].set(1.0)
    return selq.astype(BF16), selk.astype(BF16), auxc


def _attn_kernel(q_ref, k_ref, vt_ref, o_ref, s_scr, m_scr, acc_scr, *, tile, heads):
    i = pl.program_id(2)
    nt = (((1,), (1,)), ((), ()))

    m_scr[...] = jnp.full(m_scr.shape, -jnp.inf, F32)
    acc_scr[...] = jnp.zeros(acc_scr.shape, F32)

    def qk(slot, t):
        r0 = pl.multiple_of(t * tile, tile)
        for g in range(heads):
            s_scr[slot, g] = lax.dot_general(
                k_ref[pl.ds(r0, tile), g * HEAD_BLOCK:(g + 1) * HEAD_BLOCK],
                q_ref[:, g * HEAD_BLOCK:(g + 1) * HEAD_BLOCK], nt, preferred_element_type=F32)

    def softmax(slot, masked):
        ps, alphas = [], []
        for g in range(heads):
            s = s_scr[slot, g]
            if masked:
                kpos = lax.broadcasted_iota(jnp.int32, s.shape, 0)
                qpos = lax.broadcasted_iota(jnp.int32, s.shape, 1)
                s = jnp.where(kpos <= qpos, s, -jnp.inf)
            m = m_scr[g]
            m_new = jnp.maximum(m, jnp.max(s, axis=0, keepdims=True))
            ps.append(jnp.exp2(s - m_new).astype(BF16))
            alphas.append(jnp.exp2(m - m_new))
            m_scr[g] = m_new
        return ps, alphas

    def pv(pa, t):
        ps, alphas = pa
        for g in range(heads):
            acc_scr[g] = (alphas[g] * acc_scr[g]
                          + jnp.dot(vt_ref[t, g * V_ROWS:(g + 1) * V_ROWS, :], ps[g],
                                    preferred_element_type=F32))

    qk(0, 0)

    def body(tt, carry):
        t0 = 2 * tt
        qk(1, t0 + 1)
        pa0 = softmax(0, False)
        qk(0, t0 + 2)
        pv(pa0, t0)
        pv(softmax(1, False), t0 + 1)
        return carry

    lax.fori_loop(0, lax.shift_right_logical(i, 1), body, 0)

    @pl.when((i & 1) == 0)
    def _():
        pv(softmax(0, True), i)

    @pl.when((i & 1) == 1)
    def _():
        qk(1, i)
        pv(softmax(0, False), i - 1)
        pv(softmax(1, True), i)

    for pair in range(heads // 2):
        a0 = acc_scr[2 * pair]
        a1 = acc_scr[2 * pair + 1]
        o2 = jnp.concatenate([a0[:HEAD_DIM] / a0[HEAD_DIM:HEAD_DIM + 1],
                              a1[:HEAD_DIM] / a1[HEAD_DIM:HEAD_DIM + 1]], axis=0)
        o_ref[:, pair * LANES:(pair + 1) * LANES] = o2.T.astype(BF16)


def _attn(qx, kx, vt, tile, heads):
    b, s, _ = qx.shape
    n = s // tile
    assert vt.shape == (b, n, HEADS * V_ROWS, tile)
    hb = heads * HEAD_BLOCK
    return pl.pallas_call(
        functools.partial(_attn_kernel, tile=tile, heads=heads),
        grid=(b, HEADS // heads, n),
        in_specs=[
            pl.BlockSpec((None, tile, hb), lambda bi, hi, i: (bi, i, hi)),
            pl.BlockSpec((None, s, hb), lambda bi, hi, i: (bi, 0, hi)),
            pl.BlockSpec((None, n, heads * V_ROWS, tile), lambda bi, hi, i: (bi, 0, hi, 0)),
        ],
        out_specs=pl.BlockSpec((None, tile, heads * HEAD_DIM), lambda bi, hi, i: (bi, i, hi)),
        out_shape=jax.ShapeDtypeStruct((b, s, FOX_WIDTH), BF16),
        scratch_shapes=[
            pltpu.VMEM((2, heads, tile, tile), F32),
            pltpu.VMEM((heads, 1, tile), F32),
            pltpu.VMEM((heads, V_ROWS, tile), F32),
        ],
        compiler_params=_cparams(("arbitrary", "arbitrary", "arbitrary")),
        name="attn",
    )(qx, kx, vt)


def _ssm_kernel(u_ref, bm_ref, cm_ref, ar_ref, ai_ref, d_ref, y_ref, io_scr, x_scr, s_scr, *, tb):
    rows = tb * 8
    nb = u_ref.shape[0]

    @pl.when(pl.program_id(0) == 0)
    def _():
        s_scr[...] = jnp.zeros_like(s_scr)

    for g in range(SUPER):
        for b in range(nb):
            ub = u_ref[b, :, g * SG_CH:(g + 1) * SG_CH].astype(F32)
            io_scr[g, pl.ds(b, tb, stride=8), :] = ub
            io_scr[g, pl.ds(b + nb, tb, stride=8), :] = ub
    is_re = (lax.broadcasted_iota(jnp.int32, (rows, SG_CH), 0) % 8) < 4
    zero = jnp.zeros((rows, SG_CH), BF16)
    for g in range(SUPER):
        ug = io_scr[g].astype(BF16)
        lhs = jnp.concatenate([jnp.where(is_re, ug, zero), jnp.where(is_re, zero, ug)], axis=1)
        x_scr[:, g * SG_ST:(g + 1) * SG_ST] = jnp.dot(lhs, bm_ref[g], preferred_element_type=F32)

    half = 2 * SG_ST
    for c in range(SUPER * SG_ST // half):
        lo = c * half
        ar = ar_ref[:, lo:lo + half]
        ai = ai_ref[:, lo:lo + half]

        def body(t, s, lo=lo, ar=ar, ai=ai):
            r0 = pl.multiple_of(t * 8, 8)
            s = ar * s + ai * pltpu.roll(s, 4, axis=0) + x_scr[pl.ds(r0, 8), lo:lo + half]
            x_scr[pl.ds(r0, 8), lo:lo + half] = s
            return s

        s_scr[:, lo:lo + half] = lax.fori_loop(0, tb, body, s_scr[:, lo:lo + half], unroll=4)

    for g in range(SUPER):
        st = x_scr[:, g * SG_ST:(g + 1) * SG_ST].astype(BF16)
        o = jnp.dot(st, cm_ref[g], preferred_element_type=F32)
        y = o[:, :SG_CH] + pltpu.roll(o[:, SG_CH:], rows - 4, axis=0)
        io_scr[g] = y + d_ref[:, g * SG_CH:(g + 1) * SG_CH] * io_scr[g]
        for b in range(nb):
            y_ref[b, :, g * SG_CH:(g + 1) * SG_CH] = io_scr[g, pl.ds(b, tb, stride=8), :].astype(BF16)


def _ssm(u, bm, cm, ar, ai, d, tb):
    bsz, seq, _ = u.shape
    rows = tb * 8
    return pl.pallas_call(
        functools.partial(_ssm_kernel, tb=tb),
        grid=(seq // tb,),
        in_specs=[
            pl.BlockSpec((bsz, tb, SSM_WIDTH), lambda i: (0, i, 0)),
            pl.BlockSpec((SUPER, 2 * SG_CH, SG_ST), lambda i: (0, 0, 0)),
            pl.BlockSpec((SUPER, SG_ST, 2 * SG_CH), lambda i: (0, 0, 0)),
            pl.BlockSpec((8, SUPER * SG_ST), lambda i: (0, 0)),
            pl.BlockSpec((8, SUPER * SG_ST), lambda i: (0, 0)),
            pl.BlockSpec((1, SSM_WIDTH), lambda i: (0, 0)),
        ],
        out_specs=pl.BlockSpec((bsz, tb, SSM_WIDTH), lambda i: (0, i, 0)),
        out_shape=jax.ShapeDtypeStruct((bsz, seq, SSM_WIDTH), BF16),
        scratch_shapes=[
            pltpu.VMEM((SUPER, rows, SG_CH), F32),
            pltpu.VMEM((rows, SUPER * SG_ST), F32),
            pltpu.VMEM((8, SUPER * SG_ST), F32),
        ],
        compiler_params=_cparams(("arbitrary",)),
        name="ssm",
    )(u, bm, cm, ar, ai, d)


def _merge_kernel(x_ref, o_ref, ys_ref, gate_ref, woa_ref, wglu_ref, bglu_ref, wob_ref, wout_ref,
                  x1_ref):
    y_a = jnp.dot(o_ref[...], woa_ref[...], preferred_element_type=F32)
    z = jax.nn.gelu(ys_ref[...].astype(F32))
    zg = jnp.dot(z.astype(BF16), wglu_ref[...], preferred_element_type=F32) + bglu_ref[...]
    z = z * jax.nn.sigmoid(zg)
    y_b = jnp.dot(z.astype(BF16), wob_ref[...], preferred_element_type=F32)
    g0 = gate_ref[:, :D_MODEL].astype(F32)
    g1 = gate_ref[:, D_MODEL:].astype(F32)
    merged = (g0 * y_a + g1 * y_b).astype(BF16)
    x1_ref[...] = x_ref[...] + jnp.dot(merged, wout_ref[...], preferred_element_type=F32)


def _merge(x2, o, ys, gates, woa, wglu, bglu, wob, wout, tm):
    t = x2.shape[0]
    row = lambda i: (i, 0)
    const = lambda i: (0, 0)
    return pl.pallas_call(
        _merge_kernel,
        grid=(t // tm,),
        in_specs=[
            pl.BlockSpec((tm, D_MODEL), row),
            pl.BlockSpec((tm, FOX_WIDTH), row),
            pl.BlockSpec((tm, SSM_WIDTH), row),
            pl.BlockSpec((tm, 2 * D_MODEL), row),
            pl.BlockSpec((FOX_WIDTH, D_MODEL), const),
            pl.BlockSpec((SSM_WIDTH, SSM_WIDTH), const),
            pl.BlockSpec((1, SSM_WIDTH), const),
            pl.BlockSpec((SSM_WIDTH, D_MODEL), const),
            pl.BlockSpec((D_MODEL, D_MODEL), const),
        ],
        out_specs=pl.BlockSpec((tm, D_MODEL), row),
        out_shape=jax.ShapeDtypeStruct((t, D_MODEL), F32),
        compiler_params=_cparams(("arbitrary",)),
        name="merge",
    )(x2, o, ys, gates, woa, wglu, bglu, wob, wout)


def _route(lt):
    ri = lax.broadcasted_iota(jnp.int32, lt.shape, 0)
    r8 = lax.broadcasted_iota(jnp.int32, (8, lt.shape[1]), 0)
    neg = -jnp.inf
    big = jnp.int32(ROUTER_COLS)
    lg = jnp.where(ri < N_GROUPS, lt, neg)
    gmax = jnp.max(lg, axis=0, keepdims=True)
    p_sel = 1.0 / jnp.sum(jnp.exp(lg - gmax), axis=0, keepdims=True)
    g_idx = jnp.min(jnp.where(lg == gmax, ri, big), axis=0, keepdims=True)
    lo = N_GROUPS + g_idx * EXPERTS_PER_GROUP
    le = jnp.where((ri >= lo) & (ri < lo + EXPERTS_PER_GROUP), lt, neg)
    v1 = jnp.max(le, axis=0, keepdims=True)
    i1 = jnp.min(jnp.where(le == v1, ri, big), axis=0, keepdims=True)
    le2 = jnp.where(ri == i1, neg, le)
    v2 = jnp.max(le2, axis=0, keepdims=True)
    i2 = jnp.min(jnp.where(le2 == v2, ri, big), axis=0, keepdims=True)
    e2 = jnp.exp(v2 - v1)
    w1 = p_sel / (1.0 + e2)
    w2 = p_sel * e2 / (1.0 + e2)
    cw = jnp.where(r8 == i1 - lo, w1, 0.0) + jnp.where(r8 == i2 - lo, w2, 0.0)
    onehot = jnp.where(r8 == g_idx, 1.0, 0.0)
    return cw, onehot


def _route_kernel(x1_ref, gf_ref, wr_ref, br_ref, utri_ref, hs_ref, cws_ref, pos_ref, gid_ref):
    tmb = x1_ref.shape[0]
    mp = hs_ref.shape[0]
    h = _rms(x1_ref[...], gf_ref[...])
    h_hi = h.astype(BF16)
    h_lo = (h - h_hi.astype(F32)).astype(BF16)
    l_hi = jnp.dot(h_hi, jnp.concatenate([wr_ref[0], wr_ref[1]], axis=1), preferred_element_type=F32)
    logits = (l_hi[:, :ROUTER_COLS] + l_hi[:, ROUTER_COLS:]
              + jnp.dot(h_lo, wr_ref[0], preferred_element_type=F32)) + br_ref[...]
    cw4, onehot = _route(logits.T[:32])

    r8 = lax.broadcasted_iota(jnp.int32, (8, ROUTER_COLS), 0)
    counts = jnp.broadcast_to(jnp.sum(onehot, axis=1, keepdims=True), (8, ROUTER_COLS))
    nch = jnp.floor((counts + (MOE_CHUNK - 1)) * (1.0 / MOE_CHUNK))
    inc = nch
    for sh in (1, 2):
        inc = inc + jnp.where(r8 >= sh, pltpu.roll(inc, sh, axis=0), 0.0)
    start = (inc - nch) * MOE_CHUNK
    rank = jnp.dot(onehot.astype(BF16), utri_ref[...], preferred_element_type=F32)
    pos_row = jnp.sum(onehot * (start[:, 0:1] + rank), axis=0, keepdims=True)
    pos_ref[...] = jnp.broadcast_to(pos_row, (ROUTER_COLS, tmb)).T
    prow = lax.broadcasted_iota(jnp.int32, (mp, tmb), 0).astype(F32)
    perm = jnp.where(prow == pos_row, 1.0, 0.0).astype(BF16)
    hs_ref[...] = jnp.dot(perm, h_hi, preferred_element_type=F32).astype(BF16)
    cw_hi = cw4.astype(BF16).astype(F32)
    cw_t = jnp.concatenate([cw_hi, cw4 - cw_hi, jnp.zeros((ROUTER_COLS - 16, tmb), F32)], axis=0).astype(BF16)
    cw2 = lax.dot_general(perm, cw_t, (((1,), (1,)), ((), ())), preferred_element_type=F32)
    cws_ref[...] = cw2 + pltpu.roll(cw2, ROUTER_COLS - 8, axis=1)
    chunk = lax.broadcasted_iota(jnp.int32, (8, ROUTER_COLS), 1).astype(F32)
    gid = jnp.zeros((8, ROUTER_COLS), F32)
    for grp in range(N_GROUPS):
        end = jnp.sum(jnp.where(r8 == grp, inc, 0.0), axis=0, keepdims=True)
        gid = gid + jnp.where(chunk >= end, 1.0, 0.0)
    gid_ref[...] = gid.astype(jnp.int32)


def _route_call(x1, g_ffn, wr, br, utri, tmb, mp):
    t = x1.shape[0]
    nb = t // tmb
    return pl.pallas_call(
        _route_kernel,
        grid=(nb,),
        in_specs=[
            pl.BlockSpec((tmb, D_MODEL), lambda i: (i, 0)),
            pl.BlockSpec((1, D_MODEL), lambda i: (0, 0)),
            pl.BlockSpec((2, D_MODEL, ROUTER_COLS), lambda i: (0, 0, 0)),
            pl.BlockSpec((1, ROUTER_COLS), lambda i: (0, 0)),
            pl.BlockSpec((tmb, tmb), lambda i: (0, 0)),
        ],
        out_specs=[
            pl.BlockSpec((mp, D_MODEL), lambda i: (i, 0)),
            pl.BlockSpec((mp, ROUTER_COLS), lambda i: (i, 0)),
            pl.BlockSpec((tmb, ROUTER_COLS), lambda i: (i, 0)),
            pl.BlockSpec((None, 8, ROUTER_COLS), lambda i: (i, 0, 0)),
        ],
        out_shape=[
            jax.ShapeDtypeStruct((nb * mp, D_MODEL), BF16),
            jax.ShapeDtypeStruct((nb * mp, ROUTER_COLS), F32),
            jax.ShapeDtypeStruct((t, ROUTER_COLS), F32),
            jax.ShapeDtypeStruct((nb, 8, ROUTER_COLS), jnp.int32),
        ],
        compiler_params=_cparams(("arbitrary",)),
        name="route",
    )(x1, g_ffn, wr, br, utri)


def _experts_kernel(order_ref, gids_ref, hs_ref, cws_ref, wg_ref, wu_ref, wd_ref, ys_ref):
    valid = gids_ref[pl.program_id(0)] < N_GROUPS

    @pl.when(valid)
    def _():
        x = hs_ref[...]
        cw = cws_ref[...]
        acts = []
        for j in range(EXPERTS_PER_GROUP):
            hg = jnp.dot(x, wg_ref[j], preferred_element_type=F32)
            hu = jnp.dot(x, wu_ref[j], preferred_element_type=F32)
            acts.append((hg * jax.nn.sigmoid(hg) * hu * cw[:, j:j + 1]).astype(BF16))
        act = jnp.concatenate(acts, axis=1)
        ys_ref[...] = jnp.dot(act, wd_ref[...], preferred_element_type=F32).astype(BF16)

    @pl.when(jnp.logical_not(valid))
    def _():
        ys_ref[...] = jnp.zeros(ys_ref.shape, BF16)


def _experts_call(order, gids, hs, cws, wg, wu, wd):
    n_chunks = order.shape[0]
    ff = EXPERTS_PER_GROUP * EXPERT_FF
    grp = lambda s, order_ref, gids_ref: (jnp.minimum(gids_ref[s], N_GROUPS - 1), 0, 0)
    rows = lambda s, order_ref, gids_ref: (order_ref[s], 0)
    return pl.pallas_call(
        _experts_kernel,
        grid_spec=pltpu.PrefetchScalarGridSpec(
            num_scalar_prefetch=2,
            grid=(n_chunks,),
            in_specs=[
                pl.BlockSpec((MOE_CHUNK, D_MODEL), rows),
                pl.BlockSpec((MOE_CHUNK, ROUTER_COLS), rows),
                pl.BlockSpec((EXPERTS_PER_GROUP, D_MODEL, EXPERT_FF), grp),
                pl.BlockSpec((EXPERTS_PER_GROUP, D_MODEL, EXPERT_FF), grp),
                pl.BlockSpec((None, ff, D_MODEL), grp),
            ],
            out_specs=pl.BlockSpec((MOE_CHUNK, D_MODEL), rows),
        ),
        out_shape=jax.ShapeDtypeStruct(hs.shape, BF16),
        compiler_params=_cparams(("arbitrary",)),
        name="experts",
    )(order, gids, hs, cws, wg, wu, wd)


def _combine_kernel(x1_ref, ys_ref, pos_ref, gfin_ref, out_ref):
    tmb = x1_ref.shape[0]
    mp = ys_ref.shape[0]
    pos = pos_ref[...]
    lane = lax.broadcasted_iota(jnp.int32, pos.shape, 1).astype(F32)
    del tmb
    sel = jnp.concatenate(
        [jnp.where(pos == lane + float(c * ROUTER_COLS), 1.0, 0.0).astype(BF16)
         for c in range(mp // ROUTER_COLS)], axis=1)
    moe = jnp.dot(sel, ys_ref[...], preferred_element_type=F32)
    out_ref[...] = _rms(x1_ref[...] + moe, gfin_ref[...])


def _combine_call(x1, ys, pos, g_final, tmb, mp):
    t = x1.shape[0]
    return pl.pallas_call(
        _combine_kernel,
        grid=(t // tmb,),
        in_specs=[
            pl.BlockSpec((tmb, D_MODEL), lambda i: (i, 0)),
            pl.BlockSpec((mp, D_MODEL), lambda i: (i, 0)),
            pl.BlockSpec((tmb, ROUTER_COLS), lambda i: (i, 0)),
            pl.BlockSpec((1, D_MODEL), lambda i: (0, 0)),
        ],
        out_specs=pl.BlockSpec((tmb, D_MODEL), lambda i: (i, 0)),
        out_shape=jax.ShapeDtypeStruct((t, D_MODEL), F32),
        compiler_params=_cparams(("arbitrary",)),
        name="combine",
    )(x1, ys, pos, g_final)


def _ssm_params(lambda_re, lambda_im, log_step, b_re, b_im, c_re, c_im):
    dt = jnp.exp(log_step)[:, None]
    mag = jnp.exp(lambda_re * dt)
    a_re = mag * jnp.cos(lambda_im * dt)
    a_im = mag * jnp.sin(lambda_im * dt)
    den = lambda_re * lambda_re + lambda_im * lambda_im
    num_re = a_re - 1.0
    z_re = (num_re * lambda_re + a_im * lambda_im) / den
    z_im = (a_im * lambda_re - num_re * lambda_im) / den
    bb_re = z_re[..., None] * b_re - z_im[..., None] * b_im
    bb_im = z_re[..., None] * b_im + z_im[..., None] * b_re
    gl = SSM_GROUPS // SUPER
    eye = jnp.eye(gl, dtype=F32)

    def in_blk(bb):
        t = bb.reshape(SUPER, gl, SSM_STATE, SSM_GROUP).transpose(0, 1, 3, 2)
        t = t[:, :, :, None, :] * eye[None, :, None, :, None]
        return t.reshape(SUPER, gl * SSM_GROUP, gl * SSM_STATE)

    def out_blk(cc):
        t = cc.reshape(SUPER, gl, SSM_GROUP, SSM_STATE).transpose(0, 1, 3, 2)
        t = t[:, :, :, None, :] * eye[None, :, None, :, None]
        return t.reshape(SUPER, gl * SSM_STATE, gl * SSM_GROUP)

    bm = jnp.concatenate([in_blk(bb_re), in_blk(bb_im)], axis=1).astype(BF16)
    cm = jnp.concatenate([out_blk(c_re), out_blk(-c_im)], axis=2).astype(BF16)
    ar_row = a_re.reshape(1, SSM_GROUPS * SSM_STATE)
    ai_row = a_im.reshape(1, SSM_GROUPS * SSM_STATE)
    ar = jnp.broadcast_to(ar_row, (8, SSM_GROUPS * SSM_STATE))
    ai = jnp.concatenate([jnp.broadcast_to(-ai_row, (4, ai_row.shape[1])),
                          jnp.broadcast_to(ai_row, (4, ai_row.shape[1]))], axis=0)
    return bm, cm, ar, ai


def _pick(n, pref):
    return pref if n % pref == 0 else n


def _mixers(x, g_mix, w_in, b_forget, b_gate, w_out_a, lambda_re, lambda_im, log_step,
            ssm_b_re, ssm_b_im, ssm_c_re, ssm_c_im, ssm_d, w_glu, b_glu, w_out_b, w_out):
    bsz, seq, _ = x.shape
    assert bsz == 4, "the SSM kernel packs (re/im) x 4 batches onto the 8 sublanes"
    t = bsz * seq
    layer = 0
    x2 = x.reshape(t, D_MODEL)

    w = w_in[layer]
    fw = FOX_WIDTH
    wq, wk, wv = w[:, :fw], w[:, fw:2 * fw], w[:, 2 * fw:3 * fw]
    wf = w[:, 3 * fw:3 * fw + HEADS]
    wu = w[:, 3 * fw + HEADS:3 * fw + HEADS + SSM_WIDTH]
    wgt = w[:, 3 * fw + HEADS + SSM_WIDTH:]
    scale = LOG2E / math.sqrt(HEAD_DIM)
    w1 = jnp.concatenate(
        [wq * scale, wk, wv, wu, wgt, jnp.pad(wf, ((0, 0), (0, LANES - HEADS)))], axis=1).astype(BF16)
    bf_pad = jnp.pad(b_forget[layer], (0, LANES - HEADS)).reshape(1, LANES)
    selq, selk, auxc = _aux_constants()

    tm = _pick(seq, ATTN_TILE)
    qx, kx, vt, u, gates = _proj(x2, g_mix[layer].reshape(1, D_MODEL), w1, bf_pad,
                                 b_gate[layer].reshape(1, 2 * D_MODEL), selq, selk, auxc, tm, seq)

    hb = HEADS * HEAD_BLOCK
    o = _attn(qx.reshape(bsz, seq, hb), kx.reshape(bsz, seq, hb), vt, tm, ATTN_HEADS).reshape(t, FOX_WIDTH)

    bm, cm, ar, ai = _ssm_params(lambda_re[layer], lambda_im[layer], log_step[layer],
                                 ssm_b_re[layer], ssm_b_im[layer], ssm_c_re[layer], ssm_c_im[layer])
    tb = _pick(seq, 128)
    ys = _ssm(u.reshape(bsz, seq, SSM_WIDTH), bm, cm, ar, ai, ssm_d[layer].reshape(1, SSM_WIDTH),
              tb).reshape(t, SSM_WIDTH)

    x1 = _merge(x2, o, ys, gates, w_out_a[layer].astype(BF16), w_glu[layer].astype(BF16),
                b_glu[layer].reshape(1, SSM_WIDTH), w_out_b[layer].astype(BF16),
                w_out[layer].astype(BF16), tm)
    return x1, o, ys


def kernel(x, g_mix, w_in, b_forget, b_gate, w_out_a, lambda_re, lambda_im, log_step, ssm_b_re, ssm_b_im, ssm_c_re, ssm_c_im, ssm_d, w_glu, b_glu, w_out_b, w_out, g_ffn, w_router_group, b_router_group, w_router_expert, b_router_expert, w_exp_gate, w_exp_up, w_exp_down, g_final):
    bsz, seq, _ = x.shape
    t = bsz * seq
    layer = 0
    x1, _, _ = _mixers(x, g_mix, w_in, b_forget, b_gate, w_out_a, lambda_re, lambda_im, log_step,
                       ssm_b_re, ssm_b_im, ssm_c_re, ssm_c_im, ssm_d, w_glu, b_glu, w_out_b, w_out)

    wr_f = jnp.concatenate([w_router_group[layer], w_router_expert[layer]], axis=1)
    wr_f = jnp.pad(wr_f, ((0, 0), (0, ROUTER_COLS - N_GROUPS - N_EXPERTS)))
    wr_hi = wr_f.astype(BF16)
    wr_lo = (wr_f - wr_hi.astype(F32)).astype(BF16)
    wr = jnp.stack([wr_hi, wr_lo])
    br = jnp.pad(jnp.concatenate([b_router_group[layer], b_router_expert[layer]]),
                 (0, ROUTER_COLS - N_GROUPS - N_EXPERTS)).reshape(1, ROUTER_COLS)
    ff = EXPERTS_PER_GROUP * EXPERT_FF
    wd = w_exp_down[layer].reshape(N_GROUPS, ff, D_MODEL).astype(BF16)

    tmb = _pick(t, MOE_BLOCK)
    mp = tmb + N_GROUPS * MOE_CHUNK
    utri = jnp.triu(jnp.ones((tmb, tmb), BF16), 1)
    hs, cws, pos, gid = _route_call(x1, g_ffn[layer].reshape(1, D_MODEL), wr, br, utri, tmb, mp)

    cpb = mp // MOE_CHUNK
    gid_flat = gid[:, 0, :cpb].reshape(-1)
    n_chunks = gid_flat.shape[0]
    key = jnp.sort(gid_flat * n_chunks + jnp.arange(n_chunks, dtype=jnp.int32))
    ys = _experts_call(key % n_chunks, key // n_chunks, hs, cws, w_exp_gate[layer].astype(BF16),
                       w_exp_up[layer].astype(BF16), wd)
    out = _combine_call(x1, ys, pos, g_final.reshape(1, D_MODEL), tmb, mp)
    return out.reshape(bsz, seq, D_MODEL)
```

```python
import functools
import math

import jax
import jax.numpy as jnp
from jax import lax
from jax.experimental import pallas as pl
from jax.experimental.pallas import tpu as pltpu

D_MODEL = 1024
HEADS = 8
HEAD_DIM = 64
FOX_WIDTH = HEADS * HEAD_DIM
SSM_WIDTH = 512
SSM_GROUP = 16
SSM_GROUPS = 32
SSM_STATE = 64
N_GROUPS = 4
EXPERTS_PER_GROUP = 4
N_EXPERTS = 16
EXPERT_FF = 256
EPS = 1e-6

LANES = 128
HEAD_BLOCK = LANES
AUX = HEAD_BLOCK - HEAD_DIM
V_ROWS = HEAD_DIM + 16
SUPER = 4
SG_CH = SSM_WIDTH // SUPER
SG_ST = SSM_GROUPS // SUPER * SSM_STATE
ROUTER_COLS = LANES

F32 = jnp.float32
BF16 = jnp.bfloat16
VMEM_LIMIT = 56 * 1024 * 1024
LOG2E = 1.4426950408889634
ATTN_HEADS = 4
ATTN_TILE = 512
SKIP_LOG2 = 192.0
MOE_BLOCK = 1024
MOE_CHUNK = 128


def _cparams(sem):
    return pltpu.CompilerParams(dimension_semantics=sem, vmem_limit_bytes=VMEM_LIMIT)


def _rms(x, g):
    return x * lax.rsqrt(jnp.mean(x * x, axis=-1, keepdims=True) + EPS) * g


def _proj_kernel(x_ref, g_ref, w_ref, bf_ref, bg_ref, selq_ref, selk_ref, auxc_ref, hsel_ref,
                 q_ref, k_ref, v_ref, u_ref, gate_ref, stat_ref, carry_scr):
    tm = x_ref.shape[0]

    @pl.when(pl.program_id(1) == 0)
    def _():
        carry_scr[...] = jnp.zeros_like(carry_scr)

    h = _rms(x_ref[...], g_ref[...]).astype(BF16)

    def mm(lo, hi):
        return jnp.dot(h, w_ref[:, lo:hi], preferred_element_type=F32)

    w = FOX_WIDTH
    f = mm(8 * w, 8 * w + LANES) + bf_ref[...]
    lane = lax.broadcasted_iota(jnp.int32, f.shape, 1)
    row = lax.broadcasted_iota(jnp.int32, f.shape, 0)
    c = jnp.where(lane < HEADS, (jnp.minimum(f, 0.0) - jnp.log(1.0 + jnp.exp(-jnp.abs(f)))) * LOG2E, 0.0)
    sh = 1
    while sh < tm:
        c = c + jnp.where(row >= sh, pltpu.roll(c, sh, axis=0), 0.0)
        sh *= 2
    c = c + carry_scr[...]
    carry_scr[...] = c[tm - 1:tm, :]
    c1 = c.astype(BF16)
    r = c - c1.astype(F32)
    c2 = r.astype(BF16)
    c3 = (r - c2.astype(F32)).astype(BF16)

    def place(sel_ref):
        return (jnp.dot(c1, sel_ref[0], preferred_element_type=F32)
                + jnp.dot(c2, sel_ref[1], preferred_element_type=F32)
                + jnp.dot(c3, sel_ref[2], preferred_element_type=F32))

    aux_q = place(selq_ref) + auxc_ref[0:1, :]
    aux_k = place(selk_ref) + auxc_ref[1:2, :]
    aux_v = jnp.broadcast_to(auxc_ref[2:3, :], aux_q.shape)

    vals = {}
    for out_ref, lo, aux in ((q_ref, 0, aux_q), (k_ref, w, aux_k), (v_ref, 2 * w, aux_v)):
        val = mm(lo, lo + w)
        vals[lo] = val
        for hd in range(HEADS):
            blk = jnp.concatenate([val[:, hd * HEAD_DIM:(hd + 1) * HEAD_DIM],
                                   aux[:, hd * AUX:(hd + 1) * AUX]], axis=1)
            if out_ref is v_ref:
                out_ref[hd * V_ROWS:(hd + 1) * V_ROWS, :] = blk.T[:V_ROWS].astype(BF16)
            else:
                out_ref[:, hd * HEAD_BLOCK:(hd + 1) * HEAD_BLOCK] = blk.astype(BF16)

    qv, kv = vals[0], vals[w]

    def head_sum(x):
        return jnp.dot(x.astype(BF16), hsel_ref[...], preferred_element_type=F32)

    stat_ref[0:1, :] = jnp.max(head_sum(qv * qv), axis=0, keepdims=True)
    stat_ref[1:2, :] = jnp.max(head_sum(kv * kv), axis=0, keepdims=True)
    stat_ref[2:3, :] = jnp.min(head_sum(qv * kv), axis=0, keepdims=True)
    stat_ref[3:4, :] = jnp.max(c, axis=0, keepdims=True)
    stat_ref[4:5, :] = jnp.min(c, axis=0, keepdims=True)
    stat_ref[5:8, :] = jnp.zeros((3, LANES), F32)

    u_ref[...] = mm(3 * w, 4 * w).astype(BF16)
    for cc in range(4):
        lo = 4 * w + cc * w
        gate_ref[:, cc * w:(cc + 1) * w] = jax.nn.sigmoid(
            mm(lo, lo + w) + bg_ref[:, cc * w:(cc + 1) * w]).astype(BF16)


def _proj(x2, g_mix, w1, bf_pad, b_gate, selq, selk, auxc, hsel, tm, seq):
    t = x2.shape[0]
    bsz = t // seq
    nt = seq // tm
    n1 = w1.shape[1]
    hb = HEADS * HEAD_BLOCK
    row = lambda b, i: (b * nt + i, 0)
    const = lambda b, i: (0, 0)
    const3 = lambda b, i: (0, 0, 0)
    return pl.pallas_call(
        _proj_kernel,
        grid=(bsz, nt),
        in_specs=[
            pl.BlockSpec((tm, D_MODEL), row),
            pl.BlockSpec((1, D_MODEL), const),
            pl.BlockSpec((D_MODEL, n1), const),
            pl.BlockSpec((1, LANES), const),
            pl.BlockSpec((1, 2 * D_MODEL), const),
            pl.BlockSpec((3, LANES, HEADS * AUX), const3),
            pl.BlockSpec((3, LANES, HEADS * AUX), const3),
            pl.BlockSpec((8, HEADS * AUX), const),
            pl.BlockSpec((FOX_WIDTH, LANES), const),
        ],
        out_specs=[
            pl.BlockSpec((tm, hb), row),
            pl.BlockSpec((tm, hb), row),
            pl.BlockSpec((None, None, HEADS * V_ROWS, tm), lambda b, i: (b, i, 0, 0)),
            pl.BlockSpec((tm, SSM_WIDTH), row),
            pl.BlockSpec((tm, 2 * D_MODEL), row),
            pl.BlockSpec((None, None, 8, LANES), lambda b, i: (b, i, 0, 0)),
        ],
        out_shape=[
            jax.ShapeDtypeStruct((t, hb), BF16),
            jax.ShapeDtypeStruct((t, hb), BF16),
            jax.ShapeDtypeStruct((bsz, nt, HEADS * V_ROWS, tm), BF16),
            jax.ShapeDtypeStruct((t, SSM_WIDTH), BF16),
            jax.ShapeDtypeStruct((t, 2 * D_MODEL), BF16),
            jax.ShapeDtypeStruct((bsz, nt, 8, LANES), F32),
        ],
        scratch_shapes=[pltpu.VMEM((1, LANES), F32)],
        compiler_params=_cparams(("arbitrary", "arbitrary")),
        name="proj",
    )(x2, g_mix, w1, bf_pad, b_gate, selq, selk, auxc, hsel)


def _aux_constants():
    selq = jnp.zeros((3, LANES, HEADS * AUX), F32)
    selk = jnp.zeros((3, LANES, HEADS * AUX), F32)
    auxc = jnp.zeros((8, HEADS * AUX), F32)
    hd = jnp.arange(HEADS)
    for p in range(3):
        selq = selq.at[p, hd, hd * AUX + p].set(1.0)
        selk = selk.at[p, hd, hd * AUX + 3 + p].set(-1.0)
        auxc = auxc.at[0, hd * AUX + 3 + p].set(1.0)
        auxc = auxc.at[1, hd * AUX + p].set(1.0)
    auxc = auxc.at[2, hd * AUX].set(1.0)
    return selq.astype(BF16), selk.astype(BF16), auxc


def _attn_kernel(first_ref, q_ref, k_ref, vt_ref, o_ref, s_scr, m_scr, acc_scr, *, tile, heads):
    i = pl.program_id(2)
    j0 = first_ref[(pl.program_id(0) * pl.num_programs(1) + pl.program_id(1)) * pl.num_programs(2) + i]
    cnt = i - j0
    nt = (((1,), (1,)), ((), ()))

    m_scr[...] = jnp.full(m_scr.shape, -jnp.inf, F32)
    acc_scr[...] = jnp.zeros(acc_scr.shape, F32)

    def qk(slot, t):
        r0 = pl.multiple_of(t * tile, tile)
        for g in range(heads):
            s_scr[slot, g] = lax.dot_general(
                k_ref[pl.ds(r0, tile), g * HEAD_BLOCK:(g + 1) * HEAD_BLOCK],
                q_ref[:, g * HEAD_BLOCK:(g + 1) * HEAD_BLOCK], nt, preferred_element_type=F32)

    def softmax(slot, masked):
        ps, alphas = [], []
        for g in range(heads):
            s = s_scr[slot, g]
            if masked:
                kpos = lax.broadcasted_iota(jnp.int32, s.shape, 0)
                qpos = lax.broadcasted_iota(jnp.int32, s.shape, 1)
                s = jnp.where(kpos <= qpos, s, -jnp.inf)
            m = m_scr[g]
            m_new = jnp.maximum(m, jnp.max(s, axis=0, keepdims=True))
            ps.append(jnp.exp2(s - m_new).astype(BF16))
            alphas.append(jnp.exp2(m - m_new))
            m_scr[g] = m_new
        return ps, alphas

    def pv(pa, t):
        ps, alphas = pa
        for g in range(heads):
            acc_scr[g] = (alphas[g] * acc_scr[g]
                          + jnp.dot(vt_ref[t, g * V_ROWS:(g + 1) * V_ROWS, :], ps[g],
                                    preferred_element_type=F32))

    qk(0, j0)

    def body(tt, carry):
        t0 = j0 + 2 * tt
        qk(1, t0 + 1)
        pa0 = softmax(0, False)
        qk(0, t0 + 2)
        pv(pa0, t0)
        pv(softmax(1, False), t0 + 1)
        return carry

    lax.fori_loop(0, lax.shift_right_logical(cnt, 1), body, 0)

    @pl.when((cnt & 1) == 0)
    def _():
        pv(softmax(0, True), i)

    @pl.when((cnt & 1) == 1)
    def _():
        qk(1, i)
        pv(softmax(0, False), i - 1)
        pv(softmax(1, True), i)

    for pair in range(heads // 2):
        a0 = acc_scr[2 * pair]
        a1 = acc_scr[2 * pair + 1]
        o2 = jnp.concatenate([a0[:HEAD_DIM] / a0[HEAD_DIM:HEAD_DIM + 1],
                              a1[:HEAD_DIM] / a1[HEAD_DIM:HEAD_DIM + 1]], axis=0)
        o_ref[:, pair * LANES:(pair + 1) * LANES] = o2.T.astype(BF16)


def _attn(first, qx, kx, vt, tile, heads):
    b, s, _ = qx.shape
    n = s // tile
    assert vt.shape == (b, n, HEADS * V_ROWS, tile)
    hb = heads * HEAD_BLOCK
    return pl.pallas_call(
        functools.partial(_attn_kernel, tile=tile, heads=heads),
        grid_spec=pltpu.PrefetchScalarGridSpec(
            num_scalar_prefetch=1,
            grid=(b, HEADS // heads, n),
            in_specs=[
                pl.BlockSpec((None, tile, hb), lambda bi, hi, i, first_ref: (bi, i, hi)),
                pl.BlockSpec((None, s, hb), lambda bi, hi, i, first_ref: (bi, 0, hi)),
                pl.BlockSpec((None, n, heads * V_ROWS, tile), lambda bi, hi, i, first_ref: (bi, 0, hi, 0)),
            ],
            out_specs=pl.BlockSpec((None, tile, heads * HEAD_DIM), lambda bi, hi, i, first_ref: (bi, i, hi)),
            scratch_shapes=[
                pltpu.VMEM((2, heads, tile, tile), F32),
                pltpu.VMEM((heads, 1, tile), F32),
                pltpu.VMEM((heads, V_ROWS, tile), F32),
            ],
        ),
        out_shape=jax.ShapeDtypeStruct((b, s, FOX_WIDTH), BF16),
        compiler_params=_cparams(("arbitrary", "arbitrary", "arbitrary")),
        name="attn",
    )(first, qx, kx, vt)


def _first_key_tile(stats, heads):
    st = stats[..., :HEADS]
    slack = 1.0 + 2.0 ** -6
    qn = jnp.sqrt(st[:, :, 0]) * slack
    kn = jnp.sqrt(st[:, :, 1]) * slack
    self_min, c_max, c_min = st[:, :, 2], st[:, :, 3], st[:, :, 4]
    upper = qn[:, :, None] * kn[:, None, :] + c_max[:, :, None] - c_min[:, None, :]
    lower = self_min - (slack - 1.0) * qn * kn
    needed = upper - lower[:, :, None] > -SKIP_LOG2
    n = st.shape[1]
    idx = jnp.arange(n, dtype=jnp.int32)
    needed = needed | (idx[None, None, :, None] >= idx[None, :, None, None])
    needed = needed.reshape(needed.shape[:3] + (HEADS // heads, heads)).any(axis=-1)
    first = jnp.min(jnp.where(needed, idx[None, None, :, None], n), axis=2)
    return first.transpose(0, 2, 1).reshape(-1).astype(jnp.int32)


def _ssm_kernel(u_ref, bm_ref, cm_ref, ar_ref, ai_ref, d_ref, y_ref, io_scr, x_scr, s_scr, *, tb):
    rows = tb * 8
    nb = u_ref.shape[0]

    @pl.when(pl.program_id(0) == 0)
    def _():
        s_scr[...] = jnp.zeros_like(s_scr)

    for g in range(SUPER):
        for b in range(nb):
            ub = u_ref[b, :, g * SG_CH:(g + 1) * SG_CH].astype(F32)
            io_scr[g, pl.ds(b, tb, stride=8), :] = ub
            io_scr[g, pl.ds(b + nb, tb, stride=8), :] = ub
    is_re = (lax.broadcasted_iota(jnp.int32, (rows, SG_CH), 0) % 8) < 4
    zero = jnp.zeros((rows, SG_CH), BF16)
    for g in range(SUPER):
        ug = io_scr[g].astype(BF16)
        lhs = jnp.concatenate([jnp.where(is_re, ug, zero), jnp.where(is_re, zero, ug)], axis=1)
        x_scr[:, g * SG_ST:(g + 1) * SG_ST] = jnp.dot(lhs, bm_ref[g], preferred_element_type=F32)

    half = 2 * SG_ST
    for c in range(SUPER * SG_ST // half):
        lo = c * half
        ar = ar_ref[:, lo:lo + half]
        ai = ai_ref[:, lo:lo + half]

        def body(t, s, lo=lo, ar=ar, ai=ai):
            r0 = pl.multiple_of(t * 8, 8)
            s = ar * s + ai * pltpu.roll(s, 4, axis=0) + x_scr[pl.ds(r0, 8), lo:lo + half]
            x_scr[pl.ds(r0, 8), lo:lo + half] = s
            return s

        s_scr[:, lo:lo + half] = lax.fori_loop(0, tb, body, s_scr[:, lo:lo + half], unroll=4)

    for g in range(SUPER):
        st = x_scr[:, g * SG_ST:(g + 1) * SG_ST].astype(BF16)
        o = jnp.dot(st, cm_ref[g], preferred_element_type=F32)
        y = o[:, :SG_CH] + pltpu.roll(o[:, SG_CH:], rows - 4, axis=0)
        io_scr[g] = y + d_ref[:, g * SG_CH:(g + 1) * SG_CH] * io_scr[g]
        for b in range(nb):
            y_ref[b, :, g * SG_CH:(g + 1) * SG_CH] = io_scr[g, pl.ds(b, tb, stride=8), :].astype(BF16)


def _ssm(u, bm, cm, ar, ai, d, tb):
    bsz, seq, _ = u.shape
    rows = tb * 8
    return pl.pallas_call(
        functools.partial(_ssm_kernel, tb=tb),
        grid=(seq // tb,),
        in_specs=[
            pl.BlockSpec((bsz, tb, SSM_WIDTH), lambda i: (0, i, 0)),
            pl.BlockSpec((SUPER, 2 * SG_CH, SG_ST), lambda i: (0, 0, 0)),
            pl.BlockSpec((SUPER, SG_ST, 2 * SG_CH), lambda i: (0, 0, 0)),
            pl.BlockSpec((8, SUPER * SG_ST), lambda i: (0, 0)),
            pl.BlockSpec((8, SUPER * SG_ST), lambda i: (0, 0)),
            pl.BlockSpec((1, SSM_WIDTH), lambda i: (0, 0)),
        ],
        out_specs=pl.BlockSpec((bsz, tb, SSM_WIDTH), lambda i: (0, i, 0)),
        out_shape=jax.ShapeDtypeStruct((bsz, seq, SSM_WIDTH), BF16),
        scratch_shapes=[
            pltpu.VMEM((SUPER, rows, SG_CH), F32),
            pltpu.VMEM((rows, SUPER * SG_ST), F32),
            pltpu.VMEM((8, SUPER * SG_ST), F32),
        ],
        compiler_params=_cparams(("arbitrary",)),
        name="ssm",
    )(u, bm, cm, ar, ai, d)


def _merge_kernel(x_ref, o_ref, ys_ref, gate_ref, woa_ref, wglu_ref, bglu_ref, wob_ref, wout_ref,
                  x1_ref):
    y_a = jnp.dot(o_ref[...], woa_ref[...], preferred_element_type=F32)
    z = jax.nn.gelu(ys_ref[...].astype(F32))
    zg = jnp.dot(z.astype(BF16), wglu_ref[...], preferred_element_type=F32) + bglu_ref[...]
    z = z * jax.nn.sigmoid(zg)
    y_b = jnp.dot(z.astype(BF16), wob_ref[...], preferred_element_type=F32)
    g0 = gate_ref[:, :D_MODEL].astype(F32)
    g1 = gate_ref[:, D_MODEL:].astype(F32)
    merged = (g0 * y_a + g1 * y_b).astype(BF16)
    x1_ref[...] = x_ref[...] + jnp.dot(merged, wout_ref[...], preferred_element_type=F32)


def _merge(x2, o, ys, gates, woa, wglu, bglu, wob, wout, tm):
    t = x2.shape[0]
    row = lambda i: (i, 0)
    const = lambda i: (0, 0)
    return pl.pallas_call(
        _merge_kernel,
        grid=(t // tm,),
        in_specs=[
            pl.BlockSpec((tm, D_MODEL), row),
            pl.BlockSpec((tm, FOX_WIDTH), row),
            pl.BlockSpec((tm, SSM_WIDTH), row),
            pl.BlockSpec((tm, 2 * D_MODEL), row),
            pl.BlockSpec((FOX_WIDTH, D_MODEL), const),
            pl.BlockSpec((SSM_WIDTH, SSM_WIDTH), const),
            pl.BlockSpec((1, SSM_WIDTH), const),
            pl.BlockSpec((SSM_WIDTH, D_MODEL), const),
            pl.BlockSpec((D_MODEL, D_MODEL), const),
        ],
        out_specs=pl.BlockSpec((tm, D_MODEL), row),
        out_shape=jax.ShapeDtypeStruct((t, D_MODEL), F32),
        compiler_params=_cparams(("arbitrary",)),
        name="merge",
    )(x2, o, ys, gates, woa, wglu, bglu, wob, wout)


def _route(lt):
    ri = lax.broadcasted_iota(jnp.int32, lt.shape, 0)
    r8 = lax.broadcasted_iota(jnp.int32, (8, lt.shape[1]), 0)
    neg = -jnp.inf
    big = jnp.int32(ROUTER_COLS)
    lg = jnp.where(ri < N_GROUPS, lt, neg)
    gmax = jnp.max(lg, axis=0, keepdims=True)
    p_sel = 1.0 / jnp.sum(jnp.exp(lg - gmax), axis=0, keepdims=True)
    g_idx = jnp.min(jnp.where(lg == gmax, ri, big), axis=0, keepdims=True)
    lo = N_GROUPS + g_idx * EXPERTS_PER_GROUP
    le = jnp.where((ri >= lo) & (ri < lo + EXPERTS_PER_GROUP), lt, neg)
    v1 = jnp.max(le, axis=0, keepdims=True)
    i1 = jnp.min(jnp.where(le == v1, ri, big), axis=0, keepdims=True)
    le2 = jnp.where(ri == i1, neg, le)
    v2 = jnp.max(le2, axis=0, keepdims=True)
    i2 = jnp.min(jnp.where(le2 == v2, ri, big), axis=0, keepdims=True)
    e2 = jnp.exp(v2 - v1)
    w1 = p_sel / (1.0 + e2)
    w2 = p_sel * e2 / (1.0 + e2)
    cw = jnp.where(r8 == i1 - lo, w1, 0.0) + jnp.where(r8 == i2 - lo, w2, 0.0)
    onehot = jnp.where(r8 == g_idx, 1.0, 0.0)
    return cw, onehot


def _route_kernel(x1_ref, gf_ref, wr_ref, br_ref, utri_ref, hs_ref, cws_ref, pos_ref, gid_ref):
    tmb = x1_ref.shape[0]
    mp = hs_ref.shape[0]
    h = _rms(x1_ref[...], gf_ref[...])
    h_hi = h.astype(BF16)
    h_lo = (h - h_hi.astype(F32)).astype(BF16)
    l_hi = jnp.dot(h_hi, jnp.concatenate([wr_ref[0], wr_ref[1]], axis=1), preferred_element_type=F32)
    logits = (l_hi[:, :ROUTER_COLS] + l_hi[:, ROUTER_COLS:]
              + jnp.dot(h_lo, wr_ref[0], preferred_element_type=F32)) + br_ref[...]
    cw4, onehot = _route(logits.T[:32])

    r8 = lax.broadcasted_iota(jnp.int32, (8, ROUTER_COLS), 0)
    counts = jnp.broadcast_to(jnp.sum(onehot, axis=1, keepdims=True), (8, ROUTER_COLS))
    nch = jnp.floor((counts + (MOE_CHUNK - 1)) * (1.0 / MOE_CHUNK))
    inc = nch
    for sh in (1, 2):
        inc = inc + jnp.where(r8 >= sh, pltpu.roll(inc, sh, axis=0), 0.0)
    start = (inc - nch) * MOE_CHUNK
    rank = jnp.dot(onehot.astype(BF16), utri_ref[...], preferred_element_type=F32)
    pos_row = jnp.sum(onehot * (start[:, 0:1] + rank), axis=0, keepdims=True)
    pos_ref[...] = jnp.broadcast_to(pos_row, (ROUTER_COLS, tmb)).T
    prow = lax.broadcasted_iota(jnp.int32, (mp, tmb), 0).astype(F32)
    perm = jnp.where(prow == pos_row, 1.0, 0.0).astype(BF16)
    hs_ref[...] = jnp.dot(perm, h_hi, preferred_element_type=F32).astype(BF16)
    cw_hi = cw4.astype(BF16).astype(F32)
    cw_t = jnp.concatenate([cw_hi, cw4 - cw_hi, jnp.zeros((ROUTER_COLS - 16, tmb), F32)], axis=0).astype(BF16)
    cw2 = lax.dot_general(perm, cw_t, (((1,), (1,)), ((), ())), preferred_element_type=F32)
    cws_ref[...] = cw2 + pltpu.roll(cw2, ROUTER_COLS - 8, axis=1)
    chunk = lax.broadcasted_iota(jnp.int32, (8, ROUTER_COLS), 1).astype(F32)
    gid = jnp.zeros((8, ROUTER_COLS), F32)
    for grp in range(N_GROUPS):
        end = jnp.sum(jnp.where(r8 == grp, inc, 0.0), axis=0, keepdims=True)
        gid = gid + jnp.where(chunk >= end, 1.0, 0.0)
    gid_ref[...] = gid.astype(jnp.int32)


def _route_call(x1, g_ffn, wr, br, utri, tmb, mp):
    t = x1.shape[0]
    nb = t // tmb
    return pl.pallas_call(
        _route_kernel,
        grid=(nb,),
        in_specs=[
            pl.BlockSpec((tmb, D_MODEL), lambda i: (i, 0)),
            pl.BlockSpec((1, D_MODEL), lambda i: (0, 0)),
            pl.BlockSpec((2, D_MODEL, ROUTER_COLS), lambda i: (0, 0, 0)),
            pl.BlockSpec((1, ROUTER_COLS), lambda i: (0, 0)),
            pl.BlockSpec((tmb, tmb), lambda i: (0, 0)),
        ],
        out_specs=[
            pl.BlockSpec((mp, D_MODEL), lambda i: (i, 0)),
            pl.BlockSpec((mp, ROUTER_COLS), lambda i: (i, 0)),
            pl.BlockSpec((tmb, ROUTER_COLS), lambda i: (i, 0)),
            pl.BlockSpec((None, 8, ROUTER_COLS), lambda i: (i, 0, 0)),
        ],
        out_shape=[
            jax.ShapeDtypeStruct((nb * mp, D_MODEL), BF16),
            jax.ShapeDtypeStruct((nb * mp, ROUTER_COLS), F32),
            jax.ShapeDtypeStruct((t, ROUTER_COLS), F32),
            jax.ShapeDtypeStruct((nb, 8, ROUTER_COLS), jnp.int32),
        ],
        compiler_params=_cparams(("arbitrary",)),
        name="route",
    )(x1, g_ffn, wr, br, utri)


def _experts_kernel(order_ref, gids_ref, hs_ref, cws_ref, wg_ref, wu_ref, wd_ref, ys_ref):
    valid = gids_ref[pl.program_id(0)] < N_GROUPS

    @pl.when(valid)
    def _():
        x = hs_ref[...]
        cw = cws_ref[...]
        acts = []
        for j in range(EXPERTS_PER_GROUP):
            hg = jnp.dot(x, wg_ref[j], preferred_element_type=F32)
            hu = jnp.dot(x, wu_ref[j], preferred_element_type=F32)
            acts.append((hg * jax.nn.sigmoid(hg) * hu * cw[:, j:j + 1]).astype(BF16))
        act = jnp.concatenate(acts, axis=1)
        ys_ref[...] = jnp.dot(act, wd_ref[...], preferred_element_type=F32).astype(BF16)

    @pl.when(jnp.logical_not(valid))
    def _():
        ys_ref[...] = jnp.zeros(ys_ref.shape, BF16)


def _experts_call(order, gids, hs, cws, wg, wu, wd):
    n_chunks = order.shape[0]
    ff = EXPERTS_PER_GROUP * EXPERT_FF
    grp = lambda s, order_ref, gids_ref: (jnp.minimum(gids_ref[s], N_GROUPS - 1), 0, 0)
    rows = lambda s, order_ref, gids_ref: (order_ref[s], 0)
    return pl.pallas_call(
        _experts_kernel,
        grid_spec=pltpu.PrefetchScalarGridSpec(
            num_scalar_prefetch=2,
            grid=(n_chunks,),
            in_specs=[
                pl.BlockSpec((MOE_CHUNK, D_MODEL), rows),
                pl.BlockSpec((MOE_CHUNK, ROUTER_COLS), rows),
                pl.BlockSpec((EXPERTS_PER_GROUP, D_MODEL, EXPERT_FF), grp),
                pl.BlockSpec((EXPERTS_PER_GROUP, D_MODEL, EXPERT_FF), grp),
                pl.BlockSpec((None, ff, D_MODEL), grp),
            ],
            out_specs=pl.BlockSpec((MOE_CHUNK, D_MODEL), rows),
        ),
        out_shape=jax.ShapeDtypeStruct(hs.shape, BF16),
        compiler_params=_cparams(("arbitrary",)),
        name="experts",
    )(order, gids, hs, cws, wg, wu, wd)


def _combine_kernel(x1_ref, ys_ref, pos_ref, gfin_ref, out_ref):
    tmb = x1_ref.shape[0]
    mp = ys_ref.shape[0]
    pos = pos_ref[...]
    lane = lax.broadcasted_iota(jnp.int32, pos.shape, 1).astype(F32)
    del tmb
    sel = jnp.concatenate(
        [jnp.where(pos == lane + float(c * ROUTER_COLS), 1.0, 0.0).astype(BF16)
         for c in range(mp // ROUTER_COLS)], axis=1)
    moe = jnp.dot(sel, ys_ref[...], preferred_element_type=F32)
    out_ref[...] = _rms(x1_ref[...] + moe, gfin_ref[...])


def _combine_call(x1, ys, pos, g_final, tmb, mp):
    t = x1.shape[0]
    return pl.pallas_call(
        _combine_kernel,
        grid=(t // tmb,),
        in_specs=[
            pl.BlockSpec((tmb, D_MODEL), lambda i: (i, 0)),
            pl.BlockSpec((mp, D_MODEL), lambda i: (i, 0)),
            pl.BlockSpec((tmb, ROUTER_COLS), lambda i: (i, 0)),
            pl.BlockSpec((1, D_MODEL), lambda i: (0, 0)),
        ],
        out_specs=pl.BlockSpec((tmb, D_MODEL), lambda i: (i, 0)),
        out_shape=jax.ShapeDtypeStruct((t, D_MODEL), F32),
        compiler_params=_cparams(("arbitrary",)),
        name="combine",
    )(x1, ys, pos, g_final)


def _ssm_params(lambda_re, lambda_im, log_step, b_re, b_im, c_re, c_im):
    dt = jnp.exp(log_step)[:, None]
    mag = jnp.exp(lambda_re * dt)
    a_re = mag * jnp.cos(lambda_im * dt)
    a_im = mag * jnp.sin(lambda_im * dt)
    den = lambda_re * lambda_re + lambda_im * lambda_im
    num_re = a_re - 1.0
    z_re = (num_re * lambda_re + a_im * lambda_im) / den
    z_im = (a_im * lambda_re - num_re * lambda_im) / den
    bb_re = z_re[..., None] * b_re - z_im[..., None] * b_im
    bb_im = z_re[..., None] * b_im + z_im[..., None] * b_re
    gl = SSM_GROUPS // SUPER
    eye = jnp.eye(gl, dtype=F32)

    def in_blk(bb):
        t = bb.reshape(SUPER, gl, SSM_STATE, SSM_GROUP).transpose(0, 1, 3, 2)
        t = t[:, :, :, None, :] * eye[None, :, None, :, None]
        return t.reshape(SUPER, gl * SSM_GROUP, gl * SSM_STATE)

    def out_blk(cc):
        t = cc.reshape(SUPER, gl, SSM_GROUP, SSM_STATE).transpose(0, 1, 3, 2)
        t = t[:, :, :, None, :] * eye[None, :, None, :, None]
        return t.reshape(SUPER, gl * SSM_STATE, gl * SSM_GROUP)

    bm = jnp.concatenate([in_blk(bb_re), in_blk(bb_im)], axis=1).astype(BF16)
    cm = jnp.concatenate([out_blk(c_re), out_blk(-c_im)], axis=2).astype(BF16)
    ar_row = a_re.reshape(1, SSM_GROUPS * SSM_STATE)
    ai_row = a_im.reshape(1, SSM_GROUPS * SSM_STATE)
    ar = jnp.broadcast_to(ar_row, (8, SSM_GROUPS * SSM_STATE))
    ai = jnp.concatenate([jnp.broadcast_to(-ai_row, (4, ai_row.shape[1])),
                          jnp.broadcast_to(ai_row, (4, ai_row.shape[1]))], axis=0)
    return bm, cm, ar, ai


def _pick(n, pref):
    return pref if n % pref == 0 else n


def _mixers(x, g_mix, w_in, b_forget, b_gate, w_out_a, lambda_re, lambda_im, log_step,
            ssm_b_re, ssm_b_im, ssm_c_re, ssm_c_im, ssm_d, w_glu, b_glu, w_out_b, w_out):
    bsz, seq, _ = x.shape
    assert bsz == 4, "the SSM kernel packs (re/im) x 4 batches onto the 8 sublanes"
    t = bsz * seq
    layer = 0
    x2 = x.reshape(t, D_MODEL)

    w = w_in[layer]
    fw = FOX_WIDTH
    wq, wk, wv = w[:, :fw], w[:, fw:2 * fw], w[:, 2 * fw:3 * fw]
    wf = w[:, 3 * fw:3 * fw + HEADS]
    wu = w[:, 3 * fw + HEADS:3 * fw + HEADS + SSM_WIDTH]
    wgt = w[:, 3 * fw + HEADS + SSM_WIDTH:]
    scale = LOG2E / math.sqrt(HEAD_DIM)
    w1 = jnp.concatenate(
        [wq * scale, wk, wv, wu, wgt, jnp.pad(wf, ((0, 0), (0, LANES - HEADS)))], axis=1).astype(BF16)
    bf_pad = jnp.pad(b_forget[layer], (0, LANES - HEADS)).reshape(1, LANES)
    selq, selk, auxc = _aux_constants()
    hsel = (jnp.arange(FOX_WIDTH)[:, None] // HEAD_DIM == jnp.arange(LANES)[None, :]).astype(BF16)

    tm = _pick(seq, ATTN_TILE)
    qx, kx, vt, u, gates, stats = _proj(x2, g_mix[layer].reshape(1, D_MODEL), w1, bf_pad,
                                        b_gate[layer].reshape(1, 2 * D_MODEL), selq, selk, auxc, hsel, tm, seq)

    hb = HEADS * HEAD_BLOCK
    o = _attn(_first_key_tile(stats, ATTN_HEADS), qx.reshape(bsz, seq, hb), kx.reshape(bsz, seq, hb), vt,
              tm, ATTN_HEADS).reshape(t, FOX_WIDTH)

    bm, cm, ar, ai = _ssm_params(lambda_re[layer], lambda_im[layer], log_step[layer],
                                 ssm_b_re[layer], ssm_b_im[layer], ssm_c_re[layer], ssm_c_im[layer])
    tb = _pick(seq, 128)
    ys = _ssm(u.reshape(bsz, seq, SSM_WIDTH), bm, cm, ar, ai, ssm_d[layer].reshape(1, SSM_WIDTH),
              tb).reshape(t, SSM_WIDTH)

    x1 = _merge(x2, o, ys, gates, w_out_a[layer].astype(BF16), w_glu[layer].astype(BF16),
                b_glu[layer].reshape(1, SSM_WIDTH), w_out_b[layer].astype(BF16),
                w_out[layer].astype(BF16), tm)
    return x1, o, ys


def kernel(x, g_mix, w_in, b_forget, b_gate, w_out_a, lambda_re, lambda_im, log_step, ssm_b_re, ssm_b_im, ssm_c_re, ssm_c_im, ssm_d, w_glu, b_glu, w_out_b, w_out, g_ffn, w_router_group, b_router_group, w_router_expert, b_router_expert, w_exp_gate, w_exp_up, w_exp_down, g_final):
    bsz, seq, _ = x.shape
    t = bsz * seq
    layer = 0
    x1, _, _ = _mixers(x, g_mix, w_in, b_forget, b_gate, w_out_a, lambda_re, lambda_im, log_step,
                       ssm_b_re, ssm_b_im, ssm_c_re, ssm_c_im, ssm_d, w_glu, b_glu, w_out_b, w_out)

    wr_f = jnp.concatenate([w_router_group[layer], w_router_expert[layer]], axis=1)
    wr_f = jnp.pad(wr_f, ((0, 0), (0, ROUTER_COLS - N_GROUPS - N_EXPERTS)))
    wr_hi = wr_f.astype(BF16)
    wr_lo = (wr_f - wr_hi.astype(F32)).astype(BF16)
    wr = jnp.stack([wr_hi, wr_lo])
    br = jnp.pad(jnp.concatenate([b_router_group[layer], b_router_expert[layer]]),
                 (0, ROUTER_COLS - N_GROUPS - N_EXPERTS)).reshape(1, ROUTER_COLS)
    ff = EXPERTS_PER_GROUP * EXPERT_FF
    wd = w_exp_down[layer].reshape(N_GROUPS, ff, D_MODEL).astype(BF16)

    tmb = _pick(t, MOE_BLOCK)
    mp = tmb + N_GROUPS * MOE_CHUNK
    utri = jnp.triu(jnp.ones((tmb, tmb), BF16), 1)
    hs, cws, pos, gid = _route_call(x1, g_ffn[layer].reshape(1, D_MODEL), wr, br, utri, tmb, mp)

    cpb = mp // MOE_CHUNK
    gid_flat = gid[:, 0, :cpb].reshape(-1)
    n_chunks = gid_flat.shape[0]
    key = jnp.sort(gid_flat * n_chunks + jnp.arange(n_chunks, dtype=jnp.int32))
    ys = _experts_call(key % n_chunks, key // n_chunks, hs, cws, w_exp_gate[layer].astype(BF16),
                       w_exp_up[layer].astype(BF16), wd)
    out = _combine_call(x1, ys, pos, g_final.reshape(1, D_MODEL), tmb, mp)
    return out.reshape(bsz, seq, D_MODEL)
```

```python
import functools
import math

import jax
import jax.numpy as jnp
from jax import lax
from jax.experimental import pallas as pl
from jax.experimental.pallas import tpu as pltpu

D_MODEL = 1024
HEADS = 8
HEAD_DIM = 64
FOX_WIDTH = HEADS * HEAD_DIM
SSM_WIDTH = 512
SSM_GROUP = 16
SSM_GROUPS = 32
SSM_STATE = 64
N_GROUPS = 4
EXPERTS_PER_GROUP = 4
N_EXPERTS = 16
EXPERT_FF = 256
EPS = 1e-6

LANES = 128
HEAD_BLOCK = LANES
AUX = HEAD_BLOCK - HEAD_DIM
V_ROWS = HEAD_DIM + 16
SUPER = 4
SG_CH = SSM_WIDTH // SUPER
SG_ST = SSM_GROUPS // SUPER * SSM_STATE
ROUTER_COLS = LANES

F32 = jnp.float32
BF16 = jnp.bfloat16
VMEM_LIMIT = 56 * 1024 * 1024
LOG2E = 1.4426950408889634
ATTN_HEADS = 4
ATTN_TILE = 512
PROJ_SPLIT = 2
SKIP_LOG2 = 192.0
MOE_BLOCK = 1024
MOE_CHUNK = 128


def _cparams(sem):
    return pltpu.CompilerParams(dimension_semantics=sem, vmem_limit_bytes=VMEM_LIMIT)


def _rms(x, g):
    return x * lax.rsqrt(jnp.mean(x * x, axis=-1, keepdims=True) + EPS) * g


def _proj_kernel(x_ref, g_ref, w_ref, bf_ref, bg_ref, selq_ref, selk_ref, auxc_ref, hsel_ref,
                 q_ref, k_ref, v_ref, u_ref, gate_ref, stat_ref, carry_scr):
    tm = x_ref.shape[0]
    rows = tm // PROJ_SPLIT
    w = FOX_WIDTH

    @pl.when(pl.program_id(1) == 0)
    def _():
        carry_scr[...] = jnp.zeros_like(carry_scr)

    def head_sum(x):
        return jnp.dot(x.astype(BF16), hsel_ref[...], preferred_element_type=F32)

    carry = carry_scr[...]
    stats = []
    for part in range(PROJ_SPLIT):
        rs = slice(part * rows, (part + 1) * rows)
        h = _rms(x_ref[rs, :], g_ref[...]).astype(BF16)

        def mm(lo, hi, h=h):
            return jnp.dot(h, w_ref[:, lo:hi], preferred_element_type=F32)

        f = mm(8 * w, 8 * w + LANES) + bf_ref[...]
        lane = lax.broadcasted_iota(jnp.int32, f.shape, 1)
        row = lax.broadcasted_iota(jnp.int32, f.shape, 0)
        c = jnp.where(lane < HEADS, (jnp.minimum(f, 0.0) - jnp.log(1.0 + jnp.exp(-jnp.abs(f)))) * LOG2E, 0.0)
        sh = 1
        while sh < rows:
            c = c + jnp.where(row >= sh, pltpu.roll(c, sh, axis=0), 0.0)
            sh *= 2
        c = c + carry
        carry = c[rows - 1:rows, :]
        c1 = c.astype(BF16)
        r = c - c1.astype(F32)
        c2 = r.astype(BF16)
        c3 = (r - c2.astype(F32)).astype(BF16)

        def place(sel_ref, c1=c1, c2=c2, c3=c3):
            return (jnp.dot(c1, sel_ref[0], preferred_element_type=F32)
                    + jnp.dot(c2, sel_ref[1], preferred_element_type=F32)
                    + jnp.dot(c3, sel_ref[2], preferred_element_type=F32))

        aux_q = place(selq_ref) + auxc_ref[0:1, :]
        aux_k = place(selk_ref) + auxc_ref[1:2, :]
        aux_v = jnp.broadcast_to(auxc_ref[2:3, :], aux_q.shape)

        vals = {}
        for out_ref, lo, aux in ((q_ref, 0, aux_q), (k_ref, w, aux_k), (v_ref, 2 * w, aux_v)):
            val = mm(lo, lo + w)
            vals[lo] = val
            for hd in range(HEADS):
                blk = jnp.concatenate([val[:, hd * HEAD_DIM:(hd + 1) * HEAD_DIM],
                                       aux[:, hd * AUX:(hd + 1) * AUX]], axis=1)
                if out_ref is v_ref:
                    out_ref[hd * V_ROWS:(hd + 1) * V_ROWS, rs] = blk.T[:V_ROWS].astype(BF16)
                else:
                    out_ref[rs, hd * HEAD_BLOCK:(hd + 1) * HEAD_BLOCK] = blk.astype(BF16)

        qv, kv = vals[0], vals[w]
        stats.append((jnp.max(head_sum(qv * qv), axis=0, keepdims=True),
                      jnp.max(head_sum(kv * kv), axis=0, keepdims=True),
                      jnp.min(head_sum(qv * kv), axis=0, keepdims=True),
                      jnp.max(c, axis=0, keepdims=True),
                      jnp.min(c, axis=0, keepdims=True)))

        u_ref[rs, :] = mm(3 * w, 4 * w).astype(BF16)
        for cc in range(4):
            lo = 4 * w + cc * w
            gate_ref[rs, cc * w:(cc + 1) * w] = jax.nn.sigmoid(
                mm(lo, lo + w) + bg_ref[:, cc * w:(cc + 1) * w]).astype(BF16)

    carry_scr[...] = carry
    for idx, red in enumerate((jnp.maximum, jnp.maximum, jnp.minimum, jnp.maximum, jnp.minimum)):
        stat_ref[idx:idx + 1, :] = functools.reduce(red, [st[idx] for st in stats])
    stat_ref[5:8, :] = jnp.zeros((3, LANES), F32)


def _proj(x2, g_mix, w1, bf_pad, b_gate, selq, selk, auxc, hsel, tm, seq):
    t = x2.shape[0]
    bsz = t // seq
    nt = seq // tm
    n1 = w1.shape[1]
    hb = HEADS * HEAD_BLOCK
    row = lambda b, i: (b * nt + i, 0)
    const = lambda b, i: (0, 0)
    const3 = lambda b, i: (0, 0, 0)
    return pl.pallas_call(
        _proj_kernel,
        grid=(bsz, nt),
        in_specs=[
            pl.BlockSpec((tm, D_MODEL), row),
            pl.BlockSpec((1, D_MODEL), const),
            pl.BlockSpec((D_MODEL, n1), const),
            pl.BlockSpec((1, LANES), const),
            pl.BlockSpec((1, 2 * D_MODEL), const),
            pl.BlockSpec((3, LANES, HEADS * AUX), const3),
            pl.BlockSpec((3, LANES, HEADS * AUX), const3),
            pl.BlockSpec((8, HEADS * AUX), const),
            pl.BlockSpec((FOX_WIDTH, LANES), const),
        ],
        out_specs=[
            pl.BlockSpec((tm, hb), row),
            pl.BlockSpec((tm, hb), row),
            pl.BlockSpec((None, None, HEADS * V_ROWS, tm), lambda b, i: (b, i, 0, 0)),
            pl.BlockSpec((tm, SSM_WIDTH), row),
            pl.BlockSpec((tm, 2 * D_MODEL), row),
            pl.BlockSpec((None, None, 8, LANES), lambda b, i: (b, i, 0, 0)),
        ],
        out_shape=[
            jax.ShapeDtypeStruct((t, hb), BF16),
            jax.ShapeDtypeStruct((t, hb), BF16),
            jax.ShapeDtypeStruct((bsz, nt, HEADS * V_ROWS, tm), BF16),
            jax.ShapeDtypeStruct((t, SSM_WIDTH), BF16),
            jax.ShapeDtypeStruct((t, 2 * D_MODEL), BF16),
            jax.ShapeDtypeStruct((bsz, nt, 8, LANES), F32),
        ],
        scratch_shapes=[pltpu.VMEM((1, LANES), F32)],
        compiler_params=_cparams(("arbitrary", "arbitrary")),
        name="proj",
    )(x2, g_mix, w1, bf_pad, b_gate, selq, selk, auxc, hsel)


def _aux_constants():
    selq = jnp.zeros((3, LANES, HEADS * AUX), F32)
    selk = jnp.zeros((3, LANES, HEADS * AUX), F32)
    auxc = jnp.zeros((8, HEADS * AUX), F32)
    hd = jnp.arange(HEADS)
    for p in range(3):
        selq = selq.at[p, hd, hd * AUX + p].set(1.0)
        selk = selk.at[p, hd, hd * AUX + 3 + p].set(-1.0)
        auxc = auxc.at[0, hd * AUX + 3 + p].set(1.0)
        auxc = auxc.at[1, hd * AUX + p].set(1.0)
    auxc = auxc.at[2, hd * AUX].set(1.0)
    return selq.astype(BF16), selk.astype(BF16), auxc


def _attn_kernel(first_ref, q_ref, k_ref, vt_ref, o_ref, s_scr, m_scr, acc_scr, *, tile, heads):
    i = pl.program_id(2)
    j0 = first_ref[(pl.program_id(0) * pl.num_programs(1) + pl.program_id(1)) * pl.num_programs(2) + i]
    cnt = i - j0
    nt = (((1,), (1,)), ((), ()))

    m_scr[...] = jnp.full(m_scr.shape, -jnp.inf, F32)
    acc_scr[...] = jnp.zeros(acc_scr.shape, F32)

    def qk(slot, t):
        r0 = pl.multiple_of(t * tile, tile)
        for g in range(heads):
            s_scr[slot, g] = lax.dot_general(
                k_ref[pl.ds(r0, tile), g * HEAD_BLOCK:(g + 1) * HEAD_BLOCK],
                q_ref[:, g * HEAD_BLOCK:(g + 1) * HEAD_BLOCK], nt, preferred_element_type=F32)

    def softmax(slot, masked):
        ps, alphas = [], []
        for g in range(heads):
            s = s_scr[slot, g]
            if masked:
                kpos = lax.broadcasted_iota(jnp.int32, s.shape, 0)
                qpos = lax.broadcasted_iota(jnp.int32, s.shape, 1)
                s = jnp.where(kpos <= qpos, s, -jnp.inf)
            m = m_scr[g]
            m_new = jnp.maximum(m, jnp.max(s, axis=0, keepdims=True))
            ps.append(jnp.exp2(s - m_new).astype(BF16))
            alphas.append(jnp.exp2(m - m_new))
            m_scr[g] = m_new
        return ps, alphas

    def pv(pa, t):
        ps, alphas = pa
        for g in range(heads):
            acc_scr[g] = (alphas[g] * acc_scr[g]
                          + jnp.dot(vt_ref[t, g * V_ROWS:(g + 1) * V_ROWS, :], ps[g],
                                    preferred_element_type=F32))

    @pl.when(cnt == 2)
    def _():
        qk(0, j0)
        qk(1, j0 + 1)
        pa0 = softmax(0, False)
        qk(2, i)
        pv(pa0, j0)
        pv(softmax(1, False), j0 + 1)
        pv(softmax(2, True), i)

    @pl.when(cnt != 2)
    def _():
        qk(0, j0)

        def body(tt, carry):
            t0 = j0 + 2 * tt
            qk(1, t0 + 1)
            pa0 = softmax(0, False)
            qk(0, t0 + 2)
            pv(pa0, t0)
            pv(softmax(1, False), t0 + 1)
            return carry

        lax.fori_loop(0, lax.shift_right_logical(cnt, 1), body, 0)

        @pl.when((cnt & 1) == 0)
        def _():
            pv(softmax(0, True), i)

        @pl.when((cnt & 1) == 1)
        def _():
            qk(1, i)
            pv(softmax(0, False), i - 1)
            pv(softmax(1, True), i)

    for pair in range(heads // 2):
        a0 = acc_scr[2 * pair]
        a1 = acc_scr[2 * pair + 1]
        o2 = jnp.concatenate([a0[:HEAD_DIM] / a0[HEAD_DIM:HEAD_DIM + 1],
                              a1[:HEAD_DIM] / a1[HEAD_DIM:HEAD_DIM + 1]], axis=0)
        o_ref[:, pair * LANES:(pair + 1) * LANES] = o2.T.astype(BF16)


def _attn(first, qx, kx, vt, tile, heads):
    b, s, _ = qx.shape
    n = s // tile
    assert vt.shape == (b, n, HEADS * V_ROWS, tile)
    hb = heads * HEAD_BLOCK
    return pl.pallas_call(
        functools.partial(_attn_kernel, tile=tile, heads=heads),
        grid_spec=pltpu.PrefetchScalarGridSpec(
            num_scalar_prefetch=1,
            grid=(b, HEADS // heads, n),
            in_specs=[
                pl.BlockSpec((None, tile, hb), lambda bi, hi, i, first_ref: (bi, i, hi)),
                pl.BlockSpec((None, s, hb), lambda bi, hi, i, first_ref: (bi, 0, hi)),
                pl.BlockSpec((None, n, heads * V_ROWS, tile), lambda bi, hi, i, first_ref: (bi, 0, hi, 0)),
            ],
            out_specs=pl.BlockSpec((None, tile, heads * HEAD_DIM), lambda bi, hi, i, first_ref: (bi, i, hi)),
            scratch_shapes=[
                pltpu.VMEM((3, heads, tile, tile), F32),
                pltpu.VMEM((heads, 1, tile), F32),
                pltpu.VMEM((heads, V_ROWS, tile), F32),
            ],
        ),
        out_shape=jax.ShapeDtypeStruct((b, s, FOX_WIDTH), BF16),
        compiler_params=_cparams(("arbitrary", "arbitrary", "arbitrary")),
        name="attn",
    )(first, qx, kx, vt)


def _first_key_tile(stats, heads):
    st = stats[..., :HEADS]
    slack = 1.0 + 2.0 ** -6
    qn = jnp.sqrt(st[:, :, 0]) * slack
    kn = jnp.sqrt(st[:, :, 1]) * slack
    self_min, c_max, c_min = st[:, :, 2], st[:, :, 3], st[:, :, 4]
    upper = qn[:, :, None] * kn[:, None, :] + c_max[:, :, None] - c_min[:, None, :]
    lower = self_min - (slack - 1.0) * qn * kn
    needed = upper - lower[:, :, None] > -SKIP_LOG2
    n = st.shape[1]
    idx = jnp.arange(n, dtype=jnp.int32)
    needed = needed | (idx[None, None, :, None] >= idx[None, :, None, None])
    needed = needed.reshape(needed.shape[:3] + (HEADS // heads, heads)).any(axis=-1)
    first = jnp.min(jnp.where(needed, idx[None, None, :, None], n), axis=2)
    return first.transpose(0, 2, 1).reshape(-1).astype(jnp.int32)


def _ssm_kernel(u_ref, bm_ref, cm_ref, ar_ref, ai_ref, d_ref, y_ref, io_scr, x_scr, s_scr, *, tb):
    rows = tb * 8
    nb = u_ref.shape[0]

    @pl.when(pl.program_id(0) == 0)
    def _():
        s_scr[...] = jnp.zeros_like(s_scr)

    for g in range(SUPER):
        for b in range(nb):
            ub = u_ref[b, :, g * SG_CH:(g + 1) * SG_CH].astype(F32)
            io_scr[g, pl.ds(b, tb, stride=8), :] = ub
            io_scr[g, pl.ds(b + nb, tb, stride=8), :] = ub
    is_re = (lax.broadcasted_iota(jnp.int32, (rows, SG_CH), 0) % 8) < 4
    zero = jnp.zeros((rows, SG_CH), BF16)
    for g in range(SUPER):
        ug = io_scr[g].astype(BF16)
        lhs = jnp.concatenate([jnp.where(is_re, ug, zero), jnp.where(is_re, zero, ug)], axis=1)
        x_scr[:, g * SG_ST:(g + 1) * SG_ST] = jnp.dot(lhs, bm_ref[g], preferred_element_type=F32)

    half = 2 * SG_ST
    for c in range(SUPER * SG_ST // half):
        lo = c * half
        ar = ar_ref[:, lo:lo + half]
        ai = ai_ref[:, lo:lo + half]

        def body(t, s, lo=lo, ar=ar, ai=ai):
            r0 = pl.multiple_of(t * 8, 8)
            s = ar * s + ai * pltpu.roll(s, 4, axis=0) + x_scr[pl.ds(r0, 8), lo:lo + half]
            x_scr[pl.ds(r0, 8), lo:lo + half] = s
            return s

        s_scr[:, lo:lo + half] = lax.fori_loop(0, tb, body, s_scr[:, lo:lo + half], unroll=4)

    for g in range(SUPER):
        st = x_scr[:, g * SG_ST:(g + 1) * SG_ST].astype(BF16)
        o = jnp.dot(st, cm_ref[g], preferred_element_type=F32)
        y = o[:, :SG_CH] + pltpu.roll(o[:, SG_CH:], rows - 4, axis=0)
        io_scr[g] = y + d_ref[:, g * SG_CH:(g + 1) * SG_CH] * io_scr[g]
        for b in range(nb):
            y_ref[b, :, g * SG_CH:(g + 1) * SG_CH] = io_scr[g, pl.ds(b, tb, stride=8), :].astype(BF16)


def _ssm(u, bm, cm, ar, ai, d, tb):
    bsz, seq, _ = u.shape
    rows = tb * 8
    return pl.pallas_call(
        functools.partial(_ssm_kernel, tb=tb),
        grid=(seq // tb,),
        in_specs=[
            pl.BlockSpec((bsz, tb, SSM_WIDTH), lambda i: (0, i, 0)),
            pl.BlockSpec((SUPER, 2 * SG_CH, SG_ST), lambda i: (0, 0, 0)),
            pl.BlockSpec((SUPER, SG_ST, 2 * SG_CH), lambda i: (0, 0, 0)),
            pl.BlockSpec((8, SUPER * SG_ST), lambda i: (0, 0)),
            pl.BlockSpec((8, SUPER * SG_ST), lambda i: (0, 0)),
            pl.BlockSpec((1, SSM_WIDTH), lambda i: (0, 0)),
        ],
        out_specs=pl.BlockSpec((bsz, tb, SSM_WIDTH), lambda i: (0, i, 0)),
        out_shape=jax.ShapeDtypeStruct((bsz, seq, SSM_WIDTH), BF16),
        scratch_shapes=[
            pltpu.VMEM((SUPER, rows, SG_CH), F32),
            pltpu.VMEM((rows, SUPER * SG_ST), F32),
            pltpu.VMEM((8, SUPER * SG_ST), F32),
        ],
        compiler_params=_cparams(("arbitrary",)),
        name="ssm",
    )(u, bm, cm, ar, ai, d)


def _merge_kernel(x_ref, o_ref, ys_ref, gate_ref, woa_ref, wglu_ref, bglu_ref, wob_ref, wout_ref,
                  x1_ref):
    y_a = jnp.dot(o_ref[...], woa_ref[...], preferred_element_type=F32)
    z = jax.nn.gelu(ys_ref[...].astype(F32))
    zg = jnp.dot(z.astype(BF16), wglu_ref[...], preferred_element_type=F32) + bglu_ref[...]
    z = z * jax.nn.sigmoid(zg)
    y_b = jnp.dot(z.astype(BF16), wob_ref[...], preferred_element_type=F32)
    g0 = gate_ref[:, :D_MODEL].astype(F32)
    g1 = gate_ref[:, D_MODEL:].astype(F32)
    merged = (g0 * y_a + g1 * y_b).astype(BF16)
    x1_ref[...] = x_ref[...] + jnp.dot(merged, wout_ref[...], preferred_element_type=F32)


def _merge(x2, o, ys, gates, woa, wglu, bglu, wob, wout, tm):
    t = x2.shape[0]
    row = lambda i: (i, 0)
    const = lambda i: (0, 0)
    return pl.pallas_call(
        _merge_kernel,
        grid=(t // tm,),
        in_specs=[
            pl.BlockSpec((tm, D_MODEL), row),
            pl.BlockSpec((tm, FOX_WIDTH), row),
            pl.BlockSpec((tm, SSM_WIDTH), row),
            pl.BlockSpec((tm, 2 * D_MODEL), row),
            pl.BlockSpec((FOX_WIDTH, D_MODEL), const),
            pl.BlockSpec((SSM_WIDTH, SSM_WIDTH), const),
            pl.BlockSpec((1, SSM_WIDTH), const),
            pl.BlockSpec((SSM_WIDTH, D_MODEL), const),
            pl.BlockSpec((D_MODEL, D_MODEL), const),
        ],
        out_specs=pl.BlockSpec((tm, D_MODEL), row),
        out_shape=jax.ShapeDtypeStruct((t, D_MODEL), F32),
        compiler_params=_cparams(("arbitrary",)),
        name="merge",
    )(x2, o, ys, gates, woa, wglu, bglu, wob, wout)


def _route(lt):
    ri = lax.broadcasted_iota(jnp.int32, lt.shape, 0)
    r8 = lax.broadcasted_iota(jnp.int32, (8, lt.shape[1]), 0)
    neg = -jnp.inf
    big = jnp.int32(ROUTER_COLS)
    lg = jnp.where(ri < N_GROUPS, lt, neg)
    gmax = jnp.max(lg, axis=0, keepdims=True)
    p_sel = 1.0 / jnp.sum(jnp.exp(lg - gmax), axis=0, keepdims=True)
    g_idx = jnp.min(jnp.where(lg == gmax, ri, big), axis=0, keepdims=True)
    lo = N_GROUPS + g_idx * EXPERTS_PER_GROUP
    le = jnp.where((ri >= lo) & (ri < lo + EXPERTS_PER_GROUP), lt, neg)
    v1 = jnp.max(le, axis=0, keepdims=True)
    i1 = jnp.min(jnp.where(le == v1, ri, big), axis=0, keepdims=True)
    le2 = jnp.where(ri == i1, neg, le)
    v2 = jnp.max(le2, axis=0, keepdims=True)
    i2 = jnp.min(jnp.where(le2 == v2, ri, big), axis=0, keepdims=True)
    e2 = jnp.exp(v2 - v1)
    w1 = p_sel / (1.0 + e2)
    w2 = p_sel * e2 / (1.0 + e2)
    cw = jnp.where(r8 == i1 - lo, w1, 0.0) + jnp.where(r8 == i2 - lo, w2, 0.0)
    onehot = jnp.where(r8 == g_idx, 1.0, 0.0)
    return cw, onehot


def _route_kernel(x1_ref, gf_ref, wr_ref, br_ref, utri_ref, hs_ref, cws_ref, pos_ref, gid_ref):
    tmb = x1_ref.shape[0]
    mp = hs_ref.shape[0]
    h = _rms(x1_ref[...], gf_ref[...])
    h_hi = h.astype(BF16)
    h_lo = (h - h_hi.astype(F32)).astype(BF16)
    l_hi = jnp.dot(h_hi, jnp.concatenate([wr_ref[0], wr_ref[1]], axis=1), preferred_element_type=F32)
    logits = (l_hi[:, :ROUTER_COLS] + l_hi[:, ROUTER_COLS:]
              + jnp.dot(h_lo, wr_ref[0], preferred_element_type=F32)) + br_ref[...]
    cw4, onehot = _route(logits.T[:32])

    r8 = lax.broadcasted_iota(jnp.int32, (8, ROUTER_COLS), 0)
    counts = jnp.broadcast_to(jnp.sum(onehot, axis=1, keepdims=True), (8, ROUTER_COLS))
    nch = jnp.floor((counts + (MOE_CHUNK - 1)) * (1.0 / MOE_CHUNK))
    inc = nch
    for sh in (1, 2):
        inc = inc + jnp.where(r8 >= sh, pltpu.roll(inc, sh, axis=0), 0.0)
    start = (inc - nch) * MOE_CHUNK
    rank = jnp.dot(onehot.astype(BF16), utri_ref[...], preferred_element_type=F32)
    pos_row = jnp.sum(onehot * (start[:, 0:1] + rank), axis=0, keepdims=True)
    pos_ref[...] = jnp.broadcast_to(pos_row, (ROUTER_COLS, tmb)).T
    prow = lax.broadcasted_iota(jnp.int32, (mp, tmb), 0).astype(F32)
    perm = jnp.where(prow == pos_row, 1.0, 0.0).astype(BF16)
    hs_ref[...] = jnp.dot(perm, h_hi, preferred_element_type=F32).astype(BF16)
    cw_hi = cw4.astype(BF16).astype(F32)
    cw_t = jnp.concatenate([cw_hi, cw4 - cw_hi, jnp.zeros((ROUTER_COLS - 16, tmb), F32)], axis=0).astype(BF16)
    cw2 = lax.dot_general(perm, cw_t, (((1,), (1,)), ((), ())), preferred_element_type=F32)
    cws_ref[...] = cw2 + pltpu.roll(cw2, ROUTER_COLS - 8, axis=1)
    chunk = lax.broadcasted_iota(jnp.int32, (8, ROUTER_COLS), 1).astype(F32)
    gid = jnp.zeros((8, ROUTER_COLS), F32)
    for grp in range(N_GROUPS):
        end = jnp.sum(jnp.where(r8 == grp, inc, 0.0), axis=0, keepdims=True)
        gid = gid + jnp.where(chunk >= end, 1.0, 0.0)
    gid_ref[...] = gid.astype(jnp.int32)


def _route_call(x1, g_ffn, wr, br, utri, tmb, mp):
    t = x1.shape[0]
    nb = t // tmb
    return pl.pallas_call(
        _route_kernel,
        grid=(nb,),
        in_specs=[
            pl.BlockSpec((tmb, D_MODEL), lambda i: (i, 0)),
            pl.BlockSpec((1, D_MODEL), lambda i: (0, 0)),
            pl.BlockSpec((2, D_MODEL, ROUTER_COLS), lambda i: (0, 0, 0)),
            pl.BlockSpec((1, ROUTER_COLS), lambda i: (0, 0)),
            pl.BlockSpec((tmb, tmb), lambda i: (0, 0)),
        ],
        out_specs=[
            pl.BlockSpec((mp, D_MODEL), lambda i: (i, 0)),
            pl.BlockSpec((mp, ROUTER_COLS), lambda i: (i, 0)),
            pl.BlockSpec((tmb, ROUTER_COLS), lambda i: (i, 0)),
            pl.BlockSpec((None, 8, ROUTER_COLS), lambda i: (i, 0, 0)),
        ],
        out_shape=[
            jax.ShapeDtypeStruct((nb * mp, D_MODEL), BF16),
            jax.ShapeDtypeStruct((nb * mp, ROUTER_COLS), F32),
            jax.ShapeDtypeStruct((t, ROUTER_COLS), F32),
            jax.ShapeDtypeStruct((nb, 8, ROUTER_COLS), jnp.int32),
        ],
        compiler_params=_cparams(("arbitrary",)),
        name="route",
    )(x1, g_ffn, wr, br, utri)


def _experts_kernel(order_ref, gids_ref, hs_ref, cws_ref, wg_ref, wu_ref, wd_ref, ys_ref):
    valid = gids_ref[pl.program_id(0)] < N_GROUPS

    @pl.when(valid)
    def _():
        x = hs_ref[...]
        cw = cws_ref[...]
        acts = []
        for j in range(EXPERTS_PER_GROUP):
            hg = jnp.dot(x, wg_ref[j], preferred_element_type=F32)
            hu = jnp.dot(x, wu_ref[j], preferred_element_type=F32)
            acts.append((hg * jax.nn.sigmoid(hg) * hu * cw[:, j:j + 1]).astype(BF16))
        act = jnp.concatenate(acts, axis=1)
        ys_ref[...] = jnp.dot(act, wd_ref[...], preferred_element_type=F32).astype(BF16)

    @pl.when(jnp.logical_not(valid))
    def _():
        ys_ref[...] = jnp.zeros(ys_ref.shape, BF16)


def _experts_call(order, gids, hs, cws, wg, wu, wd):
    n_chunks = order.shape[0]
    ff = EXPERTS_PER_GROUP * EXPERT_FF
    grp = lambda s, order_ref, gids_ref: (jnp.minimum(gids_ref[s], N_GROUPS - 1), 0, 0)
    rows = lambda s, order_ref, gids_ref: (order_ref[s], 0)
    return pl.pallas_call(
        _experts_kernel,
        grid_spec=pltpu.PrefetchScalarGridSpec(
            num_scalar_prefetch=2,
            grid=(n_chunks,),
            in_specs=[
                pl.BlockSpec((MOE_CHUNK, D_MODEL), rows),
                pl.BlockSpec((MOE_CHUNK, ROUTER_COLS), rows),
                pl.BlockSpec((EXPERTS_PER_GROUP, D_MODEL, EXPERT_FF), grp),
                pl.BlockSpec((EXPERTS_PER_GROUP, D_MODEL, EXPERT_FF), grp),
                pl.BlockSpec((None, ff, D_MODEL), grp),
            ],
            out_specs=pl.BlockSpec((MOE_CHUNK, D_MODEL), rows),
        ),
        out_shape=jax.ShapeDtypeStruct(hs.shape, BF16),
        compiler_params=_cparams(("arbitrary",)),
        name="experts",
    )(order, gids, hs, cws, wg, wu, wd)


def _combine_kernel(x1_ref, ys_ref, pos_ref, gfin_ref, out_ref):
    tmb = x1_ref.shape[0]
    mp = ys_ref.shape[0]
    pos = pos_ref[...]
    lane = lax.broadcasted_iota(jnp.int32, pos.shape, 1).astype(F32)
    del tmb
    sel = jnp.concatenate(
        [jnp.where(pos == lane + float(c * ROUTER_COLS), 1.0, 0.0).astype(BF16)
         for c in range(mp // ROUTER_COLS)], axis=1)
    moe = jnp.dot(sel, ys_ref[...], preferred_element_type=F32)
    out_ref[...] = _rms(x1_ref[...] + moe, gfin_ref[...])


def _combine_call(x1, ys, pos, g_final, tmb, mp):
    t = x1.shape[0]
    return pl.pallas_call(
        _combine_kernel,
        grid=(t // tmb,),
        in_specs=[
            pl.BlockSpec((tmb, D_MODEL), lambda i: (i, 0)),
            pl.BlockSpec((mp, D_MODEL), lambda i: (i, 0)),
            pl.BlockSpec((tmb, ROUTER_COLS), lambda i: (i, 0)),
            pl.BlockSpec((1, D_MODEL), lambda i: (0, 0)),
        ],
        out_specs=pl.BlockSpec((tmb, D_MODEL), lambda i: (i, 0)),
        out_shape=jax.ShapeDtypeStruct((t, D_MODEL), F32),
        compiler_params=_cparams(("arbitrary",)),
        name="combine",
    )(x1, ys, pos, g_final)


def _ssm_params(lambda_re, lambda_im, log_step, b_re, b_im, c_re, c_im):
    dt = jnp.exp(log_step)[:, None]
    mag = jnp.exp(lambda_re * dt)
    a_re = mag * jnp.cos(lambda_im * dt)
    a_im = mag * jnp.sin(lambda_im * dt)
    den = lambda_re * lambda_re + lambda_im * lambda_im
    num_re = a_re - 1.0
    z_re = (num_re * lambda_re + a_im * lambda_im) / den
    z_im = (a_im * lambda_re - num_re * lambda_im) / den
    bb_re = z_re[..., None] * b_re - z_im[..., None] * b_im
    bb_im = z_re[..., None] * b_im + z_im[..., None] * b_re
    gl = SSM_GROUPS // SUPER
    eye = jnp.eye(gl, dtype=F32)

    def in_blk(bb):
        t = bb.reshape(SUPER, gl, SSM_STATE, SSM_GROUP).transpose(0, 1, 3, 2)
        t = t[:, :, :, None, :] * eye[None, :, None, :, None]
        return t.reshape(SUPER, gl * SSM_GROUP, gl * SSM_STATE)

    def out_blk(cc):
        t = cc.reshape(SUPER, gl, SSM_GROUP, SSM_STATE).transpose(0, 1, 3, 2)
        t = t[:, :, :, None, :] * eye[None, :, None, :, None]
        return t.reshape(SUPER, gl * SSM_STATE, gl * SSM_GROUP)

    bm = jnp.concatenate([in_blk(bb_re), in_blk(bb_im)], axis=1).astype(BF16)
    cm = jnp.concatenate([out_blk(c_re), out_blk(-c_im)], axis=2).astype(BF16)
    ar_row = a_re.reshape(1, SSM_GROUPS * SSM_STATE)
    ai_row = a_im.reshape(1, SSM_GROUPS * SSM_STATE)
    ar = jnp.broadcast_to(ar_row, (8, SSM_GROUPS * SSM_STATE))
    ai = jnp.concatenate([jnp.broadcast_to(-ai_row, (4, ai_row.shape[1])),
                          jnp.broadcast_to(ai_row, (4, ai_row.shape[1]))], axis=0)
    return bm, cm, ar, ai


def _pick(n, pref):
    return pref if n % pref == 0 else n


def _mixers(x, g_mix, w_in, b_forget, b_gate, w_out_a, lambda_re, lambda_im, log_step,
            ssm_b_re, ssm_b_im, ssm_c_re, ssm_c_im, ssm_d, w_glu, b_glu, w_out_b, w_out):
    bsz, seq, _ = x.shape
    assert bsz == 4, "the SSM kernel packs (re/im) x 4 batches onto the 8 sublanes"
    t = bsz * seq
    layer = 0
    x2 = x.reshape(t, D_MODEL)

    w = w_in[layer]
    fw = FOX_WIDTH
    wq, wk, wv = w[:, :fw], w[:, fw:2 * fw], w[:, 2 * fw:3 * fw]
    wf = w[:, 3 * fw:3 * fw + HEADS]
    wu = w[:, 3 * fw + HEADS:3 * fw + HEADS + SSM_WIDTH]
    wgt = w[:, 3 * fw + HEADS + SSM_WIDTH:]
    scale = LOG2E / math.sqrt(HEAD_DIM)
    w1 = jnp.concatenate(
        [wq * scale, wk, wv, wu, wgt, jnp.pad(wf, ((0, 0), (0, LANES - HEADS)))], axis=1).astype(BF16)
    bf_pad = jnp.pad(b_forget[layer], (0, LANES - HEADS)).reshape(1, LANES)
    selq, selk, auxc = _aux_constants()
    hsel = (jnp.arange(FOX_WIDTH)[:, None] // HEAD_DIM == jnp.arange(LANES)[None, :]).astype(BF16)

    tm = _pick(seq, ATTN_TILE)
    qx, kx, vt, u, gates, stats = _proj(x2, g_mix[layer].reshape(1, D_MODEL), w1, bf_pad,
                                        b_gate[layer].reshape(1, 2 * D_MODEL), selq, selk, auxc, hsel, tm, seq)

    hb = HEADS * HEAD_BLOCK
    o = _attn(_first_key_tile(stats, ATTN_HEADS), qx.reshape(bsz, seq, hb), kx.reshape(bsz, seq, hb), vt,
              tm, ATTN_HEADS).reshape(t, FOX_WIDTH)

    bm, cm, ar, ai = _ssm_params(lambda_re[layer], lambda_im[layer], log_step[layer],
                                 ssm_b_re[layer], ssm_b_im[layer], ssm_c_re[layer], ssm_c_im[layer])
    tb = _pick(seq, 128)
    ys = _ssm(u.reshape(bsz, seq, SSM_WIDTH), bm, cm, ar, ai, ssm_d[layer].reshape(1, SSM_WIDTH),
              tb).reshape(t, SSM_WIDTH)

    x1 = _merge(x2, o, ys, gates, w_out_a[layer].astype(BF16), w_glu[layer].astype(BF16),
                b_glu[layer].reshape(1, SSM_WIDTH), w_out_b[layer].astype(BF16),
                w_out[layer].astype(BF16), tm)
    return x1, o, ys


def kernel(x, g_mix, w_in, b_forget, b_gate, w_out_a, lambda_re, lambda_im, log_step, ssm_b_re, ssm_b_im, ssm_c_re, ssm_c_im, ssm_d, w_glu, b_glu, w_out_b, w_out, g_ffn, w_router_group, b_router_group, w_router_expert, b_router_expert, w_exp_gate, w_exp_up, w_exp_down, g_final):
    bsz, seq, _ = x.shape
    t = bsz * seq
    layer = 0
    x1, _, _ = _mixers(x, g_mix, w_in, b_forget, b_gate, w_out_a, lambda_re, lambda_im, log_step,
                       ssm_b_re, ssm_b_im, ssm_c_re, ssm_c_im, ssm_d, w_glu, b_glu, w_out_b, w_out)

    wr_f = jnp.concatenate([w_router_group[layer], w_router_expert[layer]], axis=1)
    wr_f = jnp.pad(wr_f, ((0, 0), (0, ROUTER_COLS - N_GROUPS - N_EXPERTS)))
    wr_hi = wr_f.astype(BF16)
    wr_lo = (wr_f - wr_hi.astype(F32)).astype(BF16)
    wr = jnp.stack([wr_hi, wr_lo])
    br = jnp.pad(jnp.concatenate([b_router_group[layer], b_router_expert[layer]]),
                 (0, ROUTER_COLS - N_GROUPS - N_EXPERTS)).reshape(1, ROUTER_COLS)
    ff = EXPERTS_PER_GROUP * EXPERT_FF
    wd = w_exp_down[layer].reshape(N_GROUPS, ff, D_MODEL).astype(BF16)

    tmb = _pick(t, MOE_BLOCK)
    mp = tmb + N_GROUPS * MOE_CHUNK
    utri = jnp.triu(jnp.ones((tmb, tmb), BF16), 1)
    hs, cws, pos, gid = _route_call(x1, g_ffn[layer].reshape(1, D_MODEL), wr, br, utri, tmb, mp)

    cpb = mp // MOE_CHUNK
    gid_flat = gid[:, 0, :cpb].reshape(-1)
    n_chunks = gid_flat.shape[0]
    key = jnp.sort(gid_flat * n_chunks + jnp.arange(n_chunks, dtype=jnp.int32))
    ys = _experts_call(key % n_chunks, key // n_chunks, hs, cws, w_exp_gate[layer].astype(BF16),
                       w_exp_up[layer].astype(BF16), wd)
    out = _combine_call(x1, ys, pos, g_final.reshape(1, D_MODEL), tmb, mp)
    return out.reshape(bsz, seq, D_MODEL)
```

```python
import functools
import math

import jax
import jax.numpy as jnp
from jax import lax
from jax.experimental import pallas as pl
from jax.experimental.pallas import tpu as pltpu

D_MODEL = 1024
HEADS = 8
HEAD_DIM = 64
FOX_WIDTH = HEADS * HEAD_DIM
SSM_WIDTH = 512
SSM_GROUP = 16
SSM_GROUPS = 32
SSM_STATE = 64
N_GROUPS = 4
EXPERTS_PER_GROUP = 4
N_EXPERTS = 16
EXPERT_FF = 256
EPS = 1e-6

LANES = 128
HEAD_BLOCK = LANES
AUX = HEAD_BLOCK - HEAD_DIM
V_ROWS = HEAD_DIM + 16
SUPER = 4
SG_CH = SSM_WIDTH // SUPER
SG_ST = SSM_GROUPS // SUPER * SSM_STATE
ROUTER_COLS = LANES

F32 = jnp.float32
BF16 = jnp.bfloat16
VMEM_LIMIT = 56 * 1024 * 1024
LOG2E = 1.4426950408889634
ATTN_HEADS = 4
ATTN_TILE = 512
PROJ_SPLIT = 2
SKIP_LOG2 = 192.0
MOE_BLOCK = 1024
MOE_CHUNK = 128


def _cparams(sem):
    return pltpu.CompilerParams(dimension_semantics=sem, vmem_limit_bytes=VMEM_LIMIT)


def _rms(x, g):
    return x * lax.rsqrt(jnp.mean(x * x, axis=-1, keepdims=True) + EPS) * g


def _proj_kernel(x_ref, g_ref, w_ref, bf_ref, bg_ref, selq_ref, selk_ref, auxc_ref, hsel_ref,
                 q_ref, k_ref, v_ref, u_ref, gate_ref, stat_ref, carry_scr):
    tm = x_ref.shape[0]
    rows = tm // PROJ_SPLIT
    w = FOX_WIDTH

    @pl.when(pl.program_id(1) == 0)
    def _():
        carry_scr[...] = jnp.zeros_like(carry_scr)

    def head_sum(x):
        return jnp.dot(x.astype(BF16), hsel_ref[...], preferred_element_type=F32)

    carry = carry_scr[...]
    stats = []
    for part in range(PROJ_SPLIT):
        rs = slice(part * rows, (part + 1) * rows)
        h = _rms(x_ref[rs, :], g_ref[...]).astype(BF16)

        def mm(lo, hi, h=h):
            return jnp.dot(h, w_ref[:, lo:hi], preferred_element_type=F32)

        f = mm(8 * w, 8 * w + LANES) + bf_ref[...]
        lane = lax.broadcasted_iota(jnp.int32, f.shape, 1)
        row = lax.broadcasted_iota(jnp.int32, f.shape, 0)
        c = jnp.where(lane < HEADS, (jnp.minimum(f, 0.0) - jnp.log(1.0 + jnp.exp(-jnp.abs(f)))) * LOG2E, 0.0)
        sh = 1
        while sh < rows:
            c = c + jnp.where(row >= sh, pltpu.roll(c, sh, axis=0), 0.0)
            sh *= 2
        c = c + carry
        carry = c[rows - 1:rows, :]
        c1 = c.astype(BF16)
        r = c - c1.astype(F32)
        c2 = r.astype(BF16)
        c3 = (r - c2.astype(F32)).astype(BF16)

        def place(sel_ref, c1=c1, c2=c2, c3=c3):
            return (jnp.dot(c1, sel_ref[0], preferred_element_type=F32)
                    + jnp.dot(c2, sel_ref[1], preferred_element_type=F32)
                    + jnp.dot(c3, sel_ref[2], preferred_element_type=F32))

        aux_q = place(selq_ref) + auxc_ref[0:1, :]
        aux_k = place(selk_ref) + auxc_ref[1:2, :]
        aux_v = jnp.broadcast_to(auxc_ref[2:3, :], aux_q.shape)

        vals = {}
        for out_ref, lo, aux in ((q_ref, 0, aux_q), (k_ref, w, aux_k), (v_ref, 2 * w, aux_v)):
            val = mm(lo, lo + w)
            vals[lo] = val
            for hd in range(HEADS):
                blk = jnp.concatenate([val[:, hd * HEAD_DIM:(hd + 1) * HEAD_DIM],
                                       aux[:, hd * AUX:(hd + 1) * AUX]], axis=1)
                if out_ref is v_ref:
                    out_ref[hd * V_ROWS:(hd + 1) * V_ROWS, rs] = blk.T[:V_ROWS].astype(BF16)
                else:
                    out_ref[rs, hd * HEAD_BLOCK:(hd + 1) * HEAD_BLOCK] = blk.astype(BF16)

        qv, kv = vals[0], vals[w]
        stats.append((jnp.max(head_sum(qv * qv), axis=0, keepdims=True),
                      jnp.max(head_sum(kv * kv), axis=0, keepdims=True),
                      jnp.min(head_sum(qv * kv), axis=0, keepdims=True),
                      jnp.max(c, axis=0, keepdims=True),
                      jnp.min(c, axis=0, keepdims=True)))

        u_ref[rs, :] = mm(3 * w, 4 * w).astype(BF16)
        for cc in range(4):
            lo = 4 * w + cc * w
            gate_ref[rs, cc * w:(cc + 1) * w] = jax.nn.sigmoid(
                mm(lo, lo + w) + bg_ref[:, cc * w:(cc + 1) * w]).astype(BF16)

    carry_scr[...] = carry
    for idx, red in enumerate((jnp.maximum, jnp.maximum, jnp.minimum, jnp.maximum, jnp.minimum)):
        stat_ref[idx:idx + 1, :] = functools.reduce(red, [st[idx] for st in stats])
    stat_ref[5:8, :] = jnp.zeros((3, LANES), F32)


def _proj(x2, g_mix, w1, bf_pad, b_gate, selq, selk, auxc, hsel, tm, seq):
    t = x2.shape[0]
    bsz = t // seq
    nt = seq // tm
    n1 = w1.shape[1]
    hb = HEADS * HEAD_BLOCK
    row = lambda b, i: (b * nt + i, 0)
    const = lambda b, i: (0, 0)
    const3 = lambda b, i: (0, 0, 0)
    return pl.pallas_call(
        _proj_kernel,
        grid=(bsz, nt),
        in_specs=[
            pl.BlockSpec((tm, D_MODEL), row),
            pl.BlockSpec((1, D_MODEL), const),
            pl.BlockSpec((D_MODEL, n1), const),
            pl.BlockSpec((1, LANES), const),
            pl.BlockSpec((1, 2 * D_MODEL), const),
            pl.BlockSpec((3, LANES, HEADS * AUX), const3),
            pl.BlockSpec((3, LANES, HEADS * AUX), const3),
            pl.BlockSpec((8, HEADS * AUX), const),
            pl.BlockSpec((FOX_WIDTH, LANES), const),
        ],
        out_specs=[
            pl.BlockSpec((tm, hb), row),
            pl.BlockSpec((tm, hb), row),
            pl.BlockSpec((None, None, HEADS * V_ROWS, tm), lambda b, i: (b, i, 0, 0)),
            pl.BlockSpec((tm, SSM_WIDTH), row),
            pl.BlockSpec((tm, 2 * D_MODEL), row),
            pl.BlockSpec((None, None, 8, LANES), lambda b, i: (b, i, 0, 0)),
        ],
        out_shape=[
            jax.ShapeDtypeStruct((t, hb), BF16),
            jax.ShapeDtypeStruct((t, hb), BF16),
            jax.ShapeDtypeStruct((bsz, nt, HEADS * V_ROWS, tm), BF16),
            jax.ShapeDtypeStruct((t, SSM_WIDTH), BF16),
            jax.ShapeDtypeStruct((t, 2 * D_MODEL), BF16),
            jax.ShapeDtypeStruct((bsz, nt, 8, LANES), F32),
        ],
        scratch_shapes=[pltpu.VMEM((1, LANES), F32)],
        compiler_params=_cparams(("arbitrary", "arbitrary")),
        name="proj",
    )(x2, g_mix, w1, bf_pad, b_gate, selq, selk, auxc, hsel)


def _aux_constants():
    selq = jnp.zeros((3, LANES, HEADS * AUX), F32)
    selk = jnp.zeros((3, LANES, HEADS * AUX), F32)
    auxc = jnp.zeros((8, HEADS * AUX), F32)
    hd = jnp.arange(HEADS)
    for p in range(3):
        selq = selq.at[p, hd, hd * AUX + p].set(1.0)
        selk = selk.at[p, hd, hd * AUX + 3 + p].set(-1.0)
        auxc = auxc.at[0, hd * AUX + 3 + p].set(1.0)
        auxc = auxc.at[1, hd * AUX + p].set(1.0)
    auxc = auxc.at[2, hd * AUX].set(1.0)
    return selq.astype(BF16), selk.astype(BF16), auxc


def _attn_kernel(first_ref, q_ref, k_ref, vt_ref, o_ref, s_scr, m_scr, acc_scr, *, tile, heads):
    i = pl.program_id(2)
    j0 = first_ref[(pl.program_id(0) * pl.num_programs(1) + pl.program_id(1)) * pl.num_programs(2) + i]
    cnt = i - j0
    nt = (((1,), (1,)), ((), ()))

    m_scr[...] = jnp.full(m_scr.shape, -jnp.inf, F32)
    acc_scr[...] = jnp.zeros(acc_scr.shape, F32)

    def qk(slot, t):
        r0 = pl.multiple_of(t * tile, tile)
        for g in range(heads):
            s_scr[slot, g] = lax.dot_general(
                k_ref[pl.ds(r0, tile), g * HEAD_BLOCK:(g + 1) * HEAD_BLOCK],
                q_ref[:, g * HEAD_BLOCK:(g + 1) * HEAD_BLOCK], nt, preferred_element_type=F32)

    def softmax(slot, masked):
        ps, alphas = [], []
        for g in range(heads):
            s = s_scr[slot, g]
            if masked:
                kpos = lax.broadcasted_iota(jnp.int32, s.shape, 0)
                qpos = lax.broadcasted_iota(jnp.int32, s.shape, 1)
                s = jnp.where(kpos <= qpos, s, -jnp.inf)
            m = m_scr[g]
            m_new = jnp.maximum(m, jnp.max(s, axis=0, keepdims=True))
            ps.append(jnp.exp2(s - m_new).astype(BF16))
            alphas.append(jnp.exp2(m - m_new))
            m_scr[g] = m_new
        return ps, alphas

    def pv(pa, t):
        ps, alphas = pa
        for g in range(heads):
            acc_scr[g] = (alphas[g] * acc_scr[g]
                          + jnp.dot(vt_ref[t, g * V_ROWS:(g + 1) * V_ROWS, :], ps[g],
                                    preferred_element_type=F32))

    @pl.when(cnt == 2)
    def _():
        qk(0, j0)
        qk(1, j0 + 1)
        pa0 = softmax(0, False)
        qk(2, i)
        pv(pa0, j0)
        pv(softmax(1, False), j0 + 1)
        pv(softmax(2, True), i)

    @pl.when(cnt != 2)
    def _():
        qk(0, j0)

        def body(tt, carry):
            t0 = j0 + 2 * tt
            qk(1, t0 + 1)
            pa0 = softmax(0, False)
            qk(0, t0 + 2)
            pv(pa0, t0)
            pv(softmax(1, False), t0 + 1)
            return carry

        lax.fori_loop(0, lax.shift_right_logical(cnt, 1), body, 0)

        @pl.when((cnt & 1) == 0)
        def _():
            pv(softmax(0, True), i)

        @pl.when((cnt & 1) == 1)
        def _():
            qk(1, i)
            pv(softmax(0, False), i - 1)
            pv(softmax(1, True), i)

    for pair in range(heads // 2):
        a0 = acc_scr[2 * pair]
        a1 = acc_scr[2 * pair + 1]
        o2 = jnp.concatenate([a0[:HEAD_DIM] / a0[HEAD_DIM:HEAD_DIM + 1],
                              a1[:HEAD_DIM] / a1[HEAD_DIM:HEAD_DIM + 1]], axis=0)
        o_ref[:, pair * LANES:(pair + 1) * LANES] = o2.T.astype(BF16)


def _attn(first, qx, kx, vt, tile, heads):
    b, s, _ = qx.shape
    n = s // tile
    assert vt.shape == (b, n, HEADS * V_ROWS, tile)
    hb = heads * HEAD_BLOCK
    return pl.pallas_call(
        functools.partial(_attn_kernel, tile=tile, heads=heads),
        grid_spec=pltpu.PrefetchScalarGridSpec(
            num_scalar_prefetch=1,
            grid=(b, HEADS // heads, n),
            in_specs=[
                pl.BlockSpec((None, tile, hb), lambda bi, hi, i, first_ref: (bi, i, hi)),
                pl.BlockSpec((None, s, hb), lambda bi, hi, i, first_ref: (bi, 0, hi)),
                pl.BlockSpec((None, n, heads * V_ROWS, tile), lambda bi, hi, i, first_ref: (bi, 0, hi, 0)),
            ],
            out_specs=pl.BlockSpec((None, tile, heads * HEAD_DIM), lambda bi, hi, i, first_ref: (bi, i, hi)),
            scratch_shapes=[
                pltpu.VMEM((3, heads, tile, tile), F32),
                pltpu.VMEM((heads, 1, tile), F32),
                pltpu.VMEM((heads, V_ROWS, tile), F32),
            ],
        ),
        out_shape=jax.ShapeDtypeStruct((b, s, FOX_WIDTH), BF16),
        compiler_params=_cparams(("arbitrary", "arbitrary", "arbitrary")),
        name="attn",
    )(first, qx, kx, vt)


def _first_key_tile(stats, heads):
    st = stats[..., :HEADS]
    slack = 1.0 + 2.0 ** -6
    qn = jnp.sqrt(st[:, :, 0]) * slack
    kn = jnp.sqrt(st[:, :, 1]) * slack
    self_min, c_max, c_min = st[:, :, 2], st[:, :, 3], st[:, :, 4]
    upper = qn[:, :, None] * kn[:, None, :] + c_max[:, :, None] - c_min[:, None, :]
    lower = self_min - (slack - 1.0) * qn * kn
    needed = upper - lower[:, :, None] > -SKIP_LOG2
    n = st.shape[1]
    idx = jnp.arange(n, dtype=jnp.int32)
    needed = needed | (idx[None, None, :, None] >= idx[None, :, None, None])
    needed = needed.reshape(needed.shape[:3] + (HEADS // heads, heads)).any(axis=-1)
    first = jnp.min(jnp.where(needed, idx[None, None, :, None], n), axis=2)
    return first.transpose(0, 2, 1).reshape(-1).astype(jnp.int32)


def _ssm_kernel(u_ref, bm_ref, cm_ref, ar_ref, ai_ref, d_ref, y_ref, io_scr, x_scr, s_scr, *, tb):
    rows = tb * 8
    nb = u_ref.shape[0]

    @pl.when(pl.program_id(0) == 0)
    def _():
        s_scr[...] = jnp.zeros_like(s_scr)

    for g in range(SUPER):
        for b in range(nb):
            ub = u_ref[b, :, g * SG_CH:(g + 1) * SG_CH].astype(F32)
            io_scr[g, pl.ds(b, tb, stride=8), :] = ub
            io_scr[g, pl.ds(b + nb, tb, stride=8), :] = ub
    is_re = (lax.broadcasted_iota(jnp.int32, (rows, SG_CH), 0) % 8) < 4
    zero = jnp.zeros((rows, SG_CH), BF16)
    for g in range(SUPER):
        ug = io_scr[g].astype(BF16)
        lhs = jnp.concatenate([jnp.where(is_re, ug, zero), jnp.where(is_re, zero, ug)], axis=1)
        x_scr[:, g * SG_ST:(g + 1) * SG_ST] = jnp.dot(lhs, bm_ref[g], preferred_element_type=F32)

    half = 2 * SG_ST
    for c in range(SUPER * SG_ST // half):
        lo = c * half
        ar = ar_ref[:, lo:lo + half]
        ai = ai_ref[:, lo:lo + half]

        def body(t, s, lo=lo, ar=ar, ai=ai):
            r0 = pl.multiple_of(t * 8, 8)
            s = ar * s + ai * pltpu.roll(s, 4, axis=0) + x_scr[pl.ds(r0, 8), lo:lo + half]
            x_scr[pl.ds(r0, 8), lo:lo + half] = s
            return s

        s_scr[:, lo:lo + half] = lax.fori_loop(0, tb, body, s_scr[:, lo:lo + half], unroll=4)

    for g in range(SUPER):
        st = x_scr[:, g * SG_ST:(g + 1) * SG_ST].astype(BF16)
        o = jnp.dot(st, cm_ref[g], preferred_element_type=F32)
        y = o[:, :SG_CH] + pltpu.roll(o[:, SG_CH:], rows - 4, axis=0)
        io_scr[g] = y + d_ref[:, g * SG_CH:(g + 1) * SG_CH] * io_scr[g]
        for b in range(nb):
            y_ref[b, :, g * SG_CH:(g + 1) * SG_CH] = io_scr[g, pl.ds(b, tb, stride=8), :].astype(BF16)


def _ssm(u, bm, cm, ar, ai, d, tb):
    bsz, seq, _ = u.shape
    rows = tb * 8
    return pl.pallas_call(
        functools.partial(_ssm_kernel, tb=tb),
        grid=(seq // tb,),
        in_specs=[
            pl.BlockSpec((bsz, tb, SSM_WIDTH), lambda i: (0, i, 0)),
            pl.BlockSpec((SUPER, 2 * SG_CH, SG_ST), lambda i: (0, 0, 0)),
            pl.BlockSpec((SUPER, SG_ST, 2 * SG_CH), lambda i: (0, 0, 0)),
            pl.BlockSpec((8, SUPER * SG_ST), lambda i: (0, 0)),
            pl.BlockSpec((8, SUPER * SG_ST), lambda i: (0, 0)),
            pl.BlockSpec((1, SSM_WIDTH), lambda i: (0, 0)),
        ],
        out_specs=pl.BlockSpec((bsz, tb, SSM_WIDTH), lambda i: (0, i, 0)),
        out_shape=jax.ShapeDtypeStruct((bsz, seq, SSM_WIDTH), BF16),
        scratch_shapes=[
            pltpu.VMEM((SUPER, rows, SG_CH), F32),
            pltpu.VMEM((rows, SUPER * SG_ST), F32),
            pltpu.VMEM((8, SUPER * SG_ST), F32),
        ],
        compiler_params=_cparams(("arbitrary",)),
        name="ssm",
    )(u, bm, cm, ar, ai, d)


def _merge_kernel(x_ref, o_ref, ys_ref, gate_ref, woa_ref, wglu_ref, bglu_ref, wob_ref, wout_ref,
                  x1_ref):
    y_a = jnp.dot(o_ref[...], woa_ref[...], preferred_element_type=F32)
    z = jax.nn.gelu(ys_ref[...].astype(F32))
    zg = jnp.dot(z.astype(BF16), wglu_ref[...], preferred_element_type=F32) + bglu_ref[...]
    z = z * jax.nn.sigmoid(zg)
    y_b = jnp.dot(z.astype(BF16), wob_ref[...], preferred_element_type=F32)
    g0 = gate_ref[:, :D_MODEL].astype(F32)
    g1 = gate_ref[:, D_MODEL:].astype(F32)
    merged = (g0 * y_a + g1 * y_b).astype(BF16)
    x1_ref[...] = x_ref[...] + jnp.dot(merged, wout_ref[...], preferred_element_type=F32)


def _merge(x2, o, ys, gates, woa, wglu, bglu, wob, wout, tm):
    t = x2.shape[0]
    row = lambda i: (i, 0)
    const = lambda i: (0, 0)
    return pl.pallas_call(
        _merge_kernel,
        grid=(t // tm,),
        in_specs=[
            pl.BlockSpec((tm, D_MODEL), row),
            pl.BlockSpec((tm, FOX_WIDTH), row),
            pl.BlockSpec((tm, SSM_WIDTH), row),
            pl.BlockSpec((tm, 2 * D_MODEL), row),
            pl.BlockSpec((FOX_WIDTH, D_MODEL), const),
            pl.BlockSpec((SSM_WIDTH, SSM_WIDTH), const),
            pl.BlockSpec((1, SSM_WIDTH), const),
            pl.BlockSpec((SSM_WIDTH, D_MODEL), const),
            pl.BlockSpec((D_MODEL, D_MODEL), const),
        ],
        out_specs=pl.BlockSpec((tm, D_MODEL), row),
        out_shape=jax.ShapeDtypeStruct((t, D_MODEL), F32),
        compiler_params=_cparams(("arbitrary",)),
        name="merge",
    )(x2, o, ys, gates, woa, wglu, bglu, wob, wout)


def _route(lt):
    ri = lax.broadcasted_iota(jnp.int32, lt.shape, 0)
    r8 = lax.broadcasted_iota(jnp.int32, (8, lt.shape[1]), 0)
    neg = -jnp.inf
    big = jnp.int32(ROUTER_COLS)
    lg = jnp.where(ri < N_GROUPS, lt, neg)
    gmax = jnp.max(lg, axis=0, keepdims=True)
    p_sel = 1.0 / jnp.sum(jnp.exp(lg - gmax), axis=0, keepdims=True)
    g_idx = jnp.min(jnp.where(lg == gmax, ri, big), axis=0, keepdims=True)
    lo = N_GROUPS + g_idx * EXPERTS_PER_GROUP
    le = jnp.where((ri >= lo) & (ri < lo + EXPERTS_PER_GROUP), lt, neg)
    v1 = jnp.max(le, axis=0, keepdims=True)
    i1 = jnp.min(jnp.where(le == v1, ri, big), axis=0, keepdims=True)
    le2 = jnp.where(ri == i1, neg, le)
    v2 = jnp.max(le2, axis=0, keepdims=True)
    i2 = jnp.min(jnp.where(le2 == v2, ri, big), axis=0, keepdims=True)
    e2 = jnp.exp(v2 - v1)
    w1 = p_sel / (1.0 + e2)
    w2 = p_sel * e2 / (1.0 + e2)
    cw = jnp.where(r8 == i1 - lo, w1, 0.0) + jnp.where(r8 == i2 - lo, w2, 0.0)
    onehot = jnp.where(r8 == g_idx, 1.0, 0.0)
    return cw, onehot


def _route_kernel(x1_ref, gf_ref, wr_ref, br_ref, utri_ref, hs_ref, cws_ref, pos_ref, gid_ref):
    tmb = x1_ref.shape[0]
    mp = hs_ref.shape[0]
    h = _rms(x1_ref[...], gf_ref[...])
    h_hi = h.astype(BF16)
    h_lo = (h - h_hi.astype(F32)).astype(BF16)
    l_hi = jnp.dot(h_hi, jnp.concatenate([wr_ref[0], wr_ref[1]], axis=1), preferred_element_type=F32)
    logits = (l_hi[:, :ROUTER_COLS] + l_hi[:, ROUTER_COLS:]
              + jnp.dot(h_lo, wr_ref[0], preferred_element_type=F32)) + br_ref[...]
    cw4, onehot = _route(logits.T[:32])

    r8 = lax.broadcasted_iota(jnp.int32, (8, ROUTER_COLS), 0)
    counts = jnp.broadcast_to(jnp.sum(onehot, axis=1, keepdims=True), (8, ROUTER_COLS))
    nch = jnp.floor((counts + (MOE_CHUNK - 1)) * (1.0 / MOE_CHUNK))
    inc = nch
    for sh in (1, 2):
        inc = inc + jnp.where(r8 >= sh, pltpu.roll(inc, sh, axis=0), 0.0)
    start = (inc - nch) * MOE_CHUNK
    rank = jnp.dot(onehot.astype(BF16), utri_ref[...], preferred_element_type=F32)
    pos_row = jnp.sum(onehot * (start[:, 0:1] + rank), axis=0, keepdims=True)
    pos_ref[...] = jnp.broadcast_to(pos_row, (ROUTER_COLS, tmb)).T
    prow = lax.broadcasted_iota(jnp.int32, (mp, tmb), 0).astype(F32)
    perm = jnp.where(prow == pos_row, 1.0, 0.0).astype(BF16)
    hs_ref[...] = jnp.dot(perm, h_hi, preferred_element_type=F32).astype(BF16)
    cw_hi = cw4.astype(BF16).astype(F32)
    cw_t = jnp.concatenate([cw_hi, cw4 - cw_hi, jnp.zeros((ROUTER_COLS - 16, tmb), F32)], axis=0).astype(BF16)
    cw2 = lax.dot_general(perm, cw_t, (((1,), (1,)), ((), ())), preferred_element_type=F32)
    cws_ref[...] = cw2 + pltpu.roll(cw2, ROUTER_COLS - 8, axis=1)
    chunk = lax.broadcasted_iota(jnp.int32, (8, ROUTER_COLS), 1).astype(F32)
    gid = jnp.zeros((8, ROUTER_COLS), F32)
    for grp in range(N_GROUPS):
        end = jnp.sum(jnp.where(r8 == grp, inc, 0.0), axis=0, keepdims=True)
        gid = gid + jnp.where(chunk >= end, 1.0, 0.0)
    gid_ref[...] = gid.astype(jnp.int32)


def _route_call(x1, g_ffn, wr, br, utri, tmb, mp):
    t = x1.shape[0]
    nb = t // tmb
    return pl.pallas_call(
        _route_kernel,
        grid=(nb,),
        in_specs=[
            pl.BlockSpec((tmb, D_MODEL), lambda i: (i, 0)),
            pl.BlockSpec((1, D_MODEL), lambda i: (0, 0)),
            pl.BlockSpec((2, D_MODEL, ROUTER_COLS), lambda i: (0, 0, 0)),
            pl.BlockSpec((1, ROUTER_COLS), lambda i: (0, 0)),
            pl.BlockSpec((tmb, tmb), lambda i: (0, 0)),
        ],
        out_specs=[
            pl.BlockSpec((mp, D_MODEL), lambda i: (i, 0)),
            pl.BlockSpec((mp, ROUTER_COLS), lambda i: (i, 0)),
            pl.BlockSpec((tmb, ROUTER_COLS), lambda i: (i, 0)),
            pl.BlockSpec((None, 8, ROUTER_COLS), lambda i: (i, 0, 0)),
        ],
        out_shape=[
            jax.ShapeDtypeStruct((nb * mp, D_MODEL), BF16),
            jax.ShapeDtypeStruct((nb * mp, ROUTER_COLS), F32),
            jax.ShapeDtypeStruct((t, ROUTER_COLS), F32),
            jax.ShapeDtypeStruct((nb, 8, ROUTER_COLS), jnp.int32),
        ],
        compiler_params=_cparams(("arbitrary",)),
        name="route",
    )(x1, g_ffn, wr, br, utri)


def _experts_kernel(order_ref, gids_ref, hs_ref, cws_ref, wg_ref, wu_ref, wd_ref, ys_ref):
    valid = gids_ref[pl.program_id(0)] < N_GROUPS

    @pl.when(valid)
    def _():
        x = hs_ref[...]
        cw = cws_ref[...]
        acts = []
        for j in range(EXPERTS_PER_GROUP):
            hg = jnp.dot(x, wg_ref[j], preferred_element_type=F32)
            hu = jnp.dot(x, wu_ref[j], preferred_element_type=F32)
            acts.append((hg * jax.nn.sigmoid(hg) * hu * cw[:, j:j + 1]).astype(BF16))
        act = jnp.concatenate(acts, axis=1)
        ys_ref[...] = jnp.dot(act, wd_ref[...], preferred_element_type=F32).astype(BF16)

    @pl.when(jnp.logical_not(valid))
    def _():
        ys_ref[...] = jnp.zeros(ys_ref.shape, BF16)


def _experts_call(order, gids, hs, cws, wg, wu, wd):
    n_chunks = order.shape[0]
    ff = EXPERTS_PER_GROUP * EXPERT_FF
    grp = lambda s, order_ref, gids_ref: (jnp.minimum(gids_ref[s], N_GROUPS - 1), 0, 0)
    rows = lambda s, order_ref, gids_ref: (order_ref[s], 0)
    return pl.pallas_call(
        _experts_kernel,
        grid_spec=pltpu.PrefetchScalarGridSpec(
            num_scalar_prefetch=2,
            grid=(n_chunks,),
            in_specs=[
                pl.BlockSpec((MOE_CHUNK, D_MODEL), rows),
                pl.BlockSpec((MOE_CHUNK, ROUTER_COLS), rows),
                pl.BlockSpec((EXPERTS_PER_GROUP, D_MODEL, EXPERT_FF), grp),
                pl.BlockSpec((EXPERTS_PER_GROUP, D_MODEL, EXPERT_FF), grp),
                pl.BlockSpec((None, ff, D_MODEL), grp),
            ],
            out_specs=pl.BlockSpec((MOE_CHUNK, D_MODEL), rows),
        ),
        out_shape=jax.ShapeDtypeStruct(hs.shape, BF16),
        compiler_params=_cparams(("arbitrary",)),
        name="experts",
    )(order, gids, hs, cws, wg, wu, wd)


def _combine_kernel(x1_ref, ys_ref, pos_ref, gfin_ref, out_ref):
    mp = ys_ref.shape[0]
    pos = pos_ref[...]
    lane = lax.broadcasted_iota(jnp.int32, pos.shape, 1).astype(F32)
    sel = jnp.concatenate(
        [jnp.where(pos == lane + float(c * ROUTER_COLS), 1.0, 0.0).astype(BF16)
         for c in range(mp // ROUTER_COLS)], axis=1)
    moe = jnp.dot(sel, ys_ref[...], preferred_element_type=F32)
    out_ref[...] = _rms(x1_ref[...] + moe, gfin_ref[...])


def _combine_call(x1, ys, pos, g_final, tmb, mp):
    t = x1.shape[0]
    return pl.pallas_call(
        _combine_kernel,
        grid=(t // tmb,),
        in_specs=[
            pl.BlockSpec((tmb, D_MODEL), lambda i: (i, 0)),
            pl.BlockSpec((mp, D_MODEL), lambda i: (i, 0)),
            pl.BlockSpec((tmb, ROUTER_COLS), lambda i: (i, 0)),
            pl.BlockSpec((1, D_MODEL), lambda i: (0, 0)),
        ],
        out_specs=pl.BlockSpec((tmb, D_MODEL), lambda i: (i, 0)),
        out_shape=jax.ShapeDtypeStruct((t, D_MODEL), F32),
        compiler_params=_cparams(("arbitrary",)),
        name="combine",
    )(x1, ys, pos, g_final)


def _ssm_params(lambda_re, lambda_im, log_step, b_re, b_im, c_re, c_im):
    dt = jnp.exp(log_step)[:, None]
    mag = jnp.exp(lambda_re * dt)
    a_re = mag * jnp.cos(lambda_im * dt)
    a_im = mag * jnp.sin(lambda_im * dt)
    den = lambda_re * lambda_re + lambda_im * lambda_im
    num_re = a_re - 1.0
    z_re = (num_re * lambda_re + a_im * lambda_im) / den
    z_im = (a_im * lambda_re - num_re * lambda_im) / den
    bb_re = z_re[..., None] * b_re - z_im[..., None] * b_im
    bb_im = z_re[..., None] * b_im + z_im[..., None] * b_re
    gl = SSM_GROUPS // SUPER
    eye = jnp.eye(gl, dtype=F32)

    def in_blk(bb):
        t = bb.reshape(SUPER, gl, SSM_STATE, SSM_GROUP).transpose(0, 1, 3, 2)
        t = t[:, :, :, None, :] * eye[None, :, None, :, None]
        return t.reshape(SUPER, gl * SSM_GROUP, gl * SSM_STATE)

    def out_blk(cc):
        t = cc.reshape(SUPER, gl, SSM_GROUP, SSM_STATE).transpose(0, 1, 3, 2)
        t = t[:, :, :, None, :] * eye[None, :, None, :, None]
        return t.reshape(SUPER, gl * SSM_STATE, gl * SSM_GROUP)

    bm = jnp.concatenate([in_blk(bb_re), in_blk(bb_im)], axis=1).astype(BF16)
    cm = jnp.concatenate([out_blk(c_re), out_blk(-c_im)], axis=2).astype(BF16)
    ar_row = a_re.reshape(1, SSM_GROUPS * SSM_STATE)
    ai_row = a_im.reshape(1, SSM_GROUPS * SSM_STATE)
    ar = jnp.broadcast_to(ar_row, (8, SSM_GROUPS * SSM_STATE))
    ai = jnp.concatenate([jnp.broadcast_to(-ai_row, (4, ai_row.shape[1])),
                          jnp.broadcast_to(ai_row, (4, ai_row.shape[1]))], axis=0)
    return bm, cm, ar, ai


def _pick(n, pref):
    return pref if n % pref == 0 else n


def _mixers(x, g_mix, w_in, b_forget, b_gate, w_out_a, lambda_re, lambda_im, log_step,
            ssm_b_re, ssm_b_im, ssm_c_re, ssm_c_im, ssm_d, w_glu, b_glu, w_out_b, w_out):
    bsz, seq, _ = x.shape
    assert bsz == 4, "the SSM kernel packs (re/im) x 4 batches onto the 8 sublanes"
    t = bsz * seq
    layer = 0
    x2 = x.reshape(t, D_MODEL)

    w = w_in[layer]
    fw = FOX_WIDTH
    wq, wk, wv = w[:, :fw], w[:, fw:2 * fw], w[:, 2 * fw:3 * fw]
    wf = w[:, 3 * fw:3 * fw + HEADS]
    wu = w[:, 3 * fw + HEADS:3 * fw + HEADS + SSM_WIDTH]
    wgt = w[:, 3 * fw + HEADS + SSM_WIDTH:]
    scale = LOG2E / math.sqrt(HEAD_DIM)
    w1 = jnp.concatenate(
        [wq * scale, wk, wv, wu, wgt, jnp.pad(wf, ((0, 0), (0, LANES - HEADS)))], axis=1).astype(BF16)
    bf_pad = jnp.pad(b_forget[layer], (0, LANES - HEADS)).reshape(1, LANES)
    selq, selk, auxc = _aux_constants()
    hsel = (jnp.arange(FOX_WIDTH)[:, None] // HEAD_DIM == jnp.arange(LANES)[None, :]).astype(BF16)

    tm = _pick(seq, ATTN_TILE)
    qx, kx, vt, u, gates, stats = _proj(x2, g_mix[layer].reshape(1, D_MODEL), w1, bf_pad,
                                        b_gate[layer].reshape(1, 2 * D_MODEL), selq, selk, auxc, hsel, tm, seq)

    hb = HEADS * HEAD_BLOCK
    o = _attn(_first_key_tile(stats, ATTN_HEADS), qx.reshape(bsz, seq, hb), kx.reshape(bsz, seq, hb), vt,
              tm, ATTN_HEADS).reshape(t, FOX_WIDTH)

    bm, cm, ar, ai = _ssm_params(lambda_re[layer], lambda_im[layer], log_step[layer],
                                 ssm_b_re[layer], ssm_b_im[layer], ssm_c_re[layer], ssm_c_im[layer])
    tb = _pick(seq, 128)
    ys = _ssm(u.reshape(bsz, seq, SSM_WIDTH), bm, cm, ar, ai, ssm_d[layer].reshape(1, SSM_WIDTH),
              tb).reshape(t, SSM_WIDTH)

    x1 = _merge(x2, o, ys, gates, w_out_a[layer].astype(BF16), w_glu[layer].astype(BF16),
                b_glu[layer].reshape(1, SSM_WIDTH), w_out_b[layer].astype(BF16),
                w_out[layer].astype(BF16), tm)
    return x1, o, ys


def kernel(x, g_mix, w_in, b_forget, b_gate, w_out_a, lambda_re, lambda_im, log_step, ssm_b_re, ssm_b_im, ssm_c_re, ssm_c_im, ssm_d, w_glu, b_glu, w_out_b, w_out, g_ffn, w_router_group, b_router_group, w_router_expert, b_router_expert, w_exp_gate, w_exp_up, w_exp_down, g_final):
    bsz, seq, _ = x.shape
    t = bsz * seq
    layer = 0
    x1, _, _ = _mixers(x, g_mix, w_in, b_forget, b_gate, w_out_a, lambda_re, lambda_im, log_step,
                       ssm_b_re, ssm_b_im, ssm_c_re, ssm_c_im, ssm_d, w_glu, b_glu, w_out_b, w_out)

    wr_f = jnp.concatenate([w_router_group[layer], w_router_expert[layer]], axis=1)
    wr_f = jnp.pad(wr_f, ((0, 0), (0, ROUTER_COLS - N_GROUPS - N_EXPERTS)))
    wr_hi = wr_f.astype(BF16)
    wr_lo = (wr_f - wr_hi.astype(F32)).astype(BF16)
    wr = jnp.stack([wr_hi, wr_lo])
    br = jnp.pad(jnp.concatenate([b_router_group[layer], b_router_expert[layer]]),
                 (0, ROUTER_COLS - N_GROUPS - N_EXPERTS)).reshape(1, ROUTER_COLS)
    ff = EXPERTS_PER_GROUP * EXPERT_FF
    wd = w_exp_down[layer].reshape(N_GROUPS, ff, D_MODEL).astype(BF16)

    tmb = _pick(t, MOE_BLOCK)
    mp = tmb + N_GROUPS * MOE_CHUNK
    utri = jnp.triu(jnp.ones((tmb, tmb), BF16), 1)
    hs, cws, pos, gid = _route_call(x1, g_ffn[layer].reshape(1, D_MODEL), wr, br, utri, tmb, mp)

    cpb = mp // MOE_CHUNK
    gid_flat = gid[:, 0, :cpb].reshape(-1)
    n_chunks = gid_flat.shape[0]
    key = jnp.sort(gid_flat * n_chunks + jnp.arange(n_chunks, dtype=jnp.int32))
    ys = _experts_call(key % n_chunks, key // n_chunks, hs, cws, w_exp_gate[layer].astype(BF16),
                       w_exp_up[layer].astype(BF16), wd)
    out = _combine_call(x1, ys, pos, g_final.reshape(1, D_MODEL), tmb, mp)
    return out.reshape(bsz, seq, D_MODEL)
```
